```python
import math
import jax, jax.numpy as jnp
from jax import lax
import numpy as np

D_MODEL = 1024
BATCH = 16
SEQ = 256
DEPTH = 4
DEC_BATCH = 8
DEC_SEQ = 4096
PAST_LEN = 256

GRID_W = 64
HEAD_DIM = 64
N_ATT_HEADS = 4
ATT_WIDTH = N_ATT_HEADS * 2 * HEAD_DIM
N_FOURIER_GROUPS = 4
FOURIER_GROUP_DIM = 64
FOURIER_WIDTH = N_FOURIER_GROUPS * FOURIER_GROUP_DIM
N_GMLP_HEADS = 4
GMLP_HEAD_DIM = 64
GMLP_WIDTH = N_GMLP_HEADS * GMLP_HEAD_DIM
CHUNK = 128
MIX_WIDTH = ATT_WIDTH + FOURIER_WIDTH + GMLP_WIDTH
IN_WIDTH = 4 * ATT_WIDTH + 2 * FOURIER_WIDTH + 3 * GMLP_WIDTH
ROPE_THETA = 10000.0
DEEPNORM_ALPHA = (2.0 * DEPTH) ** 0.25
DEEPNORM_BETA = (8.0 * DEPTH) ** -0.25
LN_EPS = 1e-5
RMS_EPS = 1e-5
BLOCK_Q = 128

kernel_name = 'hybrid_diffattn_fnet_gmlp_flow_step'


def _layernorm(x):
    xf = x.astype(jnp.float32)
    mu = jnp.mean(xf, axis=-1, keepdims=True)
    var = jnp.mean(jnp.square(xf - mu), axis=-1, keepdims=True)
    return (xf - mu) * lax.rsqrt(var + LN_EPS)


def _adaln_input(x, mod):
    shift, scale, gate = jnp.split(mod, 3, axis=-1)
    h = _layernorm(x) * (1.0 + scale[:, None, :]) + shift[:, None, :]
    return h.astype(x.dtype), gate[:, None, :]


def _split_proj(z):
    sizes = (ATT_WIDTH,) * 4 + (FOURIER_WIDTH,) * 2 + (GMLP_WIDTH,) * 3
    idx = np.cumsum(sizes)[:-1].tolist()
    return jnp.split(z, idx, axis=-1)


def _heads_qk(t):
    b, l, _ = t.shape
    return t.reshape(b, l, N_ATT_HEADS, 2, HEAD_DIM).transpose(0, 2, 1, 3, 4)


def _heads_v(t):
    b, l, _ = t.shape
    return t.reshape(b, l, N_ATT_HEADS, 2 * HEAD_DIM).transpose(0, 2, 1, 3)


def _lambda(lq1, lk1, lq2, lk2, lam_init):
    f = jnp.float32
    return (jnp.exp(jnp.sum(lq1.astype(f) * lk1.astype(f)))
            - jnp.exp(jnp.sum(lq2.astype(f) * lk2.astype(f))) + lam_init)


def _diff_attention(q, k, v, lam, subln_w, lam_init):
    s = jnp.einsum('bhqmd,bhkmd->bhmqk', q, k,
                   preferred_element_type=jnp.float32) * (HEAD_DIM ** -0.5)
    p = jax.nn.softmax(s, axis=-1)
    a = p[:, :, 0] - lam * p[:, :, 1]
    o = jnp.einsum('bhqk,bhke->bhqe', a, v.astype(jnp.float32))
    o = (o * lax.rsqrt(jnp.mean(jnp.square(o), axis=-1, keepdims=True) + RMS_EPS)
         * subln_w.astype(jnp.float32) * (1.0 - lam_init))
    return o.astype(v.dtype)


def _axial_rope_tables(rows):
    n_freq = HEAD_DIM // 4
    row = jnp.repeat(jnp.arange(rows, dtype=jnp.float32), GRID_W)
    col = jnp.tile(jnp.arange(GRID_W, dtype=jnp.float32), rows)
    inv = ROPE_THETA ** (-jnp.arange(n_freq, dtype=jnp.float32) / n_freq)
    ang_r = (row[:, None] * inv)[:, None, :]
    ang_c = (col[:, None] * inv)[:, None, :]
    return (jnp.cos(ang_r), jnp.sin(ang_r), jnp.cos(ang_c), jnp.sin(ang_c))


def _rot_half(x, cos, sin):
    x1, x2 = jnp.split(x, 2, axis=-1)
    return jnp.concatenate([x1 * cos - x2 * sin, x2 * cos + x1 * sin], axis=-1)


def _rope_2d(x, tables):
    cr, sr, cc, sc = tables
    xr, xc = jnp.split(x, 2, axis=-1)
    return jnp.concatenate([_rot_half(xr, cr, sr), _rot_half(xc, cc, sc)], axis=-1).astype(x.dtype)


def _fourier_mix(f, w_f):
    b, l, _ = f.shape
    fg = f.reshape(b, l, N_FOURIER_GROUPS, FOURIER_GROUP_DIM).astype(jnp.float32)
    re = jnp.real(jnp.fft.fft2(fg, axes=(1, 3), norm='ortho'))
    out = jnp.einsum('blgc,gce->blge', re, w_f.astype(jnp.float32))
    return out.reshape(b, l, FOURIER_WIDTH).astype(f.dtype)


def _gmlp_spatial(u, vm, w_s, b_s):
    b, l, _ = vm.shape
    n = l // CHUNK
    vn = _layernorm(vm.reshape(b, n, CHUNK, N_GMLP_HEADS, GMLP_HEAD_DIM))
    s = (jnp.einsum('hpq,bnqhc->bnphc', w_s.astype(jnp.float32), vn)
         + b_s.astype(jnp.float32).T[:, :, None])
    return (u.astype(jnp.float32) * s.reshape(b, l, GMLP_WIDTH)).astype(u.dtype)


def _local_branches(f, g_f, u, vm, g_m, w_f, w_s, b_s):
    return jnp.concatenate([_fourier_mix(f, w_f) * jax.nn.silu(g_f),
                            _gmlp_spatial(u, vm, w_s, b_s) * jax.nn.silu(g_m)], axis=-1)


def _finish(x, att, g_att, local_out, gate, w_out, ln_g, ln_b):
    mixed = jnp.concatenate([att * jax.nn.silu(g_att), local_out], axis=-1)
    y = mixed @ w_out
    r = DEEPNORM_ALPHA * x + gate * y
    return (_layernorm(r) * ln_g + ln_b).astype(x.dtype)


def _context_layer(x, c_ctx, lam_init, w_ada, b_ada, w_in, w_out, lq1, lk1, lq2, lk2,
                   subln_w, fourier_w, gmlp_ws, gmlp_bs, ln_g, ln_b):
    b, l, _ = x.shape
    mod = jax.nn.silu(c_ctx)[None, :] @ w_ada + b_ada
    h, gate = _adaln_input(x, mod)
    q, k, v, g_att, f, g_f, u, vm, g_m = _split_proj(h @ w_in)
    q, k, v = _heads_qk(q), _heads_qk(k), _heads_v(v)
    lam = _lambda(lq1, lk1, lq2, lk2, lam_init)
    att = _diff_attention(q, k, v, lam, subln_w, lam_init)
    att = att.transpose(0, 2, 1, 3).reshape(b, l, ATT_WIDTH)
    local_out = _local_branches(f, g_f, u, vm, g_m, fourier_w, gmlp_ws, gmlp_bs)
    x_new = _finish(x, att, g_att, local_out, gate, w_out, ln_g, ln_b)
    return x_new, k.reshape(b, N_ATT_HEADS, l, 2 * HEAD_DIM), v


def _latent_layer(x, c, ctx_k, ctx_v, tables, lam_init, w_ada, b_ada, w_in, w_out,
                  lq1, lk1, lq2, lk2, subln_w, fourier_w, gmlp_ws, gmlp_bs, ln_g, ln_b):
    b, l, _ = x.shape
    lc = ctx_k.shape[2]
    mod = jax.nn.silu(c) @ w_ada + b_ada
    h, gate = _adaln_input(x, mod)
    q, k, v, g_att, f, g_f, u, vm, g_m = _split_proj(h @ w_in)
    q = _rope_2d(_heads_qk(q), tables)
    k = _rope_2d(_heads_qk(k), tables)
    v = _heads_v(v)
    k_all = jnp.concatenate([ctx_k.reshape(b, N_ATT_HEADS, lc, 2, HEAD_DIM).astype(k.dtype), k], axis=2)
    v_all = jnp.concatenate([ctx_v.astype(v.dtype), v], axis=2)
    lam = _lambda(lq1, lk1, lq2, lk2, lam_init)
    n_blk = l // BLOCK_Q
    q_blocks = q.reshape(b, N_ATT_HEADS, n_blk, BLOCK_Q, 2, HEAD_DIM).transpose(2, 0, 1, 3, 4, 5)
    att = lax.map(lambda qb: _diff_attention(qb, k_all, v_all, lam, subln_w, lam_init), q_blocks)
    att = att.transpose(1, 0, 3, 2, 4).reshape(b, l, ATT_WIDTH)
    local_out = _local_branches(f, g_f, u, vm, g_m, fourier_w, gmlp_ws, gmlp_bs)
    return _finish(x, att, g_att, local_out, gate, w_out, ln_g, ln_b)


def setup_inputs(seed: int = 0) -> dict:
    key = jax.random.key(seed)
    ks = jax.random.split(key, 20)
    f32 = jnp.float32

    def nrm(k, shape, s):
        return jax.random.normal(k, shape, f32) * s

    return {
        'x_prompt': nrm(ks[0], (BATCH, SEQ, D_MODEL), 1.0),
        'x_sample': nrm(ks[1], (DEC_BATCH, DEC_SEQ, D_MODEL), 1.0),
        'c': nrm(ks[2], (DEC_BATCH, D_MODEL), 1.0),
        'cache_k': nrm(ks[3], (DEC_BATCH, DEPTH, N_ATT_HEADS, PAST_LEN, 2 * HEAD_DIM), 1.0),
        'cache_v': nrm(ks[4], (DEC_BATCH, DEPTH, N_ATT_HEADS, PAST_LEN, 2 * HEAD_DIM), 1.0),
        'c_ctx': nrm(ks[5], (D_MODEL,), 1.0),
        'w_ada': nrm(ks[6], (DEPTH, D_MODEL, 3 * D_MODEL), D_MODEL ** -0.5),
        'b_ada': nrm(ks[7], (DEPTH, 3 * D_MODEL), 0.02),
        'w_in': nrm(ks[8], (DEPTH, D_MODEL, IN_WIDTH), D_MODEL ** -0.5),
        'w_out': nrm(ks[9], (DEPTH, MIX_WIDTH, D_MODEL), DEEPNORM_BETA * MIX_WIDTH ** -0.5),
        'lam_q1': nrm(ks[10], (DEPTH, HEAD_DIM), 0.1),
        'lam_k1': nrm(ks[11], (DEPTH, HEAD_DIM), 0.1),
        'lam_q2': nrm(ks[12], (DEPTH, HEAD_DIM), 0.1),
        'lam_k2': nrm(ks[13], (DEPTH, HEAD_DIM), 0.1),
        'subln_w': 1.0 + nrm(ks[14], (DEPTH, 2 * HEAD_DIM), 0.02),
        'fourier_w': nrm(ks[15], (DEPTH, N_FOURIER_GROUPS, FOURIER_GROUP_DIM, FOURIER_GROUP_DIM), FOURIER_GROUP_DIM ** -0.5),
        'gmlp_ws': nrm(ks[16], (DEPTH, N_GMLP_HEADS, CHUNK, CHUNK), CHUNK ** -0.5),
        'gmlp_bs': nrm(ks[17], (DEPTH, N_GMLP_HEADS, CHUNK), 0.02),
        'ln_g': 1.0 + nrm(ks[18], (DEPTH, D_MODEL), 0.02),
        'ln_b': nrm(ks[19], (DEPTH, D_MODEL), 0.02),
    }


def reference(x_prompt, x_sample, c, cache_k, cache_v, c_ctx, w_ada, b_ada, w_in, w_out,
              lam_q1, lam_k1, lam_q2, lam_k2, subln_w, fourier_w, gmlp_ws, gmlp_bs, ln_g, ln_b):
    rows = x_sample.shape[1] // GRID_W
    tables = _axial_rope_tables(rows)

    xp = x_prompt
    ks_new, vs_new = [], []
    for l in range(DEPTH):
        lam_init = 0.8 - 0.6 * math.exp(-0.3 * l)
        xp, k_l, v_l = _context_layer(
            xp, c_ctx, lam_init, w_ada[l], b_ada[l], w_in[l], w_out[l],
            lam_q1[l], lam_k1[l], lam_q2[l], lam_k2[l], subln_w[l],
            fourier_w[l], gmlp_ws[l], gmlp_bs[l], ln_g[l], ln_b[l])
        ks_new.append(k_l)
        vs_new.append(v_l)
    new_k = jnp.stack(ks_new, axis=1)
    new_v = jnp.stack(vs_new, axis=1)

    xs = x_sample
    for l in range(DEPTH):
        lam_init = 0.8 - 0.6 * math.exp(-0.3 * l)
        xs = _latent_layer(
            xs, c, cache_k[:, l], cache_v[:, l], tables, lam_init,
            w_ada[l], b_ada[l], w_in[l], w_out[l],
            lam_q1[l], lam_k1[l], lam_q2[l], lam_k2[l], subln_w[l],
            fourier_w[l], gmlp_ws[l], gmlp_bs[l], ln_g[l], ln_b[l])

    return (xp, xs, new_k, new_v)
```

```python
import functools
import math

import numpy as np
import jax
import jax.numpy as jnp
from jax import lax
from jax.experimental import pallas as pl
from jax.experimental.pallas import tpu as pltpu

F32 = jnp.float32
BF16 = jnp.bfloat16

GRID_W = 64
HEAD_DIM = 64
N_ATT_HEADS = 4
HEAD_W = 2 * HEAD_DIM
ATT_WIDTH = N_ATT_HEADS * HEAD_W
N_FOURIER_GROUPS = 4
FOURIER_GROUP_DIM = 64
FOURIER_WIDTH = N_FOURIER_GROUPS * FOURIER_GROUP_DIM
N_GMLP_HEADS = 4
GMLP_HEAD_DIM = 64
GMLP_WIDTH = N_GMLP_HEADS * GMLP_HEAD_DIM
CHUNK = 128
ROPE_THETA = 10000.0
LN_EPS = 1e-5
RMS_EPS = 1e-5
QK_SCALE = HEAD_DIM ** -0.5

_SEG_NAMES = ("q", "k", "v", "g_att", "f", "g_f", "u", "vm", "g_m")
_SEG_WIDTHS = (ATT_WIDTH,) * 4 + (FOURIER_WIDTH,) * 2 + (GMLP_WIDTH,) * 3
_SEG_START = dict(zip(_SEG_NAMES, np.concatenate([[0], np.cumsum(_SEG_WIDTHS)[:-1]]).tolist()))
_SEG_WIDTH = dict(zip(_SEG_NAMES, _SEG_WIDTHS))

VMEM_LIMIT_BYTES = 52 * 1024 * 1024
MOD_ROWS = 16


def _cparams(*sem):
    return pltpu.CompilerParams(dimension_semantics=sem, vmem_limit_bytes=VMEM_LIMIT_BYTES)


def _silu(x):
    return x * jax.nn.sigmoid(x)


def _row_tile(l, want):
    t = min(l, want)
    assert l % t == 0
    return t


def _mod_kernel(c_ref, w_ref, b_ref, o_ref):
    s = _silu(c_ref[...])
    o_ref[0] = jnp.dot(s, w_ref[0], precision=lax.Precision.HIGHEST,
                       preferred_element_type=F32) + b_ref[0]


def _modulation(cvec, w_ada, b_ada):
    depth, d, d3 = w_ada.shape
    tn = 1024
    return pl.pallas_call(
        _mod_kernel,
        grid=(depth, d3 // tn),
        in_specs=[pl.BlockSpec((MOD_ROWS, d), lambda l, j: (0, 0)),
                  pl.BlockSpec((1, d, tn), lambda l, j: (l, 0, j)),
                  pl.BlockSpec((1, 1, tn), lambda l, j: (l, 0, j))],
        out_specs=pl.BlockSpec((1, MOD_ROWS, tn), lambda l, j: (l, 0, j)),
        out_shape=jax.ShapeDtypeStruct((depth, MOD_ROWS, d3), F32),
        name="adaln_mod",
        compiler_params=_cparams("arbitrary", "arbitrary"),
    )(cvec, w_ada, b_ada.reshape(depth, 1, d3))


def _chanmap_kernel(cc_ref, ss_ref, w_ref, a_ref, b_ref, *, scale):
    w = w_ref[0]
    a_ref[0] = (jnp.dot(cc_ref[...], w, precision=lax.Precision.HIGHEST,
                        preferred_element_type=F32) * scale).astype(BF16)
    b_ref[0] = (jnp.dot(ss_ref[...], w, precision=lax.Precision.HIGHEST,
                        preferred_element_type=F32) * scale).astype(BF16)


def _fourier_channel_maps(fourier_w, scale):
    depth = fourier_w.shape[0]
    n = np.arange(FOURIER_GROUP_DIM)
    ang = 2.0 * np.pi * ((n[:, None] * n[None, :]) % FOURIER_GROUP_DIM) / FOURIER_GROUP_DIM
    eye = np.eye(N_FOURIER_GROUPS)
    cc = jnp.asarray(np.kron(eye, np.cos(ang)), F32)
    ss = jnp.asarray(np.kron(eye, np.sin(ang)), F32)
    wbd = jnp.einsum("gh,lgce->lgche", jnp.eye(N_FOURIER_GROUPS, dtype=F32),
                     fourier_w.astype(F32)).reshape(depth, FOURIER_WIDTH, FOURIER_WIDTH)
    w_spec = pl.BlockSpec((1, FOURIER_WIDTH, FOURIER_WIDTH), lambda l: (l, 0, 0))
    c_spec = pl.BlockSpec((FOURIER_WIDTH, FOURIER_WIDTH), lambda l: (0, 0))
    shp = jax.ShapeDtypeStruct((depth, FOURIER_WIDTH, FOURIER_WIDTH), BF16)
    return pl.pallas_call(
        functools.partial(_chanmap_kernel, scale=scale),
        grid=(depth,),
        in_specs=[c_spec, c_spec, w_spec],
        out_specs=[w_spec, w_spec],
        out_shape=[shp, shp],
        name="fourier_chanmap",
        compiler_params=_cparams("arbitrary"),
    )(cc, ss, wbd)


def _dft_kernel(ca_ref, sa_ref, cb_ref, sb_ref, g_ref, *, l):
    ca, sa = ca_ref[0], sa_ref[0]
    cb, sb = cb_ref[...], sb_ref[...]
    norm = 1.0 / math.sqrt(l)
    g_ref[:, 0:l] = ((ca * cb - sa * sb) * norm).astype(BF16)
    g_ref[:, l:2 * l] = ((sa * cb + ca * sb) * -norm).astype(BF16)


def _dft_matrix(l):
    na = l // GRID_W
    k = jnp.arange(l, dtype=jnp.int32)
    a = jnp.arange(na, dtype=jnp.int32)
    b = jnp.arange(GRID_W, dtype=jnp.int32)
    ang_a = ((a[:, None] * k[None, :]) % na).astype(F32) * (2.0 * math.pi / na)
    ang_b = ((b[:, None] * k[None, :]) % l).astype(F32) * (2.0 * math.pi / l)
    ca, sa = jnp.cos(ang_a).reshape(na, 1, l), jnp.sin(ang_a).reshape(na, 1, l)
    cb, sb = jnp.cos(ang_b), jnp.sin(ang_b)
    a_spec = pl.BlockSpec((1, 1, l), lambda i: (i, 0, 0))
    b_spec = pl.BlockSpec((GRID_W, l), lambda i: (0, 0))
    return pl.pallas_call(
        functools.partial(_dft_kernel, l=l),
        grid=(na,),
        in_specs=[a_spec, a_spec, b_spec, b_spec],
        out_specs=pl.BlockSpec((GRID_W, 2 * l), lambda i: (i, 0)),
        out_shape=jax.ShapeDtypeStruct((l, 2 * l), BF16),
        name=f"dft_matrix_{l}",
        compiler_params=_cparams("arbitrary"),
    )(ca, sa, cb, sb)


def _inproj_kernel(*refs, rope, emit_kv):
    x_ref, mod_ref, w_ref, fa_ref, fb_ref = refs[:5]
    refs = refs[5:]
    if rope:
        cos_ref, sa_ref, sb_ref = refs[:3]
        refs = refs[3:]
    q_ref, k_ref, v_ref, ga_ref, uf_ref, gf_ref, u_ref, vm_ref, gm_ref = refs[:9]
    refs = refs[9:]

    x = x_ref[0]
    mu = jnp.mean(x, axis=-1, keepdims=True)
    xc = x - mu
    var = jnp.mean(xc * xc, axis=-1, keepdims=True)
    shift, scale = mod_ref[0, 0:1, :], mod_ref[0, 1:2, :]
    h = (xc * lax.rsqrt(var + LN_EPS) * (1.0 + scale) + shift).astype(BF16)

    def proj(name):
        c0 = _SEG_START[name]
        return jnp.dot(h, w_ref[:, c0:c0 + _SEG_WIDTH[name]], preferred_element_type=F32)

    def rotate(z):
        return (z * cos_ref[...] + pltpu.roll(z, HEAD_W - 16, 1) * sa_ref[...]
                + pltpu.roll(z, 16, 1) * sb_ref[...])

    zq, zk, zv = proj("q"), proj("k"), proj("v")
    for hh in range(N_ATT_HEADS):
        sl = slice(hh * HEAD_W, (hh + 1) * HEAD_W)
        qh, kh = zq[:, sl], zk[:, sl]
        if emit_kv:
            kf_ref, vf_ref = refs
            kf_ref[0, hh] = kh
            vf_ref[0, hh] = zv[:, sl]
        if rope:
            qh, kh = rotate(qh), rotate(kh)
        q_ref[0, :, sl] = (qh * QK_SCALE).astype(BF16)
        k_ref[0, :, sl] = kh.astype(BF16)
    v_ref[0] = zv.astype(BF16)
    ga_ref[0] = proj("g_att")
    f = proj("f").astype(BF16)
    uf_ref[0, 0] = jnp.dot(f, fa_ref[0], preferred_element_type=F32).astype(BF16)
    uf_ref[0, 1] = jnp.dot(f, fb_ref[0], preferred_element_type=F32).astype(BF16)
    gf_ref[0] = proj("g_f")
    u_ref[0] = proj("u")
    vm_ref[0] = proj("vm")
    gm_ref[0] = proj("g_m")


def _in_projection(x, mod3, per_batch_mod, w_in, fa, fb, layer, rope_tabs, emit_kv, tm):
    b, l, d = x.shape
    n_in = w_in.shape[1]
    tm = _row_tile(l, tm)
    row = lambda w: pl.BlockSpec((1, tm, w), lambda bi, i: (bi, i, 0))
    mod_map = (lambda bi, i: (bi, 0, 0)) if per_batch_mod else (lambda bi, i: (0, 0, 0))
    fmap = pl.BlockSpec((1, FOURIER_WIDTH, FOURIER_WIDTH), lambda bi, i: (layer, 0, 0))
    in_specs = [row(d),
                pl.BlockSpec((1, 3, d), mod_map),
                pl.BlockSpec((d, n_in), lambda bi, i: (0, 0)),
                fmap, fmap]
    args = [x, mod3, w_in, fa, fb]
    if rope_tabs is not None:
        in_specs += [pl.BlockSpec((tm, HEAD_W), lambda bi, i: (i, 0))] * 3
        args += list(rope_tabs)
    sds = jax.ShapeDtypeStruct
    out_shape = [sds((b, l, ATT_WIDTH), BF16)] * 3 + [sds((b, l, ATT_WIDTH), F32),
                 sds((b, 2, l, FOURIER_WIDTH), BF16), sds((b, l, FOURIER_WIDTH), F32),
                 sds((b, l, GMLP_WIDTH), F32), sds((b, l, GMLP_WIDTH), F32),
                 sds((b, l, GMLP_WIDTH), F32)]
    out_specs = [row(ATT_WIDTH)] * 4 + [
        pl.BlockSpec((1, 2, tm, FOURIER_WIDTH), lambda bi, i: (bi, 0, i, 0)),
        row(FOURIER_WIDTH), row(GMLP_WIDTH), row(GMLP_WIDTH), row(GMLP_WIDTH)]
    if emit_kv:
        kv = sds((b, N_ATT_HEADS, l, HEAD_W), F32)
        out_shape += [kv, kv]
        out_specs += [pl.BlockSpec((1, N_ATT_HEADS, tm, HEAD_W), lambda bi, i: (bi, 0, i, 0))] * 2
    return pl.pallas_call(
        functools.partial(_inproj_kernel, rope=rope_tabs is not None, emit_kv=emit_kv),
        grid=(b, l // tm),
        in_specs=in_specs, out_specs=out_specs, out_shape=out_shape,
        name=f"inproj_{l}",
        compiler_params=_cparams("parallel", "parallel"),
    )(*args)


def _attn_kernel(*refs, lam_init, has_cache, ck, n_new, past, tq):
    if has_cache:
        (q_ref, k_ref, v_ref, pk_ref, pv_ref, g_ref, lq1_ref, lk1_ref, lq2_ref, lk2_ref,
         sw_ref, o_ref, s_ref) = refs
    else:
        (q_ref, k_ref, v_ref, g_ref, lq1_ref, lk1_ref, lq2_ref, lk2_ref,
         sw_ref, o_ref, s_ref) = refs

    chunks = []
    if has_cache:
        chunks.append((True, 0, 0, past))
    for j in range(n_new):
        chunks.append((False, j * ck, past + j * ck, ck))

    def keys(c):
        return pk_ref[0, 0, 0].astype(BF16) if c[0] else k_ref[0, c[1]:c[1] + c[3], :]

    def vals(c):
        return pv_ref[0, 0, 0].astype(BF16) if c[0] else v_ref[0, c[1]:c[1] + c[3], :]

    q = q_ref[0]
    lane = lax.broadcasted_iota(jnp.int32, q.shape, 1)
    zero = jnp.zeros_like(q)
    qq = jnp.concatenate([jnp.where(lane < HEAD_DIM, q, zero),
                          jnp.where(lane >= HEAD_DIM, q, zero)], axis=0)

    mv = jnp.full((2 * tq, 128), -jnp.inf, F32)
    for c in chunks:
        s = lax.dot_general(qq, keys(c), (((1,), (1,)), ((), ())), preferred_element_type=F32)
        s_ref[:, c[2]:c[2] + c[3]] = s
        for t in range(c[3] // 128):
            mv = jnp.maximum(mv, s[:, t * 128:(t + 1) * 128])
    m = jnp.max(mv, axis=-1, keepdims=True)

    lv = jnp.zeros((2 * tq, 128), F32)
    for c in chunks:
        p = jnp.exp(s_ref[:, c[2]:c[2] + c[3]] - m)
        s_ref[:, c[2]:c[2] + c[3]] = p
        for t in range(c[3] // 128):
            lv = lv + p[:, t * 128:(t + 1) * 128]
    lsum = jnp.sum(lv, axis=-1, keepdims=True)
    l1, l2 = lsum[:tq], lsum[tq:]

    lam = (jnp.exp(jnp.sum(lq1_ref[0] * lk1_ref[0], axis=-1, keepdims=True))
           - jnp.exp(jnp.sum(lq2_ref[0] * lk2_ref[0], axis=-1, keepdims=True)) + lam_init)
    rho = lam * l1 / l2
    acc = jnp.zeros((tq, HEAD_W), F32)
    for c in chunks:
        a = (s_ref[0:tq, c[2]:c[2] + c[3]] - rho * s_ref[tq:2 * tq, c[2]:c[2] + c[3]]).astype(BF16)
        acc = acc + jnp.dot(a, vals(c), preferred_element_type=F32)
    o = acc / l1
    o = (o * lax.rsqrt(jnp.mean(o * o, axis=-1, keepdims=True) + RMS_EPS)
         * sw_ref[0] * (1.0 - lam_init))
    o_ref[0] = (o * _silu(g_ref[0])).astype(BF16)


def _attention(q, k, v, g_att, cache, layer, lam_params, subln_w, lam_init, tq, ck):
    b, l, _ = q.shape
    tq = _row_tile(l, tq)
    ck = _row_tile(l, ck)
    has_cache = cache is not None
    past = cache[0].shape[3] if has_cache else 0
    lk = past + l
    qspec = pl.BlockSpec((1, tq, HEAD_W), lambda bi, h, i: (bi, i, h))
    kvspec = pl.BlockSpec((1, l, HEAD_W), lambda bi, h, i: (bi, 0, h))
    vec = lambda w: pl.BlockSpec((1, 1, w), lambda bi, h, i: (layer, 0, 0))
    in_specs = [qspec, kvspec, kvspec]
    args = [q, k, v]
    if has_cache:
        cspec = pl.BlockSpec((1, 1, 1, past, HEAD_W), lambda bi, h, i: (bi, layer, h, 0, 0))
        in_specs += [cspec, cspec]
        args += list(cache)
    in_specs += [qspec] + [vec(HEAD_DIM)] * 4 + [vec(HEAD_W)]
    args += [g_att] + list(lam_params) + [subln_w]
    return pl.pallas_call(
        functools.partial(_attn_kernel, lam_init=lam_init, has_cache=has_cache, ck=ck,
                          n_new=l // ck, past=past, tq=tq),
        grid=(b, N_ATT_HEADS, l // tq),
        in_specs=in_specs,
        out_specs=qspec,
        out_shape=jax.ShapeDtypeStruct((b, l, ATT_WIDTH), BF16),
        scratch_shapes=[pltpu.VMEM((2 * tq, lk), F32)],
        name=f"attn_{l}",
        compiler_params=_cparams("parallel", "parallel", "arbitrary"),
    )(*args)


def _fourier_kernel(g_ref, u_ref, gate_ref, o_ref, *, nb):
    g = g_ref[...]
    for j in range(nb):
        y = jnp.dot(g, u_ref[j], preferred_element_type=F32)
        o_ref[j] = (y * _silu(gate_ref[j])).astype(BF16)


def _fourier(gmat, uf, g_f, tr, nb):
    b, l, w = g_f.shape
    tr = _row_tile(l, tr)
    u2 = uf.reshape(b, 2 * l, w)
    return pl.pallas_call(
        functools.partial(_fourier_kernel, nb=nb),
        grid=(b // nb, l // tr),
        in_specs=[pl.BlockSpec((tr, 2 * l), lambda bi, i: (i, 0)),
                  pl.BlockSpec((nb, 2 * l, w), lambda bi, i: (bi, 0, 0)),
                  pl.BlockSpec((nb, tr, w), lambda bi, i: (bi, i, 0))],
        out_specs=pl.BlockSpec((nb, tr, w), lambda bi, i: (bi, i, 0)),
        out_shape=jax.ShapeDtypeStruct((b, l, w), BF16),
        name=f"fourier_{l}",
        compiler_params=_cparams("parallel", "arbitrary"),
    )(gmat, u2, g_f)


def _gmlp_kernel(vm_ref, u_ref, gm_ref, w_ref, bias_ref, avg_ref, o_ref, *, n_chunks):
    avg = avg_ref[...]

    def group_mean(x):
        hi = x.astype(BF16)
        lo = (x - hi.astype(F32)).astype(BF16)
        return (jnp.dot(hi, avg, preferred_element_type=F32)
                + jnp.dot(lo, avg, preferred_element_type=F32))

    lane = lax.broadcasted_iota(jnp.int32, (CHUNK, GMLP_WIDTH), 1)
    head_mask = [(lane >= hh * GMLP_HEAD_DIM) & (lane < (hh + 1) * GMLP_HEAD_DIM)
                 for hh in range(N_GMLP_HEADS)]
    for c in range(n_chunks):
        rows = slice(c * CHUNK, (c + 1) * CHUNK)
        x = vm_ref[0, rows, :]
        xc = x - group_mean(x)
        vn = (xc * lax.rsqrt(group_mean(xc * xc) + LN_EPS)).astype(BF16)
        stacked = jnp.concatenate(
            [jnp.where(head_mask[hh], vn, jnp.zeros_like(vn)) for hh in range(N_GMLP_HEADS)], axis=0)
        s = jnp.dot(w_ref[0], stacked, preferred_element_type=F32) + bias_ref[0]
        o_ref[0, rows, :] = (u_ref[0, rows, :] * s * _silu(gm_ref[0, rows, :])).astype(BF16)


def _gmlp(vm, u, g_m, w_cat, bias_full, avg, layer, tm):
    b, l, w = vm.shape
    tm = _row_tile(l, tm)
    row = pl.BlockSpec((1, tm, w), lambda bi, i: (bi, i, 0))
    return pl.pallas_call(
        functools.partial(_gmlp_kernel, n_chunks=tm // CHUNK),
        grid=(b, l // tm),
        in_specs=[row, row, row,
                  pl.BlockSpec((1, CHUNK, N_GMLP_HEADS * CHUNK), lambda bi, i: (layer, 0, 0)),
                  pl.BlockSpec((1, CHUNK, w), lambda bi, i: (layer, 0, 0)),
                  pl.BlockSpec((w, w), lambda bi, i: (0, 0))],
        out_specs=row,
        out_shape=jax.ShapeDtypeStruct((b, l, w), BF16),
        name=f"gmlp_{l}",
        compiler_params=_cparams("parallel", "parallel"),
    )(vm, u, g_m, w_cat, bias_full, avg)


def _outproj_kernel(att_ref, fo_ref, gm_ref, x_ref, mod_ref, w_ref, lg_ref, lb_ref, o_ref, *, alpha):
    a0, a1 = ATT_WIDTH, ATT_WIDTH + FOURIER_WIDTH
    y = (jnp.dot(att_ref[0], w_ref[0:a0, :], preferred_element_type=F32)
         + jnp.dot(fo_ref[0], w_ref[a0:a1, :], preferred_element_type=F32)
         + jnp.dot(gm_ref[0], w_ref[a1:, :], preferred_element_type=F32))
    r = alpha * x_ref[0] + mod_ref[0, 2:3, :] * y
    mu = jnp.mean(r, axis=-1, keepdims=True)
    rc = r - mu
    var = jnp.mean(rc * rc, axis=-1, keepdims=True)
    o_ref[0] = rc * lax.rsqrt(var + LN_EPS) * lg_ref[0] + lb_ref[0]


def _out_projection(att, fo, gm, x, mod3, per_batch_mod, w_out, ln_g, ln_b, layer, alpha, tm):
    b, l, d = x.shape
    tm = _row_tile(l, tm)
    row = lambda w: pl.BlockSpec((1, tm, w), lambda bi, i: (bi, i, 0))
    mod_map = (lambda bi, i: (bi, 0, 0)) if per_batch_mod else (lambda bi, i: (0, 0, 0))
    vec = pl.BlockSpec((1, 1, d), lambda bi, i: (layer, 0, 0))
    return pl.pallas_call(
        functools.partial(_outproj_kernel, alpha=alpha),
        grid=(b, l // tm),
        in_specs=[row(ATT_WIDTH), row(FOURIER_WIDTH), row(GMLP_WIDTH), row(d),
                  pl.BlockSpec((1, 3, d), mod_map),
                  pl.BlockSpec(w_out.shape, lambda bi, i: (0, 0)),
                  vec, vec],
        out_specs=row(d),
        out_shape=jax.ShapeDtypeStruct((b, l, d), F32),
        name=f"outproj_{l}",
        compiler_params=_cparams("parallel", "parallel"),
    )(att, fo, gm, x, mod3, w_out, ln_g, ln_b)


def _rope_tables(l):
    n_freq = HEAD_DIM // 4
    rows = l // GRID_W
    row = jnp.repeat(jnp.arange(rows, dtype=F32), GRID_W)
    col = jnp.tile(jnp.arange(GRID_W, dtype=F32), rows)
    inv = ROPE_THETA ** (-jnp.arange(n_freq, dtype=F32) / n_freq)
    ang_r, ang_c = row[:, None] * inv, col[:, None] * inv
    zeros = jnp.zeros_like(ang_r)

    def per_head(r_lo, r_hi, c_lo, c_hi):
        one = jnp.concatenate([r_lo, r_hi, c_lo, c_hi], axis=-1)
        return jnp.concatenate([one, one], axis=-1)

    cr, sr, cc, sc = jnp.cos(ang_r), jnp.sin(ang_r), jnp.cos(ang_c), jnp.sin(ang_c)
    cos_t = per_head(cr, cr, cc, cc)
    sa_t = per_head(-sr, zeros, -sc, zeros)
    sb_t = per_head(zeros, sr, zeros, sc)
    return cos_t, sa_t, sb_t


def _run_layer(x, mod3, per_batch_mod, layer, lam_init, alpha, p, gmat, rope_tabs, cache, emit_kv,
               tiles):
    outs = _in_projection(x, mod3, per_batch_mod, p["w_in"][layer], p["fa"], p["fb"], layer,
                          rope_tabs, emit_kv, tiles["tm_in"])
    q, k, v, g_att, uf, g_f, u, vm, g_m = outs[:9]
    att = _attention(q, k, v, g_att, cache, layer, p["lam"], p["subln_w"], lam_init,
                     tiles["tq"], tiles["ck"])
    fo = _fourier(gmat, uf, g_f, tiles["tr"], tiles["nb"])
    gm = _gmlp(vm, u, g_m, p["gmlp_w"], p["gmlp_bias"], p["avg"], layer, tiles["tm_gmlp"])
    x_new = _out_projection(att, fo, gm, x, mod3, per_batch_mod, p["w_out"][layer],
                            p["ln_g"], p["ln_b"], layer, alpha, tiles["tm_out"])
    return x_new, outs[9:]


def kernel(x_prompt, x_sample, c, cache_k, cache_v, c_ctx, w_ada, b_ada, w_in, w_out,
           lam_q1, lam_k1, lam_q2, lam_k2, subln_w, fourier_w, gmlp_ws, gmlp_bs, ln_g, ln_b):
    depth, d, _ = w_ada.shape
    nb_lat = c.shape[0]
    l_ctx, l_lat = x_prompt.shape[1], x_sample.shape[1]
    assert 1 + nb_lat <= MOD_ROWS
    alpha = (2.0 * depth) ** 0.25

    cvec = jnp.concatenate([c_ctx[None, :], c, jnp.zeros((MOD_ROWS - 1 - nb_lat, d), F32)], axis=0)
    mod = _modulation(cvec, w_ada, b_ada)

    fa, fb = _fourier_channel_maps(fourier_w, 1.0 / math.sqrt(FOURIER_GROUP_DIM))
    params = {
        "w_in": w_in.astype(BF16), "w_out": w_out.astype(BF16), "fa": fa, "fb": fb,
        "lam": tuple(t[:, None, :] for t in (lam_q1, lam_k1, lam_q2, lam_k2)),
        "subln_w": subln_w[:, None, :], "ln_g": ln_g[:, None, :], "ln_b": ln_b[:, None, :],
        "gmlp_w": gmlp_ws.astype(BF16).transpose(0, 2, 1, 3).reshape(depth, CHUNK, N_GMLP_HEADS * CHUNK),
        "gmlp_bias": jnp.repeat(gmlp_bs.astype(F32).transpose(0, 2, 1), GMLP_HEAD_DIM, axis=2),
        "avg": jnp.asarray(np.kron(np.eye(N_GMLP_HEADS), np.full((GMLP_HEAD_DIM,) * 2, 1.0 / GMLP_HEAD_DIM)), BF16),
    }
    ctx_tiles = dict(tm_in=256, tq=256, ck=256, tr=256, nb=2, tm_gmlp=256, tm_out=256)
    lat_tiles = dict(tm_in=512, tq=256, ck=512, tr=512, nb=2, tm_gmlp=512, tm_out=512)

    g_ctx = _dft_matrix(l_ctx)
    xp = x_prompt
    ks_new, vs_new = [], []
    for layer in range(depth):
        lam_init = 0.8 - 0.6 * math.exp(-0.3 * layer)
        mod3 = mod[layer, 0:1].reshape(1, 3, d)
        xp, (k_l, v_l) = _run_layer(xp, mod3, False, layer, lam_init, alpha, params, g_ctx,
                                    None, None, True, ctx_tiles)
        ks_new.append(k_l)
        vs_new.append(v_l)
    new_k = jnp.stack(ks_new, axis=1)
    new_v = jnp.stack(vs_new, axis=1)

    g_lat = _dft_matrix(l_lat)
    rope_tabs = _rope_tables(l_lat)
    xs = x_sample
    for layer in range(depth):
        lam_init = 0.8 - 0.6 * math.exp(-0.3 * layer)
        mod3 = mod[layer, 1:1 + nb_lat].reshape(nb_lat, 3, d)
        xs, _ = _run_layer(xs, mod3, True, layer, lam_init, alpha, params, g_lat,
                           rope_tabs, (cache_k, cache_v), False, lat_tiles)
    return (xp, xs, new_k, new_v)
```

```python
import functools
import math

import numpy as np
import jax
import jax.numpy as jnp
from jax import lax
from jax.experimental import pallas as pl
from jax.experimental.pallas import tpu as pltpu

F32 = jnp.float32
BF16 = jnp.bfloat16

GRID_W = 64
HEAD_DIM = 64
N_ATT_HEADS = 4
HEAD_W = 2 * HEAD_DIM
ATT_WIDTH = N_ATT_HEADS * HEAD_W
N_FOURIER_GROUPS = 4
FOURIER_GROUP_DIM = 64
FOURIER_WIDTH = N_FOURIER_GROUPS * FOURIER_GROUP_DIM
N_GMLP_HEADS = 4
GMLP_HEAD_DIM = 64
GMLP_WIDTH = N_GMLP_HEADS * GMLP_HEAD_DIM
CHUNK = 128
ROPE_THETA = 10000.0
LN_EPS = 1e-5
RMS_EPS = 1e-5
QK_SCALE = HEAD_DIM ** -0.5
LOG2_E = math.log2(math.e)

_SEG_NAMES = ("q", "k", "v", "g_att", "f", "g_f", "u", "vm", "g_m")
_SEG_WIDTHS = (ATT_WIDTH,) * 4 + (FOURIER_WIDTH,) * 2 + (GMLP_WIDTH,) * 3
_SEG_START = dict(zip(_SEG_NAMES, np.concatenate([[0], np.cumsum(_SEG_WIDTHS)[:-1]]).tolist()))
_SEG_WIDTH = dict(zip(_SEG_NAMES, _SEG_WIDTHS))

VMEM_LIMIT_BYTES = 52 * 1024 * 1024
MOD_ROWS = 16
SOFTMAX_ROWS = 8


def _cparams(*sem):
    return pltpu.CompilerParams(dimension_semantics=sem, vmem_limit_bytes=VMEM_LIMIT_BYTES)


def _silu(x):
    return x * jax.nn.sigmoid(x)


def _row_tile(l, want):
    t = min(l, want)
    assert l % t == 0
    return t


def _mod_kernel(c_ref, w_ref, b_ref, o_ref):
    s = _silu(c_ref[...])
    o_ref[0] = jnp.dot(s, w_ref[0], precision=lax.Precision.HIGHEST,
                       preferred_element_type=F32) + b_ref[0]


def _modulation(cvec, w_ada, b_ada):
    depth, d, d3 = w_ada.shape
    tn = 1024
    return pl.pallas_call(
        _mod_kernel,
        grid=(depth, d3 // tn),
        in_specs=[pl.BlockSpec((MOD_ROWS, d), lambda l, j: (0, 0)),
                  pl.BlockSpec((1, d, tn), lambda l, j: (l, 0, j)),
                  pl.BlockSpec((1, 1, tn), lambda l, j: (l, 0, j))],
        out_specs=pl.BlockSpec((1, MOD_ROWS, tn), lambda l, j: (l, 0, j)),
        out_shape=jax.ShapeDtypeStruct((depth, MOD_ROWS, d3), F32),
        name="adaln_mod",
        compiler_params=_cparams("arbitrary", "arbitrary"),
    )(cvec, w_ada, b_ada.reshape(depth, 1, d3))


def _chanmap_kernel(cc_ref, ss_ref, w_ref, a_ref, b_ref, *, scale):
    w = w_ref[0]
    a_ref[0] = (jnp.dot(cc_ref[...], w, precision=lax.Precision.HIGHEST,
                        preferred_element_type=F32) * scale).astype(BF16)
    b_ref[0] = (jnp.dot(ss_ref[...], w, precision=lax.Precision.HIGHEST,
                        preferred_element_type=F32) * scale).astype(BF16)


def _fourier_channel_maps(fourier_w, scale):
    depth = fourier_w.shape[0]
    n = np.arange(FOURIER_GROUP_DIM)
    ang = 2.0 * np.pi * ((n[:, None] * n[None, :]) % FOURIER_GROUP_DIM) / FOURIER_GROUP_DIM
    eye = np.eye(N_FOURIER_GROUPS)
    cc = jnp.asarray(np.kron(eye, np.cos(ang)), F32)
    ss = jnp.asarray(np.kron(eye, np.sin(ang)), F32)
    wbd = jnp.einsum("gh,lgce->lgche", jnp.eye(N_FOURIER_GROUPS, dtype=F32),
                     fourier_w.astype(F32)).reshape(depth, FOURIER_WIDTH, FOURIER_WIDTH)
    w_spec = pl.BlockSpec((1, FOURIER_WIDTH, FOURIER_WIDTH), lambda l: (l, 0, 0))
    c_spec = pl.BlockSpec((FOURIER_WIDTH, FOURIER_WIDTH), lambda l: (0, 0))
    shp = jax.ShapeDtypeStruct((depth, FOURIER_WIDTH, FOURIER_WIDTH), BF16)
    return pl.pallas_call(
        functools.partial(_chanmap_kernel, scale=scale),
        grid=(depth,),
        in_specs=[c_spec, c_spec, w_spec],
        out_specs=[w_spec, w_spec],
        out_shape=[shp, shp],
        name="fourier_chanmap",
        compiler_params=_cparams("arbitrary"),
    )(cc, ss, wbd)


def _dft_kernel(ca_ref, sa_ref, cb_ref, sb_ref, g_ref, *, l):
    ca, sa = ca_ref[0], sa_ref[0]
    cb, sb = cb_ref[...], sb_ref[...]
    norm = 1.0 / math.sqrt(l)
    g_ref[:, 0:l] = ((ca * cb - sa * sb) * norm).astype(BF16)
    g_ref[:, l:2 * l] = ((sa * cb + ca * sb) * -norm).astype(BF16)


def _dft_matrix(l):
    na = l // GRID_W
    k = np.arange(l, dtype=np.int64)
    ang_a = ((np.arange(na)[:, None] * k[None, :]) % na) * (2.0 * np.pi / na)
    ang_b = ((np.arange(GRID_W)[:, None] * k[None, :]) % l) * (2.0 * np.pi / l)
    ca = jnp.asarray(np.cos(ang_a).reshape(na, 1, l), F32)
    sa = jnp.asarray(np.sin(ang_a).reshape(na, 1, l), F32)
    cb, sb = jnp.asarray(np.cos(ang_b), F32), jnp.asarray(np.sin(ang_b), F32)
    a_spec = pl.BlockSpec((1, 1, l), lambda i: (i, 0, 0))
    b_spec = pl.BlockSpec((GRID_W, l), lambda i: (0, 0))
    return pl.pallas_call(
        functools.partial(_dft_kernel, l=l),
        grid=(na,),
        in_specs=[a_spec, a_spec, b_spec, b_spec],
        out_specs=pl.BlockSpec((GRID_W, 2 * l), lambda i: (i, 0)),
        out_shape=jax.ShapeDtypeStruct((l, 2 * l), BF16),
        name=f"dft_matrix_{l}",
        compiler_params=_cparams("arbitrary"),
    )(ca, sa, cb, sb)


def _inproj_kernel(*refs, rope, emit_kv):
    x_ref, mod_ref, w_ref, fa_ref, fb_ref = refs[:5]
    refs = refs[5:]
    if rope:
        cos_ref, sa_ref, sb_ref = refs[:3]
        refs = refs[3:]
    q_ref, k_ref, v_ref, ga_ref, uf_ref, gf_ref, u_ref, vm_ref, gm_ref = refs[:9]
    refs = refs[9:]

    x = x_ref[0]
    mu = jnp.mean(x, axis=-1, keepdims=True)
    xc = x - mu
    var = jnp.mean(xc * xc, axis=-1, keepdims=True)
    shift, scale = mod_ref[0, 0:1, :], mod_ref[0, 1:2, :]
    h = (xc * lax.rsqrt(var + LN_EPS) * (1.0 + scale) + shift).astype(BF16)

    def proj(name):
        c0 = _SEG_START[name]
        return jnp.dot(h, w_ref[:, c0:c0 + _SEG_WIDTH[name]], preferred_element_type=F32)

    def rotate(z):
        return (z * cos_ref[...] + pltpu.roll(z, HEAD_W - 16, 1) * sa_ref[...]
                + pltpu.roll(z, 16, 1) * sb_ref[...])

    zq, zk, zv = proj("q"), proj("k"), proj("v")
    for hh in range(N_ATT_HEADS):
        sl = slice(hh * HEAD_W, (hh + 1) * HEAD_W)
        qh, kh = zq[:, sl], zk[:, sl]
        if emit_kv:
            kf_ref, vf_ref = refs
            kf_ref[0, hh] = kh
            vf_ref[0, hh] = zv[:, sl]
        if rope:
            qh, kh = rotate(qh), rotate(kh)
        q_ref[0, :, sl] = (qh * (QK_SCALE * LOG2_E)).astype(BF16)
        k_ref[0, :, sl] = kh.astype(BF16)
    v_ref[0] = zv.astype(BF16)
    ga_ref[0] = proj("g_att")
    f = proj("f").astype(BF16)
    uf_ref[0, 0] = jnp.dot(f, fa_ref[0], preferred_element_type=F32).astype(BF16)
    uf_ref[0, 1] = jnp.dot(f, fb_ref[0], preferred_element_type=F32).astype(BF16)
    gf_ref[0] = proj("g_f")
    u_ref[0] = proj("u")
    vm_ref[0] = proj("vm")
    gm_ref[0] = proj("g_m")


def _in_projection(x, mod3, per_batch_mod, w_in, fa, fb, layer, rope_tabs, emit_kv, tm):
    b, l, d = x.shape
    n_in = w_in.shape[1]
    tm = _row_tile(l, tm)
    row = lambda w: pl.BlockSpec((1, tm, w), lambda bi, i: (bi, i, 0))
    mod_map = (lambda bi, i: (bi, 0, 0)) if per_batch_mod else (lambda bi, i: (0, 0, 0))
    fmap = pl.BlockSpec((1, FOURIER_WIDTH, FOURIER_WIDTH), lambda bi, i: (layer, 0, 0))
    in_specs = [row(d),
                pl.BlockSpec((1, 3, d), mod_map),
                pl.BlockSpec((d, n_in), lambda bi, i: (0, 0)),
                fmap, fmap]
    args = [x, mod3, w_in, fa, fb]
    if rope_tabs is not None:
        in_specs += [pl.BlockSpec((tm, HEAD_W), lambda bi, i: (i, 0))] * 3
        args += list(rope_tabs)
    sds = jax.ShapeDtypeStruct
    out_shape = [sds((b, l, ATT_WIDTH), BF16)] * 3 + [sds((b, l, ATT_WIDTH), F32),
                 sds((b, 2, l, FOURIER_WIDTH), BF16), sds((b, l, FOURIER_WIDTH), F32),
                 sds((b, l, GMLP_WIDTH), F32), sds((b, l, GMLP_WIDTH), F32),
                 sds((b, l, GMLP_WIDTH), F32)]
    out_specs = [row(ATT_WIDTH)] * 4 + [
        pl.BlockSpec((1, 2, tm, FOURIER_WIDTH), lambda bi, i: (bi, 0, i, 0)),
        row(FOURIER_WIDTH), row(GMLP_WIDTH), row(GMLP_WIDTH), row(GMLP_WIDTH)]
    if emit_kv:
        kv = sds((b, N_ATT_HEADS, l, HEAD_W), F32)
        out_shape += [kv, kv]
        out_specs += [pl.BlockSpec((1, N_ATT_HEADS, tm, HEAD_W), lambda bi, i: (bi, 0, i, 0))] * 2
    return pl.pallas_call(
        functools.partial(_inproj_kernel, rope=rope_tabs is not None, emit_kv=emit_kv),
        grid=(b, l // tm),
        in_specs=in_specs, out_specs=out_specs, out_shape=out_shape,
        name=f"inproj_{l}",
        compiler_params=_cparams("parallel", "parallel"),
    )(*args)


def _attn_kernel(*refs, lam_init, has_cache, ck, n_new, past, tq, nq):
    if has_cache:
        (q_ref, k_ref, v_ref, pk_ref, pv_ref, g_ref, lq1_ref, lk1_ref, lq2_ref, lk2_ref,
         sw_ref, o_ref, sa_ref, sb_ref) = refs
    else:
        (q_ref, k_ref, v_ref, g_ref, lq1_ref, lk1_ref, lq2_ref, lk2_ref,
         sw_ref, o_ref, sa_ref, sb_ref) = refs
    s_slots = (sa_ref, sb_ref)

    chunks = []
    if has_cache:
        chunks.append((True, 0, 0, past))
    for j in range(n_new):
        chunks.append((False, j * ck, past + j * ck, ck))

    def keys(c):
        return pk_ref[0, 0, 0].astype(BF16) if c[0] else k_ref[0, c[1]:c[1] + c[3], :]

    def vals(c):
        return pv_ref[0, 0, 0].astype(BF16) if c[0] else v_ref[0, c[1]:c[1] + c[3], :]

    lam = (jnp.exp(jnp.sum(lq1_ref[0] * lk1_ref[0], axis=-1, keepdims=True))
           - jnp.exp(jnp.sum(lq2_ref[0] * lk2_ref[0], axis=-1, keepdims=True)) + lam_init)

    def rows(tile):
        return pl.ds(pl.multiple_of(tile * tq, tq), tq)

    def stacked_q(tile):
        q = q_ref[0, rows(tile), :]
        lane = lax.broadcasted_iota(jnp.int32, q.shape, 1)
        zero = jnp.zeros_like(q)
        return jnp.concatenate([jnp.where(lane < HEAD_DIM, q, zero),
                                jnp.where(lane >= HEAD_DIM, q, zero)], axis=0)

    def stage(cur, cur_tile, nxt_tile):
        s_cur, s_nxt = s_slots[cur], s_slots[1 - cur]
        n_groups = 2 * tq // SOFTMAX_ROWS
        per_chunk = -(-n_groups // len(chunks))
        if nxt_tile is not None:
            qq = stacked_q(nxt_tile)
        lparts = []
        for ci, c in enumerate(chunks):
            if nxt_tile is not None:
                s_nxt[:, c[2]:c[2] + c[3]] = lax.dot_general(
                    qq, keys(c), (((1,), (1,)), ((), ())), preferred_element_type=F32)
            if cur_tile is not None:
                for gi in range(ci * per_chunk, min((ci + 1) * per_chunk, n_groups)):
                    grp = slice(gi * SOFTMAX_ROWS, (gi + 1) * SOFTMAX_ROWS)
                    blk = s_cur[grp, :]
                    p = jnp.exp2(blk - jnp.max(blk, axis=-1, keepdims=True))
                    s_cur[grp, :] = p
                    lparts.append(jnp.sum(p, axis=-1, keepdims=True))
        if cur_tile is None:
            return
        lsum = jnp.concatenate(lparts, axis=0)
        l1, l2 = lsum[:tq], lsum[tq:]
        rho = lam * l1 / l2
        acc = jnp.zeros((tq, HEAD_W), F32)
        for c in chunks:
            a = (s_cur[0:tq, c[2]:c[2] + c[3]]
                 - rho * s_cur[tq:2 * tq, c[2]:c[2] + c[3]]).astype(BF16)
            acc = acc + jnp.dot(a, vals(c), preferred_element_type=F32)
        o = acc / l1
        o = (o * lax.rsqrt(jnp.mean(o * o, axis=-1, keepdims=True) + RMS_EPS)
             * sw_ref[0] * (1.0 - lam_init))
        o_ref[0, rows(cur_tile), :] = (o * _silu(g_ref[0, rows(cur_tile), :])).astype(BF16)

    stage(1, None, 0)

    def pair(t, carry):
        stage(0, 2 * t, 2 * t + 1)
        stage(1, 2 * t + 1, 2 * t + 2)
        return carry

    lax.fori_loop(0, nq // 2 - 1, pair, 0)
    stage(0, nq - 2, nq - 1)
    stage(1, nq - 1, None)


def _attention(q, k, v, g_att, cache, layer, lam_params, subln_w, lam_init, tq, ck):
    b, l, _ = q.shape
    tq = _row_tile(l, tq)
    ck = _row_tile(l, ck)
    nq = l // tq
    assert nq >= 2 and nq % 2 == 0
    has_cache = cache is not None
    past = cache[0].shape[3] if has_cache else 0
    lk = past + l
    seq = pl.BlockSpec((1, l, HEAD_W), lambda bi, h: (bi, 0, h))
    vec = lambda w: pl.BlockSpec((1, 1, w), lambda bi, h: (layer, 0, 0))
    in_specs = [seq, seq, seq]
    args = [q, k, v]
    if has_cache:
        cspec = pl.BlockSpec((1, 1, 1, past, HEAD_W), lambda bi, h: (bi, layer, h, 0, 0))
        in_specs += [cspec, cspec]
        args += list(cache)
    in_specs += [seq] + [vec(HEAD_DIM)] * 4 + [vec(HEAD_W)]
    args += [g_att] + list(lam_params) + [subln_w]
    return pl.pallas_call(
        functools.partial(_attn_kernel, lam_init=lam_init, has_cache=has_cache, ck=ck,
                          n_new=l // ck, past=past, tq=tq, nq=nq),
        grid=(b, N_ATT_HEADS),
        in_specs=in_specs,
        out_specs=seq,
        out_shape=jax.ShapeDtypeStruct((b, l, ATT_WIDTH), BF16),
        scratch_shapes=[pltpu.VMEM((2 * tq, lk), F32), pltpu.VMEM((2 * tq, lk), F32)],
        name=f"attn_{l}",
        compiler_params=_cparams("parallel", "parallel"),
    )(*args)


def _fourier_kernel(g_ref, u_ref, gate_ref, o_ref, *, nb):
    g = g_ref[...]
    for j in range(nb):
        y = jnp.dot(g, u_ref[j], preferred_element_type=F32)
        o_ref[j] = (y * _silu(gate_ref[j])).astype(BF16)


def _fourier(gmat, uf, g_f, tr, nb):
    b, l, w = g_f.shape
    tr = _row_tile(l, tr)
    u2 = uf.reshape(b, 2 * l, w)
    return pl.pallas_call(
        functools.partial(_fourier_kernel, nb=nb),
        grid=(b // nb, l // tr),
        in_specs=[pl.BlockSpec((tr, 2 * l), lambda bi, i: (i, 0)),
                  pl.BlockSpec((nb, 2 * l, w), lambda bi, i: (bi, 0, 0)),
                  pl.BlockSpec((nb, tr, w), lambda bi, i: (bi, i, 0))],
        out_specs=pl.BlockSpec((nb, tr, w), lambda bi, i: (bi, i, 0)),
        out_shape=jax.ShapeDtypeStruct((b, l, w), BF16),
        name=f"fourier_{l}",
        compiler_params=_cparams("parallel", "arbitrary"),
    )(gmat, u2, g_f)


def _gmlp_kernel(vm_ref, u_ref, gm_ref, w_ref, bias_ref, avg_ref, o_ref, *, n_chunks):
    avg = avg_ref[...]

    def group_mean(x):
        hi = x.astype(BF16)
        lo = (x - hi.astype(F32)).astype(BF16)
        return (jnp.dot(hi, avg, preferred_element_type=F32)
                + jnp.dot(lo, avg, preferred_element_type=F32))

    lane = lax.broadcasted_iota(jnp.int32, (CHUNK, GMLP_WIDTH), 1)
    head_mask = [(lane >= hh * GMLP_HEAD_DIM) & (lane < (hh + 1) * GMLP_HEAD_DIM)
                 for hh in range(N_GMLP_HEADS)]
    for c in range(n_chunks):
        rows = slice(c * CHUNK, (c + 1) * CHUNK)
        x = vm_ref[0, rows, :]
        xc = x - group_mean(x)
        vn = (xc * lax.rsqrt(group_mean(xc * xc) + LN_EPS)).astype(BF16)
        stacked = jnp.concatenate(
            [jnp.where(head_mask[hh], vn, jnp.zeros_like(vn)) for hh in range(N_GMLP_HEADS)], axis=0)
        s = jnp.dot(w_ref[0], stacked, preferred_element_type=F32) + bias_ref[0]
        o_ref[0, rows, :] = (u_ref[0, rows, :] * s * _silu(gm_ref[0, rows, :])).astype(BF16)


def _gmlp(vm, u, g_m, w_cat, bias_full, avg, layer, tm):
    b, l, w = vm.shape
    tm = _row_tile(l, tm)
    row = pl.BlockSpec((1, tm, w), lambda bi, i: (bi, i, 0))
    return pl.pallas_call(
        functools.partial(_gmlp_kernel, n_chunks=tm // CHUNK),
        grid=(b, l // tm),
        in_specs=[row, row, row,
                  pl.BlockSpec((1, CHUNK, N_GMLP_HEADS * CHUNK), lambda bi, i: (layer, 0, 0)),
                  pl.BlockSpec((1, CHUNK, w), lambda bi, i: (layer, 0, 0)),
                  pl.BlockSpec((w, w), lambda bi, i: (0, 0))],
        out_specs=row,
        out_shape=jax.ShapeDtypeStruct((b, l, w), BF16),
        name=f"gmlp_{l}",
        compiler_params=_cparams("parallel", "parallel"),
    )(vm, u, g_m, w_cat, bias_full, avg)


def _outproj_kernel(att_ref, fo_ref, gm_ref, x_ref, mod_ref, w_ref, lg_ref, lb_ref, o_ref, *, alpha):
    a0, a1 = ATT_WIDTH, ATT_WIDTH + FOURIER_WIDTH
    y = (jnp.dot(att_ref[0], w_ref[0:a0, :], preferred_element_type=F32)
         + jnp.dot(fo_ref[0], w_ref[a0:a1, :], preferred_element_type=F32)
         + jnp.dot(gm_ref[0], w_ref[a1:, :], preferred_element_type=F32))
    r = alpha * x_ref[0] + mod_ref[0, 2:3, :] * y
    mu = jnp.mean(r, axis=-1, keepdims=True)
    rc = r - mu
    var = jnp.mean(rc * rc, axis=-1, keepdims=True)
    o_ref[0] = rc * lax.rsqrt(var + LN_EPS) * lg_ref[0] + lb_ref[0]


def _out_projection(att, fo, gm, x, mod3, per_batch_mod, w_out, ln_g, ln_b, layer, alpha, tm):
    b, l, d = x.shape
    tm = _row_tile(l, tm)
    row = lambda w: pl.BlockSpec((1, tm, w), lambda bi, i: (bi, i, 0))
    mod_map = (lambda bi, i: (bi, 0, 0)) if per_batch_mod else (lambda bi, i: (0, 0, 0))
    vec = pl.BlockSpec((1, 1, d), lambda bi, i: (layer, 0, 0))
    return pl.pallas_call(
        functools.partial(_outproj_kernel, alpha=alpha),
        grid=(b, l // tm),
        in_specs=[row(ATT_WIDTH), row(FOURIER_WIDTH), row(GMLP_WIDTH), row(d),
                  pl.BlockSpec((1, 3, d), mod_map),
                  pl.BlockSpec(w_out.shape, lambda bi, i: (0, 0)),
                  vec, vec],
        out_specs=row(d),
        out_shape=jax.ShapeDtypeStruct((b, l, d), F32),
        name=f"outproj_{l}",
        compiler_params=_cparams("parallel", "parallel"),
    )(att, fo, gm, x, mod3, w_out, ln_g, ln_b)


def _rope_tables(l):
    n_freq = HEAD_DIM // 4
    rows = l // GRID_W
    row = np.repeat(np.arange(rows, dtype=np.float64), GRID_W)
    col = np.tile(np.arange(GRID_W, dtype=np.float64), rows)
    inv = ROPE_THETA ** (-np.arange(n_freq, dtype=np.float64) / n_freq)
    ang_r, ang_c = row[:, None] * inv, col[:, None] * inv
    zeros = np.zeros_like(ang_r)

    def per_head(r_lo, r_hi, c_lo, c_hi):
        one = np.concatenate([r_lo, r_hi, c_lo, c_hi], axis=-1)
        return jnp.asarray(np.concatenate([one, one], axis=-1), F32)

    cr, sr, cc, sc = np.cos(ang_r), np.sin(ang_r), np.cos(ang_c), np.sin(ang_c)
    cos_t = per_head(cr, cr, cc, cc)
    sa_t = per_head(-sr, zeros, -sc, zeros)
    sb_t = per_head(zeros, sr, zeros, sc)
    return cos_t, sa_t, sb_t


def _run_layer(x, mod3, per_batch_mod, layer, lam_init, alpha, p, gmat, rope_tabs, cache, emit_kv,
               tiles):
    outs = _in_projection(x, mod3, per_batch_mod, p["w_in"][layer], p["fa"], p["fb"], layer,
                          rope_tabs, emit_kv, tiles["tm_in"])
    q, k, v, g_att, uf, g_f, u, vm, g_m = outs[:9]
    att = _attention(q, k, v, g_att, cache, layer, p["lam"], p["subln_w"], lam_init,
                     tiles["tq"], tiles["ck"])
    fo = _fourier(gmat, uf, g_f, tiles["tr"], tiles["nb"])
    gm = _gmlp(vm, u, g_m, p["gmlp_w"], p["gmlp_bias"], p["avg"], layer, tiles["tm_gmlp"])
    x_new = _out_projection(att, fo, gm, x, mod3, per_batch_mod, p["w_out"][layer],
                            p["ln_g"], p["ln_b"], layer, alpha, tiles["tm_out"])
    return x_new, outs[9:]


def kernel(x_prompt, x_sample, c, cache_k, cache_v, c_ctx, w_ada, b_ada, w_in, w_out,
           lam_q1, lam_k1, lam_q2, lam_k2, subln_w, fourier_w, gmlp_ws, gmlp_bs, ln_g, ln_b):
    depth, d, _ = w_ada.shape
    nb_lat = c.shape[0]
    l_ctx, l_lat = x_prompt.shape[1], x_sample.shape[1]
    assert 1 + nb_lat <= MOD_ROWS
    alpha = (2.0 * depth) ** 0.25

    cvec = jnp.concatenate([c_ctx[None, :], c, jnp.zeros((MOD_ROWS - 1 - nb_lat, d), F32)], axis=0)
    mod = _modulation(cvec, w_ada, b_ada)

    fa, fb = _fourier_channel_maps(fourier_w, 1.0 / math.sqrt(FOURIER_GROUP_DIM))
    params = {
        "w_in": w_in.astype(BF16), "w_out": w_out.astype(BF16), "fa": fa, "fb": fb,
        "lam": tuple(t[:, None, :] for t in (lam_q1, lam_k1, lam_q2, lam_k2)),
        "subln_w": subln_w[:, None, :], "ln_g": ln_g[:, None, :], "ln_b": ln_b[:, None, :],
        "gmlp_w": gmlp_ws.astype(BF16).transpose(0, 2, 1, 3).reshape(depth, CHUNK, N_GMLP_HEADS * CHUNK),
        "gmlp_bias": jnp.repeat(gmlp_bs.astype(F32).transpose(0, 2, 1), GMLP_HEAD_DIM, axis=2),
        "avg": jnp.asarray(np.kron(np.eye(N_GMLP_HEADS), np.full((GMLP_HEAD_DIM,) * 2, 1.0 / GMLP_HEAD_DIM)), BF16),
    }
    ctx_tiles = dict(tm_in=256, tq=128, ck=256, tr=256, nb=2, tm_gmlp=256, tm_out=256)
    lat_tiles = dict(tm_in=512, tq=256, ck=512, tr=512, nb=2, tm_gmlp=512, tm_out=512)

    g_ctx = _dft_matrix(l_ctx)
    xp = x_prompt
    ks_new, vs_new = [], []
    for layer in range(depth):
        lam_init = 0.8 - 0.6 * math.exp(-0.3 * layer)
        mod3 = mod[layer, 0:1].reshape(1, 3, d)
        xp, (k_l, v_l) = _run_layer(xp, mod3, False, layer, lam_init, alpha, params, g_ctx,
                                    None, None, True, ctx_tiles)
        ks_new.append(k_l)
        vs_new.append(v_l)
    new_k = jnp.stack(ks_new, axis=1)
    new_v = jnp.stack(vs_new, axis=1)

    g_lat = _dft_matrix(l_lat)
    rope_tabs = _rope_tables(l_lat)
    xs = x_sample
    for layer in range(depth):
        lam_init = 0.8 - 0.6 * math.exp(-0.3 * layer)
        mod3 = mod[layer, 1:1 + nb_lat].reshape(nb_lat, 3, d)
        xs, _ = _run_layer(xs, mod3, True, layer, lam_init, alpha, params, g_lat,
                           rope_tabs, (cache_k, cache_v), False, lat_tiles)
    return (xp, xs, new_k, new_v)
```

```python
import functools
import math

import numpy as np
import jax
import jax.numpy as jnp
from jax import lax
from jax.experimental import pallas as pl
from jax.experimental.pallas import tpu as pltpu

F32 = jnp.float32
BF16 = jnp.bfloat16

GRID_W = 64
HEAD_DIM = 64
N_ATT_HEADS = 4
HEAD_W = 2 * HEAD_DIM
ATT_WIDTH = N_ATT_HEADS * HEAD_W
N_FOURIER_GROUPS = 4
FOURIER_GROUP_DIM = 64
FOURIER_WIDTH = N_FOURIER_GROUPS * FOURIER_GROUP_DIM
N_GMLP_HEADS = 4
GMLP_HEAD_DIM = 64
GMLP_WIDTH = N_GMLP_HEADS * GMLP_HEAD_DIM
CHUNK = 128
ROPE_THETA = 10000.0
LN_EPS = 1e-5
RMS_EPS = 1e-5
QK_SCALE = HEAD_DIM ** -0.5
LOG2_E = math.log2(math.e)

_SEG_NAMES = ("q", "k", "v", "g_att", "f", "g_f", "u", "vm", "g_m")
_SEG_WIDTHS = (ATT_WIDTH,) * 4 + (FOURIER_WIDTH,) * 2 + (GMLP_WIDTH,) * 3
_SEG_START = dict(zip(_SEG_NAMES, np.concatenate([[0], np.cumsum(_SEG_WIDTHS)[:-1]]).tolist()))
_SEG_WIDTH = dict(zip(_SEG_NAMES, _SEG_WIDTHS))

VMEM_LIMIT_BYTES = 52 * 1024 * 1024
MOD_ROWS = 16
SOFTMAX_ROWS = 8


def _cparams(*sem):
    return pltpu.CompilerParams(dimension_semantics=sem, vmem_limit_bytes=VMEM_LIMIT_BYTES)


def _silu(x):
    return x * jax.nn.sigmoid(x)


def _row_tile(l, want):
    t = min(l, want)
    assert l % t == 0
    return t


def _mod_kernel(c_ref, w_ref, b_ref, o_ref):
    s = _silu(c_ref[...])
    o_ref[0] = jnp.dot(s, w_ref[0], precision=lax.Precision.HIGHEST,
                       preferred_element_type=F32) + b_ref[0]


def _modulation(cvec, w_ada, b_ada):
    depth, d, d3 = w_ada.shape
    tn = 1024
    return pl.pallas_call(
        _mod_kernel,
        grid=(depth, d3 // tn),
        in_specs=[pl.BlockSpec((MOD_ROWS, d), lambda l, j: (0, 0)),
                  pl.BlockSpec((1, d, tn), lambda l, j: (l, 0, j)),
                  pl.BlockSpec((1, 1, tn), lambda l, j: (l, 0, j))],
        out_specs=pl.BlockSpec((1, MOD_ROWS, tn), lambda l, j: (l, 0, j)),
        out_shape=jax.ShapeDtypeStruct((depth, MOD_ROWS, d3), F32),
        name="adaln_mod",
        compiler_params=_cparams("arbitrary", "arbitrary"),
    )(cvec, w_ada, b_ada.reshape(depth, 1, d3))


def _chanmap_kernel(cc_ref, ss_ref, w_ref, a_ref, b_ref, *, scale):
    w = w_ref[0]
    a_ref[0] = (jnp.dot(cc_ref[...], w, precision=lax.Precision.HIGHEST,
                        preferred_element_type=F32) * scale).astype(BF16)
    b_ref[0] = (jnp.dot(ss_ref[...], w, precision=lax.Precision.HIGHEST,
                        preferred_element_type=F32) * scale).astype(BF16)


def _fourier_channel_maps(fourier_w, scale):
    depth = fourier_w.shape[0]
    n = np.arange(FOURIER_GROUP_DIM)
    ang = 2.0 * np.pi * ((n[:, None] * n[None, :]) % FOURIER_GROUP_DIM) / FOURIER_GROUP_DIM
    eye = np.eye(N_FOURIER_GROUPS)
    cc = jnp.asarray(np.kron(eye, np.cos(ang)), F32)
    ss = jnp.asarray(np.kron(eye, np.sin(ang)), F32)
    wbd = jnp.einsum("gh,lgce->lgche", jnp.eye(N_FOURIER_GROUPS, dtype=F32),
                     fourier_w.astype(F32)).reshape(depth, FOURIER_WIDTH, FOURIER_WIDTH)
    w_spec = pl.BlockSpec((1, FOURIER_WIDTH, FOURIER_WIDTH), lambda l: (l, 0, 0))
    c_spec = pl.BlockSpec((FOURIER_WIDTH, FOURIER_WIDTH), lambda l: (0, 0))
    shp = jax.ShapeDtypeStruct((depth, FOURIER_WIDTH, FOURIER_WIDTH), BF16)
    return pl.pallas_call(
        functools.partial(_chanmap_kernel, scale=scale),
        grid=(depth,),
        in_specs=[c_spec, c_spec, w_spec],
        out_specs=[w_spec, w_spec],
        out_shape=[shp, shp],
        name="fourier_chanmap",
        compiler_params=_cparams("arbitrary"),
    )(cc, ss, wbd)


def _dft_kernel(ca_ref, sa_ref, cb_ref, sb_ref, g_ref, *, l):
    ca, sa = ca_ref[0], sa_ref[0]
    cb, sb = cb_ref[...], sb_ref[...]
    norm = 1.0 / math.sqrt(l)
    g_ref[:, 0:l] = ((ca * cb - sa * sb) * norm).astype(BF16)
    g_ref[:, l:2 * l] = ((sa * cb + ca * sb) * -norm).astype(BF16)


def _dft_matrix(l):
    na = l // GRID_W
    k = np.arange(l, dtype=np.int64)
    ang_a = ((np.arange(na)[:, None] * k[None, :]) % na) * (2.0 * np.pi / na)
    ang_b = ((np.arange(GRID_W)[:, None] * k[None, :]) % l) * (2.0 * np.pi / l)
    ca = jnp.asarray(np.cos(ang_a).reshape(na, 1, l), F32)
    sa = jnp.asarray(np.sin(ang_a).reshape(na, 1, l), F32)
    cb, sb = jnp.asarray(np.cos(ang_b), F32), jnp.asarray(np.sin(ang_b), F32)
    a_spec = pl.BlockSpec((1, 1, l), lambda i: (i, 0, 0))
    b_spec = pl.BlockSpec((GRID_W, l), lambda i: (0, 0))
    return pl.pallas_call(
        functools.partial(_dft_kernel, l=l),
        grid=(na,),
        in_specs=[a_spec, a_spec, b_spec, b_spec],
        out_specs=pl.BlockSpec((GRID_W, 2 * l), lambda i: (i, 0)),
        out_shape=jax.ShapeDtypeStruct((l, 2 * l), BF16),
        name=f"dft_matrix_{l}",
        compiler_params=_cparams("arbitrary"),
    )(ca, sa, cb, sb)


def _inproj_kernel(*refs, rope, emit_kv):
    x_ref, mod_ref, w_ref, fa_ref, fb_ref = refs[:5]
    refs = refs[5:]
    if rope:
        cos_ref, sa_ref, sb_ref = refs[:3]
        refs = refs[3:]
    q_ref, k_ref, v_ref, ga_ref, uf_ref, gf_ref, u_ref, vm_ref, gm_ref = refs[:9]
    refs = refs[9:]

    x = x_ref[0]
    mu = jnp.mean(x, axis=-1, keepdims=True)
    xc = x - mu
    var = jnp.mean(xc * xc, axis=-1, keepdims=True)
    shift, scale = mod_ref[0, 0:1, :], mod_ref[0, 1:2, :]
    h = (xc * lax.rsqrt(var + LN_EPS) * (1.0 + scale) + shift).astype(BF16)

    def proj(name):
        c0 = _SEG_START[name]
        return jnp.dot(h, w_ref[:, c0:c0 + _SEG_WIDTH[name]], preferred_element_type=F32)

    def rotate(z):
        return (z * cos_ref[...] + pltpu.roll(z, HEAD_W - 16, 1) * sa_ref[...]
                + pltpu.roll(z, 16, 1) * sb_ref[...])

    zq, zk, zv = proj("q"), proj("k"), proj("v")
    for hh in range(N_ATT_HEADS):
        sl = slice(hh * HEAD_W, (hh + 1) * HEAD_W)
        qh, kh = zq[:, sl], zk[:, sl]
        if emit_kv:
            kf_ref, vf_ref = refs
            kf_ref[0, hh] = kh
            vf_ref[0, hh] = zv[:, sl]
        if rope:
            qh, kh = rotate(qh), rotate(kh)
        q_ref[0, :, sl] = (qh * (QK_SCALE * LOG2_E)).astype(BF16)
        k_ref[0, :, sl] = kh.astype(BF16)
    v_ref[0] = zv.astype(BF16)
    ga_ref[0] = proj("g_att")
    f = proj("f").astype(BF16)
    uf_ref[0, 0] = jnp.dot(f, fa_ref[0], preferred_element_type=F32).astype(BF16)
    uf_ref[0, 1] = jnp.dot(f, fb_ref[0], preferred_element_type=F32).astype(BF16)
    gf_ref[0] = proj("g_f")
    u_ref[0] = proj("u")
    vm_ref[0] = proj("vm")
    gm_ref[0] = proj("g_m")


def _in_projection(x, mod3, per_batch_mod, w_in, fa, fb, layer, rope_tabs, emit_kv, tm):
    b, l, d = x.shape
    n_in = w_in.shape[1]
    tm = _row_tile(l, tm)
    row = lambda w: pl.BlockSpec((1, tm, w), lambda bi, i: (bi, i, 0))
    mod_map = (lambda bi, i: (bi, 0, 0)) if per_batch_mod else (lambda bi, i: (0, 0, 0))
    fmap = pl.BlockSpec((1, FOURIER_WIDTH, FOURIER_WIDTH), lambda bi, i: (layer, 0, 0))
    in_specs = [row(d),
                pl.BlockSpec((1, 3, d), mod_map),
                pl.BlockSpec((d, n_in), lambda bi, i: (0, 0)),
                fmap, fmap]
    args = [x, mod3, w_in, fa, fb]
    if rope_tabs is not None:
        in_specs += [pl.BlockSpec((tm, HEAD_W), lambda bi, i: (i, 0))] * 3
        args += list(rope_tabs)
    sds = jax.ShapeDtypeStruct
    out_shape = [sds((b, l, ATT_WIDTH), BF16)] * 3 + [sds((b, l, ATT_WIDTH), F32),
                 sds((b, 2, l, FOURIER_WIDTH), BF16), sds((b, l, FOURIER_WIDTH), F32),
                 sds((b, l, GMLP_WIDTH), F32), sds((b, l, GMLP_WIDTH), F32),
                 sds((b, l, GMLP_WIDTH), F32)]
    out_specs = [row(ATT_WIDTH)] * 4 + [
        pl.BlockSpec((1, 2, tm, FOURIER_WIDTH), lambda bi, i: (bi, 0, i, 0)),
        row(FOURIER_WIDTH), row(GMLP_WIDTH), row(GMLP_WIDTH), row(GMLP_WIDTH)]
    if emit_kv:
        kv = sds((b, N_ATT_HEADS, l, HEAD_W), F32)
        out_shape += [kv, kv]
        out_specs += [pl.BlockSpec((1, N_ATT_HEADS, tm, HEAD_W), lambda bi, i: (bi, 0, i, 0))] * 2
    return pl.pallas_call(
        functools.partial(_inproj_kernel, rope=rope_tabs is not None, emit_kv=emit_kv),
        grid=(b, l // tm),
        in_specs=in_specs, out_specs=out_specs, out_shape=out_shape,
        name=f"inproj_{l}",
        compiler_params=_cparams("parallel", "parallel"),
    )(*args)


def _attn_kernel(*refs, lam_init, has_cache, ck, n_new, past, tq, nq):
    if has_cache:
        (q_ref, k_ref, v_ref, pk_ref, pv_ref, g_ref, lq1_ref, lk1_ref, lq2_ref, lk2_ref,
         sw_ref, o_ref, s_ref, m_ref) = refs
    else:
        (q_ref, k_ref, v_ref, g_ref, lq1_ref, lk1_ref, lq2_ref, lk2_ref,
         sw_ref, o_ref, s_ref, m_ref) = refs

    chunks = []
    if has_cache:
        chunks.append((True, 0, 0, past))
    for j in range(n_new):
        chunks.append((False, j * ck, past + j * ck, ck))

    def keys(c):
        return pk_ref[0, 0, 0].astype(BF16) if c[0] else k_ref[0, c[1]:c[1] + c[3], :]

    def vals(c):
        return pv_ref[0, 0, 0].astype(BF16) if c[0] else v_ref[0, c[1]:c[1] + c[3], :]

    def scores(qq, c):
        return lax.dot_general(qq, keys(c), (((1,), (1,)), ((), ())), preferred_element_type=F32)

    def lane_tiles(x):
        return [x[:, t * 128:(t + 1) * 128] for t in range(x.shape[1] // 128)]

    lam = (jnp.exp(jnp.sum(lq1_ref[0] * lk1_ref[0], axis=-1, keepdims=True))
           - jnp.exp(jnp.sum(lq2_ref[0] * lk2_ref[0], axis=-1, keepdims=True)) + lam_init)

    def output(lsum, rows):
        l1, l2 = lsum[:tq], lsum[tq:]
        rho = lam * l1 / l2
        acc = jnp.zeros((tq, HEAD_W), F32)
        for c in chunks:
            a = (s_ref[0:tq, c[2]:c[2] + c[3]]
                 - rho * s_ref[tq:2 * tq, c[2]:c[2] + c[3]]).astype(BF16)
            acc = acc + jnp.dot(a, vals(c), preferred_element_type=F32)
        o = acc / l1
        o = (o * lax.rsqrt(jnp.mean(o * o, axis=-1, keepdims=True) + RMS_EPS)
             * sw_ref[0] * (1.0 - lam_init))
        return o * _silu(g_ref[0, rows, :])

    def tile(t, carry):
        rows = pl.ds(pl.multiple_of(t * tq, tq), tq)
        q = q_ref[0, rows, :]
        lane = lax.broadcasted_iota(jnp.int32, q.shape, 1)
        zero = jnp.zeros_like(q)
        qq = jnp.concatenate([jnp.where(lane < HEAD_DIM, q, zero),
                              jnp.where(lane >= HEAD_DIM, q, zero)], axis=0)

        s0 = scores(qq, chunks[0])
        stab = jnp.broadcast_to(jnp.max(s0, axis=-1, keepdims=True), (2 * tq, 128))
        lv = jnp.zeros((2 * tq, 128), F32)
        for ci, c in enumerate(chunks):
            s = s0 if ci == 0 else scores(qq, c)
            for ti, st in enumerate(lane_tiles(s)):
                p = jnp.exp2(st - stab)
                s_ref[:, c[2] + ti * 128:c[2] + (ti + 1) * 128] = p
                lv = lv + p
        out = output(jnp.sum(lv, axis=-1, keepdims=True), rows)
        o_ref[0, rows, :] = out.astype(BF16)
        n_bad = jnp.sum(jnp.where(jnp.isfinite(out), 0.0, 1.0))

        @pl.when(n_bad > 0.0)
        def _exact_max():
            for ci, c in enumerate(chunks):
                s = scores(qq, c)
                s_ref[:, c[2]:c[2] + c[3]] = s
                cm = functools.reduce(jnp.maximum, lane_tiles(s))
                m_ref[...] = cm if ci == 0 else jnp.maximum(m_ref[...], cm)
            lparts = []
            for gi in range(2 * tq // SOFTMAX_ROWS):
                grp = slice(gi * SOFTMAX_ROWS, (gi + 1) * SOFTMAX_ROWS)
                m = jnp.max(m_ref[grp, :], axis=-1, keepdims=True)
                p = jnp.exp2(s_ref[grp, :] - m)
                s_ref[grp, :] = p
                lparts.append(jnp.sum(p, axis=-1, keepdims=True))
            o_ref[0, rows, :] = output(jnp.concatenate(lparts, axis=0), rows).astype(BF16)

        return carry

    lax.fori_loop(0, nq, tile, 0)


def _attention(q, k, v, g_att, cache, layer, lam_params, subln_w, lam_init, tq, ck):
    b, l, _ = q.shape
    tq = _row_tile(l, tq)
    ck = _row_tile(l, ck)
    nq = l // tq
    has_cache = cache is not None
    past = cache[0].shape[3] if has_cache else 0
    lk = past + l
    seq = pl.BlockSpec((1, l, HEAD_W), lambda bi, h: (bi, 0, h))
    vec = lambda w: pl.BlockSpec((1, 1, w), lambda bi, h: (layer, 0, 0))
    in_specs = [seq, seq, seq]
    args = [q, k, v]
    if has_cache:
        cspec = pl.BlockSpec((1, 1, 1, past, HEAD_W), lambda bi, h: (bi, layer, h, 0, 0))
        in_specs += [cspec, cspec]
        args += list(cache)
    in_specs += [seq] + [vec(HEAD_DIM)] * 4 + [vec(HEAD_W)]
    args += [g_att] + list(lam_params) + [subln_w]
    return pl.pallas_call(
        functools.partial(_attn_kernel, lam_init=lam_init, has_cache=has_cache, ck=ck,
                          n_new=l // ck, past=past, tq=tq, nq=nq),
        grid=(b, N_ATT_HEADS),
        in_specs=in_specs,
        out_specs=seq,
        out_shape=jax.ShapeDtypeStruct((b, l, ATT_WIDTH), BF16),
        scratch_shapes=[pltpu.VMEM((2 * tq, lk), F32), pltpu.VMEM((2 * tq, 128), F32)],
        name=f"attn_{l}",
        compiler_params=_cparams("parallel", "parallel"),
    )(*args)


def _fourier_kernel(g_ref, u_ref, gate_ref, o_ref, *, nb):
    g = g_ref[...]
    for j in range(nb):
        y = jnp.dot(g, u_ref[j], preferred_element_type=F32)
        o_ref[j] = (y * _silu(gate_ref[j])).astype(BF16)


def _fourier(gmat, uf, g_f, tr, nb):
    b, l, w = g_f.shape
    tr = _row_tile(l, tr)
    u2 = uf.reshape(b, 2 * l, w)
    return pl.pallas_call(
        functools.partial(_fourier_kernel, nb=nb),
        grid=(b // nb, l // tr),
        in_specs=[pl.BlockSpec((tr, 2 * l), lambda bi, i: (i, 0)),
                  pl.BlockSpec((nb, 2 * l, w), lambda bi, i: (bi, 0, 0)),
                  pl.BlockSpec((nb, tr, w), lambda bi, i: (bi, i, 0))],
        out_specs=pl.BlockSpec((nb, tr, w), lambda bi, i: (bi, i, 0)),
        out_shape=jax.ShapeDtypeStruct((b, l, w), BF16),
        name=f"fourier_{l}",
        compiler_params=_cparams("parallel", "arbitrary"),
    )(gmat, u2, g_f)


def _gmlp_kernel(vm_ref, u_ref, gm_ref, w_ref, bias_ref, avg_ref, o_ref, *, n_chunks):
    avg = avg_ref[...]

    def group_mean(x):
        hi = x.astype(BF16)
        lo = (x - hi.astype(F32)).astype(BF16)
        return (jnp.dot(hi, avg, preferred_element_type=F32)
                + jnp.dot(lo, avg, preferred_element_type=F32))

    lane = lax.broadcasted_iota(jnp.int32, (CHUNK, GMLP_WIDTH), 1)
    head_mask = [(lane >= hh * GMLP_HEAD_DIM) & (lane < (hh + 1) * GMLP_HEAD_DIM)
                 for hh in range(N_GMLP_HEADS)]
    for c in range(n_chunks):
        rows = slice(c * CHUNK, (c + 1) * CHUNK)
        x = vm_ref[0, rows, :]
        xc = x - group_mean(x)
        vn = (xc * lax.rsqrt(group_mean(xc * xc) + LN_EPS)).astype(BF16)
        stacked = jnp.concatenate(
            [jnp.where(head_mask[hh], vn, jnp.zeros_like(vn)) for hh in range(N_GMLP_HEADS)], axis=0)
        s = jnp.dot(w_ref[0], stacked, preferred_element_type=F32) + bias_ref[0]
        o_ref[0, rows, :] = (u_ref[0, rows, :] * s * _silu(gm_ref[0, rows, :])).astype(BF16)


def _gmlp(vm, u, g_m, w_cat, bias_full, avg, layer, tm):
    b, l, w = vm.shape
    tm = _row_tile(l, tm)
    row = pl.BlockSpec((1, tm, w), lambda bi, i: (bi, i, 0))
    return pl.pallas_call(
        functools.partial(_gmlp_kernel, n_chunks=tm // CHUNK),
        grid=(b, l // tm),
        in_specs=[row, row, row,
                  pl.BlockSpec((1, CHUNK, N_GMLP_HEADS * CHUNK), lambda bi, i: (layer, 0, 0)),
                  pl.BlockSpec((1, CHUNK, w), lambda bi, i: (layer, 0, 0)),
                  pl.BlockSpec((w, w), lambda bi, i: (0, 0))],
        out_specs=row,
        out_shape=jax.ShapeDtypeStruct((b, l, w), BF16),
        name=f"gmlp_{l}",
        compiler_params=_cparams("parallel", "parallel"),
    )(vm, u, g_m, w_cat, bias_full, avg)


def _outproj_kernel(att_ref, fo_ref, gm_ref, x_ref, mod_ref, w_ref, lg_ref, lb_ref, o_ref, *, alpha):
    a0, a1 = ATT_WIDTH, ATT_WIDTH + FOURIER_WIDTH
    y = (jnp.dot(att_ref[0], w_ref[0:a0, :], preferred_element_type=F32)
         + jnp.dot(fo_ref[0], w_ref[a0:a1, :], preferred_element_type=F32)
         + jnp.dot(gm_ref[0], w_ref[a1:, :], preferred_element_type=F32))
    r = alpha * x_ref[0] + mod_ref[0, 2:3, :] * y
    mu = jnp.mean(r, axis=-1, keepdims=True)
    rc = r - mu
    var = jnp.mean(rc * rc, axis=-1, keepdims=True)
    o_ref[0] = rc * lax.rsqrt(var + LN_EPS) * lg_ref[0] + lb_ref[0]


def _out_projection(att, fo, gm, x, mod3, per_batch_mod, w_out, ln_g, ln_b, layer, alpha, tm):
    b, l, d = x.shape
    tm = _row_tile(l, tm)
    row = lambda w: pl.BlockSpec((1, tm, w), lambda bi, i: (bi, i, 0))
    mod_map = (lambda bi, i: (bi, 0, 0)) if per_batch_mod else (lambda bi, i: (0, 0, 0))
    vec = pl.BlockSpec((1, 1, d), lambda bi, i: (layer, 0, 0))
    return pl.pallas_call(
        functools.partial(_outproj_kernel, alpha=alpha),
        grid=(b, l // tm),
        in_specs=[row(ATT_WIDTH), row(FOURIER_WIDTH), row(GMLP_WIDTH), row(d),
                  pl.BlockSpec((1, 3, d), mod_map),
                  pl.BlockSpec(w_out.shape, lambda bi, i: (0, 0)),
                  vec, vec],
        out_specs=row(d),
        out_shape=jax.ShapeDtypeStruct((b, l, d), F32),
        name=f"outproj_{l}",
        compiler_params=_cparams("parallel", "parallel"),
    )(att, fo, gm, x, mod3, w_out, ln_g, ln_b)


def _rope_tables(l):
    n_freq = HEAD_DIM // 4
    rows = l // GRID_W
    row = np.repeat(np.arange(rows, dtype=np.float64), GRID_W)
    col = np.tile(np.arange(GRID_W, dtype=np.float64), rows)
    inv = ROPE_THETA ** (-np.arange(n_freq, dtype=np.float64) / n_freq)
    ang_r, ang_c = row[:, None] * inv, col[:, None] * inv
    zeros = np.zeros_like(ang_r)

    def per_head(r_lo, r_hi, c_lo, c_hi):
        one = np.concatenate([r_lo, r_hi, c_lo, c_hi], axis=-1)
        return jnp.asarray(np.concatenate([one, one], axis=-1), F32)

    cr, sr, cc, sc = np.cos(ang_r), np.sin(ang_r), np.cos(ang_c), np.sin(ang_c)
    cos_t = per_head(cr, cr, cc, cc)
    sa_t = per_head(-sr, zeros, -sc, zeros)
    sb_t = per_head(zeros, sr, zeros, sc)
    return cos_t, sa_t, sb_t


def _run_layer(x, mod3, per_batch_mod, layer, lam_init, alpha, p, gmat, rope_tabs, cache, emit_kv,
               tiles):
    outs = _in_projection(x, mod3, per_batch_mod, p["w_in"][layer], p["fa"], p["fb"], layer,
                          rope_tabs, emit_kv, tiles["tm_in"])
    q, k, v, g_att, uf, g_f, u, vm, g_m = outs[:9]
    att = _attention(q, k, v, g_att, cache, layer, p["lam"], p["subln_w"], lam_init,
                     tiles["tq"], tiles["ck"])
    fo = _fourier(gmat, uf, g_f, tiles["tr"], tiles["nb"])
    gm = _gmlp(vm, u, g_m, p["gmlp_w"], p["gmlp_bias"], p["avg"], layer, tiles["tm_gmlp"])
    x_new = _out_projection(att, fo, gm, x, mod3, per_batch_mod, p["w_out"][layer],
                            p["ln_g"], p["ln_b"], layer, alpha, tiles["tm_out"])
    return x_new, outs[9:]


def kernel(x_prompt, x_sample, c, cache_k, cache_v, c_ctx, w_ada, b_ada, w_in, w_out,
           lam_q1, lam_k1, lam_q2, lam_k2, subln_w, fourier_w, gmlp_ws, gmlp_bs, ln_g, ln_b):
    depth, d, _ = w_ada.shape
    nb_lat = c.shape[0]
    l_ctx, l_lat = x_prompt.shape[1], x_sample.shape[1]
    assert 1 + nb_lat <= MOD_ROWS
    alpha = (2.0 * depth) ** 0.25

    cvec = jnp.concatenate([c_ctx[None, :], c, jnp.zeros((MOD_ROWS - 1 - nb_lat, d), F32)], axis=0)
    mod = _modulation(cvec, w_ada, b_ada)

    fa, fb = _fourier_channel_maps(fourier_w, 1.0 / math.sqrt(FOURIER_GROUP_DIM))
    params = {
        "w_in": w_in.astype(BF16), "w_out": w_out.astype(BF16), "fa": fa, "fb": fb,
        "lam": tuple(t[:, None, :] for t in (lam_q1, lam_k1, lam_q2, lam_k2)),
        "subln_w": subln_w[:, None, :], "ln_g": ln_g[:, None, :], "ln_b": ln_b[:, None, :],
        "gmlp_w": gmlp_ws.astype(BF16).transpose(0, 2, 1, 3).reshape(depth, CHUNK, N_GMLP_HEADS * CHUNK),
        "gmlp_bias": jnp.repeat(gmlp_bs.astype(F32).transpose(0, 2, 1), GMLP_HEAD_DIM, axis=2),
        "avg": jnp.asarray(np.kron(np.eye(N_GMLP_HEADS), np.full((GMLP_HEAD_DIM,) * 2, 1.0 / GMLP_HEAD_DIM)), BF16),
    }
    ctx_tiles = dict(tm_in=256, tq=128, ck=256, tr=256, nb=2, tm_gmlp=256, tm_out=256)
    lat_tiles = dict(tm_in=512, tq=256, ck=512, tr=512, nb=2, tm_gmlp=512, tm_out=512)

    g_ctx = _dft_matrix(l_ctx)
    xp = x_prompt
    ks_new, vs_new = [], []
    for layer in range(depth):
        lam_init = 0.8 - 0.6 * math.exp(-0.3 * layer)
        mod3 = mod[layer, 0:1].reshape(1, 3, d)
        xp, (k_l, v_l) = _run_layer(xp, mod3, False, layer, lam_init, alpha, params, g_ctx,
                                    None, None, True, ctx_tiles)
        ks_new.append(k_l)
        vs_new.append(v_l)
    new_k = jnp.stack(ks_new, axis=1)
    new_v = jnp.stack(vs_new, axis=1)

    g_lat = _dft_matrix(l_lat)
    rope_tabs = _rope_tables(l_lat)
    xs = x_sample
    for layer in range(depth):
        lam_init = 0.8 - 0.6 * math.exp(-0.3 * layer)
        mod3 = mod[layer, 1:1 + nb_lat].reshape(nb_lat, 3, d)
        xs, _ = _run_layer(xs, mod3, True, layer, lam_init, alpha, params, g_lat,
                           rope_tabs, (cache_k, cache_v), False, lat_tiles)
    return (xp, xs, new_k, new_v)
```

```python
import functools
import math

import numpy as np
import jax
import jax.numpy as jnp
from jax import lax
from jax.experimental import pallas as pl
from jax.experimental.pallas import tpu as pltpu

F32 = jnp.float32
BF16 = jnp.bfloat16

GRID_W = 64
HEAD_DIM = 64
N_ATT_HEADS = 4
HEAD_W = 2 * HEAD_DIM
ATT_WIDTH = N_ATT_HEADS * HEAD_W
N_FOURIER_GROUPS = 4
FOURIER_GROUP_DIM = 64
FOURIER_WIDTH = N_FOURIER_GROUPS * FOURIER_GROUP_DIM
N_GMLP_HEADS = 4
GMLP_HEAD_DIM = 64
GMLP_WIDTH = N_GMLP_HEADS * GMLP_HEAD_DIM
CHUNK = 128
ROPE_THETA = 10000.0
LN_EPS = 1e-5
RMS_EPS = 1e-5
QK_SCALE = HEAD_DIM ** -0.5
LOG2_E = math.log2(math.e)

_SEG_NAMES = ("q", "k", "v", "g_att", "f", "g_f", "u", "vm", "g_m")
_SEG_WIDTHS = (ATT_WIDTH,) * 4 + (FOURIER_WIDTH,) * 2 + (GMLP_WIDTH,) * 3
_SEG_START = dict(zip(_SEG_NAMES, np.concatenate([[0], np.cumsum(_SEG_WIDTHS)[:-1]]).tolist()))
_SEG_WIDTH = dict(zip(_SEG_NAMES, _SEG_WIDTHS))

VMEM_LIMIT_BYTES = 52 * 1024 * 1024
MOD_ROWS = 16
SOFTMAX_ROWS = 8


def _cparams(*sem):
    return pltpu.CompilerParams(dimension_semantics=sem, vmem_limit_bytes=VMEM_LIMIT_BYTES)


def _silu(x):
    return x * jax.nn.sigmoid(x)


def _row_tile(l, want):
    t = min(l, want)
    assert l % t == 0
    return t


def _mod_kernel(c_ref, w_ref, b_ref, o_ref):
    s = _silu(c_ref[...])
    o_ref[0] = jnp.dot(s, w_ref[0], precision=lax.Precision.HIGHEST,
                       preferred_element_type=F32) + b_ref[0]


def _modulation(cvec, w_ada, b_ada):
    depth, d, d3 = w_ada.shape
    tn = 1024
    return pl.pallas_call(
        _mod_kernel,
        grid=(depth, d3 // tn),
        in_specs=[pl.BlockSpec((MOD_ROWS, d), lambda l, j: (0, 0)),
                  pl.BlockSpec((1, d, tn), lambda l, j: (l, 0, j)),
                  pl.BlockSpec((1, 1, tn), lambda l, j: (l, 0, j))],
        out_specs=pl.BlockSpec((1, MOD_ROWS, tn), lambda l, j: (l, 0, j)),
        out_shape=jax.ShapeDtypeStruct((depth, MOD_ROWS, d3), F32),
        name="adaln_mod",
        compiler_params=_cparams("arbitrary", "arbitrary"),
    )(cvec, w_ada, b_ada.reshape(depth, 1, d3))


def _chanmap_kernel(cc_ref, ss_ref, w_ref, a_ref, b_ref, *, scale):
    w = w_ref[0]
    a_ref[0] = (jnp.dot(cc_ref[...], w, precision=lax.Precision.HIGHEST,
                        preferred_element_type=F32) * scale).astype(BF16)
    b_ref[0] = (jnp.dot(ss_ref[...], w, precision=lax.Precision.HIGHEST,
                        preferred_element_type=F32) * scale).astype(BF16)


def _fourier_channel_maps(fourier_w, scale):
    depth = fourier_w.shape[0]
    n = np.arange(FOURIER_GROUP_DIM)
    ang = 2.0 * np.pi * ((n[:, None] * n[None, :]) % FOURIER_GROUP_DIM) / FOURIER_GROUP_DIM
    eye = np.eye(N_FOURIER_GROUPS)
    cc = jnp.asarray(np.kron(eye, np.cos(ang)), F32)
    ss = jnp.asarray(np.kron(eye, np.sin(ang)), F32)
    wbd = jnp.einsum("gh,lgce->lgche", jnp.eye(N_FOURIER_GROUPS, dtype=F32),
                     fourier_w.astype(F32)).reshape(depth, FOURIER_WIDTH, FOURIER_WIDTH)
    w_spec = pl.BlockSpec((1, FOURIER_WIDTH, FOURIER_WIDTH), lambda l: (l, 0, 0))
    c_spec = pl.BlockSpec((FOURIER_WIDTH, FOURIER_WIDTH), lambda l: (0, 0))
    shp = jax.ShapeDtypeStruct((depth, FOURIER_WIDTH, FOURIER_WIDTH), BF16)
    return pl.pallas_call(
        functools.partial(_chanmap_kernel, scale=scale),
        grid=(depth,),
        in_specs=[c_spec, c_spec, w_spec],
        out_specs=[w_spec, w_spec],
        out_shape=[shp, shp],
        name="fourier_chanmap",
        compiler_params=_cparams("arbitrary"),
    )(cc, ss, wbd)


def _dft_kernel(ca_ref, sa_ref, cb_ref, sb_ref, g_ref, *, l):
    ca, sa = ca_ref[0], sa_ref[0]
    cb, sb = cb_ref[...], sb_ref[...]
    norm = 1.0 / math.sqrt(l)
    g_ref[:, 0:l] = ((ca * cb - sa * sb) * norm).astype(BF16)
    g_ref[:, l:2 * l] = ((sa * cb + ca * sb) * -norm).astype(BF16)


def _dft_matrix(l):
    na = l // GRID_W
    k = np.arange(l, dtype=np.int64)
    ang_a = ((np.arange(na)[:, None] * k[None, :]) % na) * (2.0 * np.pi / na)
    ang_b = ((np.arange(GRID_W)[:, None] * k[None, :]) % l) * (2.0 * np.pi / l)
    ca = jnp.asarray(np.cos(ang_a).reshape(na, 1, l), F32)
    sa = jnp.asarray(np.sin(ang_a).reshape(na, 1, l), F32)
    cb, sb = jnp.asarray(np.cos(ang_b), F32), jnp.asarray(np.sin(ang_b), F32)
    a_spec = pl.BlockSpec((1, 1, l), lambda i: (i, 0, 0))
    b_spec = pl.BlockSpec((GRID_W, l), lambda i: (0, 0))
    return pl.pallas_call(
        functools.partial(_dft_kernel, l=l),
        grid=(na,),
        in_specs=[a_spec, a_spec, b_spec, b_spec],
        out_specs=pl.BlockSpec((GRID_W, 2 * l), lambda i: (i, 0)),
        out_shape=jax.ShapeDtypeStruct((l, 2 * l), BF16),
        name=f"dft_matrix_{l}",
        compiler_params=_cparams("arbitrary"),
    )(ca, sa, cb, sb)


def _inproj_kernel(*refs, rope, emit_kv):
    x_ref, mod_ref, w_ref, fa_ref, fb_ref = refs[:5]
    refs = refs[5:]
    if rope:
        cos_ref, sa_ref, sb_ref = refs[:3]
        refs = refs[3:]
    q_ref, k_ref, v_ref, ga_ref, uf_ref, gf_ref, u_ref, vm_ref, gm_ref = refs[:9]
    refs = refs[9:]

    x = x_ref[0]
    mu = jnp.mean(x, axis=-1, keepdims=True)
    xc = x - mu
    var = jnp.mean(xc * xc, axis=-1, keepdims=True)
    shift, scale = mod_ref[0, 0:1, :], mod_ref[0, 1:2, :]
    h = (xc * lax.rsqrt(var + LN_EPS) * (1.0 + scale) + shift).astype(BF16)

    def proj(name):
        c0 = _SEG_START[name]
        return jnp.dot(h, w_ref[:, c0:c0 + _SEG_WIDTH[name]], preferred_element_type=F32)

    def rotate(z):
        return (z * cos_ref[...] + pltpu.roll(z, HEAD_W - 16, 1) * sa_ref[...]
                + pltpu.roll(z, 16, 1) * sb_ref[...])

    zq, zk, zv = proj("q"), proj("k"), proj("v")
    for hh in range(N_ATT_HEADS):
        sl = slice(hh * HEAD_W, (hh + 1) * HEAD_W)
        qh, kh = zq[:, sl], zk[:, sl]
        if emit_kv:
            kf_ref, vf_ref = refs
            kf_ref[0, hh] = kh
            vf_ref[0, hh] = zv[:, sl]
        if rope:
            qh, kh = rotate(qh), rotate(kh)
        q_ref[0, :, sl] = (qh * (QK_SCALE * LOG2_E)).astype(BF16)
        k_ref[0, :, sl] = kh.astype(BF16)
    v_ref[0] = zv.astype(BF16)
    ga_ref[0] = proj("g_att")
    f = proj("f").astype(BF16)
    uf_ref[0, 0] = jnp.dot(f, fa_ref[0], preferred_element_type=F32).astype(BF16)
    uf_ref[0, 1] = jnp.dot(f, fb_ref[0], preferred_element_type=F32).astype(BF16)
    gf_ref[0] = proj("g_f")
    u_ref[0] = proj("u")
    vm_ref[0] = proj("vm")
    gm_ref[0] = proj("g_m")


def _in_projection(x, mod3, per_batch_mod, w_in, fa, fb, layer, rope_tabs, emit_kv, tm):
    b, l, d = x.shape
    n_in = w_in.shape[1]
    tm = _row_tile(l, tm)
    row = lambda w: pl.BlockSpec((1, tm, w), lambda bi, i: (bi, i, 0))
    mod_map = (lambda bi, i: (bi, 0, 0)) if per_batch_mod else (lambda bi, i: (0, 0, 0))
    fmap = pl.BlockSpec((1, FOURIER_WIDTH, FOURIER_WIDTH), lambda bi, i: (layer, 0, 0))
    in_specs = [row(d),
                pl.BlockSpec((1, 3, d), mod_map),
                pl.BlockSpec((d, n_in), lambda bi, i: (0, 0)),
                fmap, fmap]
    args = [x, mod3, w_in, fa, fb]
    if rope_tabs is not None:
        in_specs += [pl.BlockSpec((tm, HEAD_W), lambda bi, i: (i, 0))] * 3
        args += list(rope_tabs)
    sds = jax.ShapeDtypeStruct
    out_shape = [sds((b, l, ATT_WIDTH), BF16)] * 3 + [sds((b, l, ATT_WIDTH), F32),
                 sds((b, 2, l, FOURIER_WIDTH), BF16), sds((b, l, FOURIER_WIDTH), F32),
                 sds((b, l, GMLP_WIDTH), F32), sds((b, l, GMLP_WIDTH), F32),
                 sds((b, l, GMLP_WIDTH), F32)]
    out_specs = [row(ATT_WIDTH)] * 4 + [
        pl.BlockSpec((1, 2, tm, FOURIER_WIDTH), lambda bi, i: (bi, 0, i, 0)),
        row(FOURIER_WIDTH), row(GMLP_WIDTH), row(GMLP_WIDTH), row(GMLP_WIDTH)]
    if emit_kv:
        kv = sds((b, N_ATT_HEADS, l, HEAD_W), F32)
        out_shape += [kv, kv]
        out_specs += [pl.BlockSpec((1, N_ATT_HEADS, tm, HEAD_W), lambda bi, i: (bi, 0, i, 0))] * 2
    return pl.pallas_call(
        functools.partial(_inproj_kernel, rope=rope_tabs is not None, emit_kv=emit_kv),
        grid=(b, l // tm),
        in_specs=in_specs, out_specs=out_specs, out_shape=out_shape,
        name=f"inproj_{l}",
        compiler_params=_cparams("parallel", "parallel"),
    )(*args)


def _attn_kernel(*refs, lam_init, has_cache, ck, n_new, past, tq, nq):
    if has_cache:
        (q_ref, k_ref, v_ref, pk_ref, pv_ref, g_ref, lq1_ref, lk1_ref, lq2_ref, lk2_ref,
         sw_ref, o_ref, p_ref, lv_ref, m_ref, pend_ref, bad_ref) = refs
    else:
        (q_ref, k_ref, v_ref, g_ref, lq1_ref, lk1_ref, lq2_ref, lk2_ref,
         sw_ref, o_ref, p_ref, lv_ref, m_ref, pend_ref, bad_ref) = refs

    chunks = []
    if has_cache:
        chunks.append((True, 0, 0, past))
    for j in range(n_new):
        chunks.append((False, j * ck, past + j * ck, ck))

    def keys(c):
        return pk_ref[0, 0, 0].astype(BF16) if c[0] else k_ref[0, c[1]:c[1] + c[3], :]

    def vals(c):
        return pv_ref[0, 0, 0].astype(BF16) if c[0] else v_ref[0, c[1]:c[1] + c[3], :]

    def scores(qq, c):
        return lax.dot_general(qq, keys(c), (((1,), (1,)), ((), ())), preferred_element_type=F32)

    def lane_tiles(x):
        return [x[:, t * 128:(t + 1) * 128] for t in range(x.shape[1] // 128)]

    lam = (jnp.exp(jnp.sum(lq1_ref[0] * lk1_ref[0], axis=-1, keepdims=True))
           - jnp.exp(jnp.sum(lq2_ref[0] * lk2_ref[0], axis=-1, keepdims=True)) + lam_init)

    def rows_of(t):
        return pl.ds(pl.multiple_of(t * tq, tq), tq)

    def stacked_q(t):
        q = q_ref[0, rows_of(t), :]
        lane = lax.broadcasted_iota(jnp.int32, q.shape, 1)
        zero = jnp.zeros_like(q)
        return jnp.concatenate([jnp.where(lane < HEAD_DIM, q, zero),
                                jnp.where(lane >= HEAD_DIM, q, zero)], axis=0)

    def scores_phase(t, slot):
        qq = stacked_q(t)
        s0 = scores(qq, chunks[0])
        stab = jnp.broadcast_to(jnp.max(s0, axis=-1, keepdims=True), (2 * tq, 128))
        lv = jnp.zeros((2 * tq, 128), F32)
        for ci, c in enumerate(chunks):
            s = s0 if ci == 0 else scores(qq, c)
            for ti, st in enumerate(lane_tiles(s)):
                p = jnp.exp2(st - stab)
                p_ref[slot, :, c[2] + ti * 128:c[2] + (ti + 1) * 128] = p
                lv = lv + p
        lv_ref[slot] = lv

    def values_phase(slot):
        lsum = jnp.sum(lv_ref[slot], axis=-1, keepdims=True)
        l1, l2 = lsum[:tq], lsum[tq:]
        rho = lam * l1 / l2
        acc = jnp.zeros((tq, HEAD_W), F32)
        for c in chunks:
            a = (p_ref[slot, 0:tq, c[2]:c[2] + c[3]]
                 - rho * p_ref[slot, tq:2 * tq, c[2]:c[2] + c[3]]).astype(BF16)
            acc = acc + jnp.dot(a, vals(c), preferred_element_type=F32)
        return acc / l1

    def finish(o, rows):
        o = (o * lax.rsqrt(jnp.mean(o * o, axis=-1, keepdims=True) + RMS_EPS)
             * sw_ref[0] * (1.0 - lam_init))
        return o * _silu(g_ref[0, rows, :])

    def emit(t, slot):
        rows = rows_of(t)
        out = finish(pend_ref[slot], rows)
        o_ref[0, rows, :] = out.astype(BF16)
        bad_ref[t] = jnp.sum(jnp.where(jnp.isfinite(out), 0.0, 1.0))

    scores_phase(0, 0)
    pend_ref[0] = values_phase(0)
    scores_phase(1, 1)

    def pair(i, carry):
        t = 2 * i + 2
        emit(t - 2, 0)
        pend_ref[1] = values_phase(1)
        scores_phase(t, 0)
        emit(t - 1, 1)
        pend_ref[0] = values_phase(0)
        scores_phase(t + 1, 1)
        return carry

    lax.fori_loop(0, nq // 2 - 1, pair, 0)
    emit(nq - 2, 0)
    pend_ref[1] = values_phase(1)
    emit(nq - 1, 1)

    def exact_tile(t, carry):
        @pl.when(bad_ref[t] > 0.0)
        def _():
            qq = stacked_q(t)
            for ci, c in enumerate(chunks):
                s = scores(qq, c)
                p_ref[0, :, c[2]:c[2] + c[3]] = s
                cm = functools.reduce(jnp.maximum, lane_tiles(s))
                m_ref[...] = cm if ci == 0 else jnp.maximum(m_ref[...], cm)
            for gi in range(2 * tq // SOFTMAX_ROWS):
                grp = slice(gi * SOFTMAX_ROWS, (gi + 1) * SOFTMAX_ROWS)
                m = jnp.max(m_ref[grp, :], axis=-1, keepdims=True)
                p = jnp.exp2(p_ref[0, grp, :] - m)
                p_ref[0, grp, :] = p
                lv_ref[0, grp, :] = functools.reduce(jnp.add, lane_tiles(p))
            rows = rows_of(t)
            o_ref[0, rows, :] = finish(values_phase(0), rows).astype(BF16)
        return carry

    lax.fori_loop(0, nq, exact_tile, 0)


def _attention(q, k, v, g_att, cache, layer, lam_params, subln_w, lam_init, tq, ck):
    b, l, _ = q.shape
    tq = _row_tile(l, tq)
    ck = _row_tile(l, ck)
    nq = l // tq
    assert nq >= 2 and nq % 2 == 0
    has_cache = cache is not None
    past = cache[0].shape[3] if has_cache else 0
    lk = past + l
    seq = pl.BlockSpec((1, l, HEAD_W), lambda bi, h: (bi, 0, h))
    vec = lambda w: pl.BlockSpec((1, 1, w), lambda bi, h: (layer, 0, 0))
    in_specs = [seq, seq, seq]
    args = [q, k, v]
    if has_cache:
        cspec = pl.BlockSpec((1, 1, 1, past, HEAD_W), lambda bi, h: (bi, layer, h, 0, 0))
        in_specs += [cspec, cspec]
        args += list(cache)
    in_specs += [seq] + [vec(HEAD_DIM)] * 4 + [vec(HEAD_W)]
    args += [g_att] + list(lam_params) + [subln_w]
    return pl.pallas_call(
        functools.partial(_attn_kernel, lam_init=lam_init, has_cache=has_cache, ck=ck,
                          n_new=l // ck, past=past, tq=tq, nq=nq),
        grid=(b, N_ATT_HEADS),
        in_specs=in_specs,
        out_specs=seq,
        out_shape=jax.ShapeDtypeStruct((b, l, ATT_WIDTH), BF16),
        scratch_shapes=[pltpu.VMEM((2, 2 * tq, lk), F32), pltpu.VMEM((2, 2 * tq, 128), F32),
                        pltpu.VMEM((2 * tq, 128), F32), pltpu.VMEM((2, tq, HEAD_W), F32),
                        pltpu.SMEM((nq,), F32)],
        name=f"attn_{l}",
        compiler_params=_cparams("parallel", "parallel"),
    )(*args)


def _fourier_kernel(g_ref, u_ref, gate_ref, o_ref, *, nb):
    g = g_ref[...]
    for j in range(nb):
        y = jnp.dot(g, u_ref[j], preferred_element_type=F32)
        o_ref[j] = (y * _silu(gate_ref[j])).astype(BF16)


def _fourier(gmat, uf, g_f, tr, nb):
    b, l, w = g_f.shape
    tr = _row_tile(l, tr)
    u2 = uf.reshape(b, 2 * l, w)
    return pl.pallas_call(
        functools.partial(_fourier_kernel, nb=nb),
        grid=(b // nb, l // tr),
        in_specs=[pl.BlockSpec((tr, 2 * l), lambda bi, i: (i, 0)),
                  pl.BlockSpec((nb, 2 * l, w), lambda bi, i: (bi, 0, 0)),
                  pl.BlockSpec((nb, tr, w), lambda bi, i: (bi, i, 0))],
        out_specs=pl.BlockSpec((nb, tr, w), lambda bi, i: (bi, i, 0)),
        out_shape=jax.ShapeDtypeStruct((b, l, w), BF16),
        name=f"fourier_{l}",
        compiler_params=_cparams("parallel", "arbitrary"),
    )(gmat, u2, g_f)


def _gmlp_kernel(vm_ref, u_ref, gm_ref, w_ref, bias_ref, avg_ref, o_ref, *, n_chunks):
    avg = avg_ref[...]

    def group_mean(x):
        hi = x.astype(BF16)
        lo = (x - hi.astype(F32)).astype(BF16)
        return (jnp.dot(hi, avg, preferred_element_type=F32)
                + jnp.dot(lo, avg, preferred_element_type=F32))

    lane = lax.broadcasted_iota(jnp.int32, (CHUNK, GMLP_WIDTH), 1)
    head_mask = [(lane >= hh * GMLP_HEAD_DIM) & (lane < (hh + 1) * GMLP_HEAD_DIM)
                 for hh in range(N_GMLP_HEADS)]
    for c in range(n_chunks):
        rows = slice(c * CHUNK, (c + 1) * CHUNK)
        x = vm_ref[0, rows, :]
        xc = x - group_mean(x)
        vn = (xc * lax.rsqrt(group_mean(xc * xc) + LN_EPS)).astype(BF16)
        stacked = jnp.concatenate(
            [jnp.where(head_mask[hh], vn, jnp.zeros_like(vn)) for hh in range(N_GMLP_HEADS)], axis=0)
        s = jnp.dot(w_ref[0], stacked, preferred_element_type=F32) + bias_ref[0]
        o_ref[0, rows, :] = (u_ref[0, rows, :] * s * _silu(gm_ref[0, rows, :])).astype(BF16)


def _gmlp(vm, u, g_m, w_cat, bias_full, avg, layer, tm):
    b, l, w = vm.shape
    tm = _row_tile(l, tm)
    row = pl.BlockSpec((1, tm, w), lambda bi, i: (bi, i, 0))
    return pl.pallas_call(
        functools.partial(_gmlp_kernel, n_chunks=tm // CHUNK),
        grid=(b, l // tm),
        in_specs=[row, row, row,
                  pl.BlockSpec((1, CHUNK, N_GMLP_HEADS * CHUNK), lambda bi, i: (layer, 0, 0)),
                  pl.BlockSpec((1, CHUNK, w), lambda bi, i: (layer, 0, 0)),
                  pl.BlockSpec((w, w), lambda bi, i: (0, 0))],
        out_specs=row,
        out_shape=jax.ShapeDtypeStruct((b, l, w), BF16),
        name=f"gmlp_{l}",
        compiler_params=_cparams("parallel", "parallel"),
    )(vm, u, g_m, w_cat, bias_full, avg)


def _outproj_kernel(att_ref, fo_ref, gm_ref, x_ref, mod_ref, w_ref, lg_ref, lb_ref, o_ref, *, alpha):
    a0, a1 = ATT_WIDTH, ATT_WIDTH + FOURIER_WIDTH
    y = (jnp.dot(att_ref[0], w_ref[0:a0, :], preferred_element_type=F32)
         + jnp.dot(fo_ref[0], w_ref[a0:a1, :], preferred_element_type=F32)
         + jnp.dot(gm_ref[0], w_ref[a1:, :], preferred_element_type=F32))
    r = alpha * x_ref[0] + mod_ref[0, 2:3, :] * y
    mu = jnp.mean(r, axis=-1, keepdims=True)
    rc = r - mu
    var = jnp.mean(rc * rc, axis=-1, keepdims=True)
    o_ref[0] = rc * lax.rsqrt(var + LN_EPS) * lg_ref[0] + lb_ref[0]


def _out_projection(att, fo, gm, x, mod3, per_batch_mod, w_out, ln_g, ln_b, layer, alpha, tm):
    b, l, d = x.shape
    tm = _row_tile(l, tm)
    row = lambda w: pl.BlockSpec((1, tm, w), lambda bi, i: (bi, i, 0))
    mod_map = (lambda bi, i: (bi, 0, 0)) if per_batch_mod else (lambda bi, i: (0, 0, 0))
    vec = pl.BlockSpec((1, 1, d), lambda bi, i: (layer, 0, 0))
    return pl.pallas_call(
        functools.partial(_outproj_kernel, alpha=alpha),
        grid=(b, l // tm),
        in_specs=[row(ATT_WIDTH), row(FOURIER_WIDTH), row(GMLP_WIDTH), row(d),
                  pl.BlockSpec((1, 3, d), mod_map),
                  pl.BlockSpec(w_out.shape, lambda bi, i: (0, 0)),
                  vec, vec],
        out_specs=row(d),
        out_shape=jax.ShapeDtypeStruct((b, l, d), F32),
        name=f"outproj_{l}",
        compiler_params=_cparams("parallel", "parallel"),
    )(att, fo, gm, x, mod3, w_out, ln_g, ln_b)


def _rope_tables(l):
    n_freq = HEAD_DIM // 4
    rows = l // GRID_W
    row = np.repeat(np.arange(rows, dtype=np.float64), GRID_W)
    col = np.tile(np.arange(GRID_W, dtype=np.float64), rows)
    inv = ROPE_THETA ** (-np.arange(n_freq, dtype=np.float64) / n_freq)
    ang_r, ang_c = row[:, None] * inv, col[:, None] * inv
    zeros = np.zeros_like(ang_r)

    def per_head(r_lo, r_hi, c_lo, c_hi):
        one = np.concatenate([r_lo, r_hi, c_lo, c_hi], axis=-1)
        return jnp.asarray(np.concatenate([one, one], axis=-1), F32)

    cr, sr, cc, sc = np.cos(ang_r), np.sin(ang_r), np.cos(ang_c), np.sin(ang_c)
    cos_t = per_head(cr, cr, cc, cc)
    sa_t = per_head(-sr, zeros, -sc, zeros)
    sb_t = per_head(zeros, sr, zeros, sc)
    return cos_t, sa_t, sb_t


def _run_layer(x, mod3, per_batch_mod, layer, lam_init, alpha, p, gmat, rope_tabs, cache, emit_kv,
               tiles):
    outs = _in_projection(x, mod3, per_batch_mod, p["w_in"][layer], p["fa"], p["fb"], layer,
                          rope_tabs, emit_kv, tiles["tm_in"])
    q, k, v, g_att, uf, g_f, u, vm, g_m = outs[:9]
    att = _attention(q, k, v, g_att, cache, layer, p["lam"], p["subln_w"], lam_init,
                     tiles["tq"], tiles["ck"])
    fo = _fourier(gmat, uf, g_f, tiles["tr"], tiles["nb"])
    gm = _gmlp(vm, u, g_m, p["gmlp_w"], p["gmlp_bias"], p["avg"], layer, tiles["tm_gmlp"])
    x_new = _out_projection(att, fo, gm, x, mod3, per_batch_mod, p["w_out"][layer],
                            p["ln_g"], p["ln_b"], layer, alpha, tiles["tm_out"])
    return x_new, outs[9:]


def kernel(x_prompt, x_sample, c, cache_k, cache_v, c_ctx, w_ada, b_ada, w_in, w_out,
           lam_q1, lam_k1, lam_q2, lam_k2, subln_w, fourier_w, gmlp_ws, gmlp_bs, ln_g, ln_b):
    depth, d, _ = w_ada.shape
    nb_lat = c.shape[0]
    l_ctx, l_lat = x_prompt.shape[1], x_sample.shape[1]
    assert 1 + nb_lat <= MOD_ROWS
    alpha = (2.0 * depth) ** 0.25

    cvec = jnp.concatenate([c_ctx[None, :], c, jnp.zeros((MOD_ROWS - 1 - nb_lat, d), F32)], axis=0)
    mod = _modulation(cvec, w_ada, b_ada)

    fa, fb = _fourier_channel_maps(fourier_w, 1.0 / math.sqrt(FOURIER_GROUP_DIM))
    params = {
        "w_in": w_in.astype(BF16), "w_out": w_out.astype(BF16), "fa": fa, "fb": fb,
        "lam": tuple(t[:, None, :] for t in (lam_q1, lam_k1, lam_q2, lam_k2)),
        "subln_w": subln_w[:, None, :], "ln_g": ln_g[:, None, :], "ln_b": ln_b[:, None, :],
        "gmlp_w": gmlp_ws.astype(BF16).transpose(0, 2, 1, 3).reshape(depth, CHUNK, N_GMLP_HEADS * CHUNK),
        "gmlp_bias": jnp.repeat(gmlp_bs.astype(F32).transpose(0, 2, 1), GMLP_HEAD_DIM, axis=2),
        "avg": jnp.asarray(np.kron(np.eye(N_GMLP_HEADS), np.full((GMLP_HEAD_DIM,) * 2, 1.0 / GMLP_HEAD_DIM)), BF16),
    }
    ctx_tiles = dict(tm_in=256, tq=128, ck=256, tr=256, nb=2, tm_gmlp=256, tm_out=256)
    lat_tiles = dict(tm_in=512, tq=256, ck=512, tr=512, nb=2, tm_gmlp=512, tm_out=512)

    g_ctx = _dft_matrix(l_ctx)
    xp = x_prompt
    ks_new, vs_new = [], []
    for layer in range(depth):
        lam_init = 0.8 - 0.6 * math.exp(-0.3 * layer)
        mod3 = mod[layer, 0:1].reshape(1, 3, d)
        xp, (k_l, v_l) = _run_layer(xp, mod3, False, layer, lam_init, alpha, params, g_ctx,
                                    None, None, True, ctx_tiles)
        ks_new.append(k_l)
        vs_new.append(v_l)
    new_k = jnp.stack(ks_new, axis=1)
    new_v = jnp.stack(vs_new, axis=1)

    g_lat = _dft_matrix(l_lat)
    rope_tabs = _rope_tables(l_lat)
    xs = x_sample
    for layer in range(depth):
        lam_init = 0.8 - 0.6 * math.exp(-0.3 * layer)
        mod3 = mod[layer, 1:1 + nb_lat].reshape(nb_lat, 3, d)
        xs, _ = _run_layer(xs, mod3, True, layer, lam_init, alpha, params, g_lat,
                           rope_tabs, (cache_k, cache_v), False, lat_tiles)
    return (xp, xs, new_k, new_v)
```

```python
import functools
import math

import numpy as np
import jax
import jax.numpy as jnp
from jax import lax
from jax.experimental import pallas as pl
from jax.experimental.pallas import tpu as pltpu

F32 = jnp.float32
BF16 = jnp.bfloat16

GRID_W = 64
HEAD_DIM = 64
N_ATT_HEADS = 4
HEAD_W = 2 * HEAD_DIM
ATT_WIDTH = N_ATT_HEADS * HEAD_W
N_FOURIER_GROUPS = 4
FOURIER_GROUP_DIM = 64
FOURIER_WIDTH = N_FOURIER_GROUPS * FOURIER_GROUP_DIM
N_GMLP_HEADS = 4
GMLP_HEAD_DIM = 64
GMLP_WIDTH = N_GMLP_HEADS * GMLP_HEAD_DIM
CHUNK = 128
ROPE_THETA = 10000.0
LN_EPS = 1e-5
RMS_EPS = 1e-5
QK_SCALE = HEAD_DIM ** -0.5
LOG2_E = math.log2(math.e)

_SEG_NAMES = ("q", "k", "v", "g_att", "f", "g_f", "u", "vm", "g_m")
_SEG_WIDTHS = (ATT_WIDTH,) * 4 + (FOURIER_WIDTH,) * 2 + (GMLP_WIDTH,) * 3
_SEG_START = dict(zip(_SEG_NAMES, np.concatenate([[0], np.cumsum(_SEG_WIDTHS)[:-1]]).tolist()))
_SEG_WIDTH = dict(zip(_SEG_NAMES, _SEG_WIDTHS))

VMEM_LIMIT_BYTES = 52 * 1024 * 1024
MOD_ROWS = 16


def _cparams(*sem):
    return pltpu.CompilerParams(dimension_semantics=sem, vmem_limit_bytes=VMEM_LIMIT_BYTES)


def _silu(x):
    return x * jax.nn.sigmoid(x)


def _row_tile(l, want):
    t = min(l, want)
    assert l % t == 0
    return t


def _mod_kernel(c_ref, w_ref, b_ref, o_ref):
    s = _silu(c_ref[...])
    o_ref[0] = jnp.dot(s, w_ref[0], precision=lax.Precision.HIGHEST,
                       preferred_element_type=F32) + b_ref[0]


def _modulation(cvec, w_ada, b_ada):
    depth, d, d3 = w_ada.shape
    tn = 1024
    return pl.pallas_call(
        _mod_kernel,
        grid=(depth, d3 // tn),
        in_specs=[pl.BlockSpec((MOD_ROWS, d), lambda l, j: (0, 0)),
                  pl.BlockSpec((1, d, tn), lambda l, j: (l, 0, j)),
                  pl.BlockSpec((1, 1, tn), lambda l, j: (l, 0, j))],
        out_specs=pl.BlockSpec((1, MOD_ROWS, tn), lambda l, j: (l, 0, j)),
        out_shape=jax.ShapeDtypeStruct((depth, MOD_ROWS, d3), F32),
        name="adaln_mod",
        compiler_params=_cparams("arbitrary", "arbitrary"),
    )(cvec, w_ada, b_ada.reshape(depth, 1, d3))


def _chanmap_kernel(cc_ref, ss_ref, w_ref, a_ref, b_ref, *, scale):
    w = w_ref[0]
    a_ref[0] = (jnp.dot(cc_ref[...], w, precision=lax.Precision.HIGHEST,
                        preferred_element_type=F32) * scale).astype(BF16)
    b_ref[0] = (jnp.dot(ss_ref[...], w, precision=lax.Precision.HIGHEST,
                        preferred_element_type=F32) * scale).astype(BF16)


def _fourier_channel_maps(fourier_w, scale):
    depth = fourier_w.shape[0]
    n = np.arange(FOURIER_GROUP_DIM)
    ang = 2.0 * np.pi * ((n[:, None] * n[None, :]) % FOURIER_GROUP_DIM) / FOURIER_GROUP_DIM
    eye = np.eye(N_FOURIER_GROUPS)
    cc = jnp.asarray(np.kron(eye, np.cos(ang)), F32)
    ss = jnp.asarray(np.kron(eye, np.sin(ang)), F32)
    wbd = jnp.einsum("gh,lgce->lgche", jnp.eye(N_FOURIER_GROUPS, dtype=F32),
                     fourier_w.astype(F32)).reshape(depth, FOURIER_WIDTH, FOURIER_WIDTH)
    w_spec = pl.BlockSpec((1, FOURIER_WIDTH, FOURIER_WIDTH), lambda l: (l, 0, 0))
    c_spec = pl.BlockSpec((FOURIER_WIDTH, FOURIER_WIDTH), lambda l: (0, 0))
    shp = jax.ShapeDtypeStruct((depth, FOURIER_WIDTH, FOURIER_WIDTH), BF16)
    return pl.pallas_call(
        functools.partial(_chanmap_kernel, scale=scale),
        grid=(depth,),
        in_specs=[c_spec, c_spec, w_spec],
        out_specs=[w_spec, w_spec],
        out_shape=[shp, shp],
        name="fourier_chanmap",
        compiler_params=_cparams("arbitrary"),
    )(cc, ss, wbd)


def _dft_kernel(ca_ref, sa_ref, cb_ref, sb_ref, g_ref, *, l):
    ca, sa = ca_ref[0], sa_ref[0]
    cb, sb = cb_ref[...], sb_ref[...]
    norm = 1.0 / math.sqrt(l)
    g_ref[:, 0:l] = ((ca * cb - sa * sb) * norm).astype(BF16)
    g_ref[:, l:2 * l] = ((sa * cb + ca * sb) * -norm).astype(BF16)


def _dft_matrix(l):
    na = l // GRID_W
    k = np.arange(l, dtype=np.int64)
    ang_a = ((np.arange(na)[:, None] * k[None, :]) % na) * (2.0 * np.pi / na)
    ang_b = ((np.arange(GRID_W)[:, None] * k[None, :]) % l) * (2.0 * np.pi / l)
    ca = jnp.asarray(np.cos(ang_a).reshape(na, 1, l), F32)
    sa = jnp.asarray(np.sin(ang_a).reshape(na, 1, l), F32)
    cb, sb = jnp.asarray(np.cos(ang_b), F32), jnp.asarray(np.sin(ang_b), F32)
    a_spec = pl.BlockSpec((1, 1, l), lambda i: (i, 0, 0))
    b_spec = pl.BlockSpec((GRID_W, l), lambda i: (0, 0))
    return pl.pallas_call(
        functools.partial(_dft_kernel, l=l),
        grid=(na,),
        in_specs=[a_spec, a_spec, b_spec, b_spec],
        out_specs=pl.BlockSpec((GRID_W, 2 * l), lambda i: (i, 0)),
        out_shape=jax.ShapeDtypeStruct((l, 2 * l), BF16),
        name=f"dft_matrix_{l}",
        compiler_params=_cparams("arbitrary"),
    )(ca, sa, cb, sb)


def _inproj_kernel(*refs, rope, emit_kv):
    x_ref, mod_ref, w_ref, fa_ref, fb_ref = refs[:5]
    refs = refs[5:]
    if rope:
        cos_ref, sa_ref, sb_ref = refs[:3]
        refs = refs[3:]
    q_ref, k_ref, vt_ref, ga_ref, uf_ref, gf_ref, u_ref, vm_ref, gm_ref = refs[:9]
    refs = refs[9:]

    x = x_ref[0]
    mu = jnp.mean(x, axis=-1, keepdims=True)
    xc = x - mu
    var = jnp.mean(xc * xc, axis=-1, keepdims=True)
    shift, scale = mod_ref[0, 0:1, :], mod_ref[0, 1:2, :]
    h = (xc * lax.rsqrt(var + LN_EPS) * (1.0 + scale) + shift).astype(BF16)

    def proj(name):
        c0 = _SEG_START[name]
        return jnp.dot(h, w_ref[:, c0:c0 + _SEG_WIDTH[name]], preferred_element_type=F32)

    def rotate(z):
        return (z * cos_ref[...] + pltpu.roll(z, HEAD_W - 16, 1) * sa_ref[...]
                + pltpu.roll(z, 16, 1) * sb_ref[...])

    zq, zk, zv = proj("q"), proj("k"), proj("v")
    for hh in range(N_ATT_HEADS):
        sl = slice(hh * HEAD_W, (hh + 1) * HEAD_W)
        qh, kh = zq[:, sl], zk[:, sl]
        if emit_kv:
            kf_ref, vf_ref = refs
            kf_ref[0, hh] = kh
            vf_ref[0, hh] = zv[:, sl]
        if rope:
            qh, kh = rotate(qh), rotate(kh)
        q_ref[0, :, sl] = (qh * (QK_SCALE * LOG2_E)).astype(BF16)
        k_ref[0, :, sl] = kh.astype(BF16)
    vt_ref[0] = zv.T.astype(BF16)
    ga_ref[0] = proj("g_att")
    f = proj("f").astype(BF16)
    uf_ref[0, 0] = jnp.dot(f, fa_ref[0], preferred_element_type=F32).astype(BF16)
    uf_ref[0, 1] = jnp.dot(f, fb_ref[0], preferred_element_type=F32).astype(BF16)
    gf_ref[0] = proj("g_f")
    u_ref[0] = proj("u")
    vm_ref[0] = proj("vm")
    gm_ref[0] = proj("g_m")


def _in_projection(x, mod3, per_batch_mod, w_in, fa, fb, layer, rope_tabs, emit_kv, tm):
    b, l, d = x.shape
    n_in = w_in.shape[1]
    tm = _row_tile(l, tm)
    row = lambda w: pl.BlockSpec((1, tm, w), lambda bi, i: (bi, i, 0))
    mod_map = (lambda bi, i: (bi, 0, 0)) if per_batch_mod else (lambda bi, i: (0, 0, 0))
    fmap = pl.BlockSpec((1, FOURIER_WIDTH, FOURIER_WIDTH), lambda bi, i: (layer, 0, 0))
    in_specs = [row(d),
                pl.BlockSpec((1, 3, d), mod_map),
                pl.BlockSpec((d, n_in), lambda bi, i: (0, 0)),
                fmap, fmap]
    args = [x, mod3, w_in, fa, fb]
    if rope_tabs is not None:
        in_specs += [pl.BlockSpec((tm, HEAD_W), lambda bi, i: (i, 0))] * 3
        args += list(rope_tabs)
    sds = jax.ShapeDtypeStruct
    out_shape = [sds((b, l, ATT_WIDTH), BF16)] * 2 + [sds((b, ATT_WIDTH, l), BF16),
                 sds((b, l, ATT_WIDTH), F32),
                 sds((b, 2, l, FOURIER_WIDTH), BF16), sds((b, l, FOURIER_WIDTH), F32),
                 sds((b, l, GMLP_WIDTH), F32), sds((b, l, GMLP_WIDTH), F32),
                 sds((b, l, GMLP_WIDTH), F32)]
    out_specs = [row(ATT_WIDTH)] * 2 + [
        pl.BlockSpec((1, ATT_WIDTH, tm), lambda bi, i: (bi, 0, i)), row(ATT_WIDTH)] + [
        pl.BlockSpec((1, 2, tm, FOURIER_WIDTH), lambda bi, i: (bi, 0, i, 0)),
        row(FOURIER_WIDTH), row(GMLP_WIDTH), row(GMLP_WIDTH), row(GMLP_WIDTH)]
    if emit_kv:
        kv = sds((b, N_ATT_HEADS, l, HEAD_W), F32)
        out_shape += [kv, kv]
        out_specs += [pl.BlockSpec((1, N_ATT_HEADS, tm, HEAD_W), lambda bi, i: (bi, 0, i, 0))] * 2
    return pl.pallas_call(
        functools.partial(_inproj_kernel, rope=rope_tabs is not None, emit_kv=emit_kv),
        grid=(b, l // tm),
        in_specs=in_specs, out_specs=out_specs, out_shape=out_shape,
        name=f"inproj_{l}",
        compiler_params=_cparams("parallel", "parallel"),
    )(*args)


def _attn_kernel(*refs, lam_init, has_cache, ck, n_new, past, tq, nq):
    if has_cache:
        (q_ref, k_ref, vt_ref, pk_ref, pv_ref, g_ref, lq1_ref, lk1_ref, lq2_ref, lk2_ref,
         sw_ref, o_ref, p_ref, lv_ref, pend_ref, bad_ref, pvt_ref) = refs
    else:
        (q_ref, k_ref, vt_ref, g_ref, lq1_ref, lk1_ref, lq2_ref, lk2_ref,
         sw_ref, o_ref, p_ref, lv_ref, pend_ref, bad_ref) = refs

    chunks = []
    if has_cache:
        chunks.append((True, 0, 0, past))
    for j in range(n_new):
        chunks.append((False, j * ck, past + j * ck, ck))

    if has_cache:
        pvt_ref[...] = pv_ref[0, 0, 0].T.astype(BF16)

    def keys(c):
        return pk_ref[0, 0, 0].astype(BF16) if c[0] else k_ref[0, c[1]:c[1] + c[3], :]

    def vals_t(c):
        return pvt_ref[...] if c[0] else vt_ref[0, :, c[1]:c[1] + c[3]]

    def scores_t(qq, c):
        return lax.dot_general(keys(c), qq, (((1,), (1,)), ((), ())), preferred_element_type=F32)

    lam = (jnp.exp(jnp.sum(lq1_ref[0] * lk1_ref[0], axis=-1, keepdims=True))
           - jnp.exp(jnp.sum(lq2_ref[0] * lk2_ref[0], axis=-1, keepdims=True)) + lam_init)

    def rows_of(t):
        return pl.ds(pl.multiple_of(t * tq, tq), tq)

    def stacked_q(t):
        q = q_ref[0, rows_of(t), :]
        lane = lax.broadcasted_iota(jnp.int32, q.shape, 1)
        zero = jnp.zeros_like(q)
        return jnp.concatenate([jnp.where(lane < HEAD_DIM, q, zero),
                                jnp.where(lane >= HEAD_DIM, q, zero)], axis=0)

    def scores_phase(t, slot):
        qq = stacked_q(t)
        s0 = scores_t(qq, chunks[0])
        stab = jnp.max(s0, axis=0, keepdims=True)
        lv = jnp.zeros((1, 2 * tq), F32)
        for ci, c in enumerate(chunks):
            s = s0 if ci == 0 else scores_t(qq, c)
            p = jnp.exp2(s - stab)
            p_ref[slot, c[2]:c[2] + c[3], :] = p
            lv = lv + jnp.sum(p, axis=0, keepdims=True)
        lv_ref[slot] = jnp.broadcast_to(lv, (8, 2 * tq))

    def values_phase(slot):
        lsum = lv_ref[slot][0:1, :]
        l1, l2 = lsum[:, :tq], lsum[:, tq:]
        rho = lam * l1 / l2
        acc = jnp.zeros((HEAD_W, tq), F32)
        for c in chunks:
            a = (p_ref[slot, c[2]:c[2] + c[3], 0:tq]
                 - rho * p_ref[slot, c[2]:c[2] + c[3], tq:2 * tq]).astype(BF16)
            acc = acc + jnp.dot(vals_t(c), a, preferred_element_type=F32)
        return (acc / l1).T

    def finish(o, rows):
        o = (o * lax.rsqrt(jnp.mean(o * o, axis=-1, keepdims=True) + RMS_EPS)
             * sw_ref[0] * (1.0 - lam_init))
        return o * _silu(g_ref[0, rows, :])

    def emit(t, slot):
        rows = rows_of(t)
        out = finish(pend_ref[slot], rows)
        o_ref[0, rows, :] = out.astype(BF16)
        bad_ref[t] = jnp.sum(jnp.where(jnp.isfinite(out), 0.0, 1.0))

    scores_phase(0, 0)
    pend_ref[0] = values_phase(0)
    scores_phase(1, 1)

    def pair(i, carry):
        t = 2 * i + 2
        emit(t - 2, 0)
        pend_ref[1] = values_phase(1)
        scores_phase(t, 0)
        emit(t - 1, 1)
        pend_ref[0] = values_phase(0)
        scores_phase(t + 1, 1)
        return carry

    lax.fori_loop(0, nq // 2 - 1, pair, 0)
    emit(nq - 2, 0)
    pend_ref[1] = values_phase(1)
    emit(nq - 1, 1)

    def exact_tile(t, carry):
        @pl.when(bad_ref[t] > 0.0)
        def _():
            qq = stacked_q(t)
            m = None
            for c in chunks:
                s = scores_t(qq, c)
                p_ref[0, c[2]:c[2] + c[3], :] = s
                cm = jnp.max(s, axis=0, keepdims=True)
                m = cm if m is None else jnp.maximum(m, cm)
            lv = jnp.zeros((1, 2 * tq), F32)
            for c in chunks:
                p = jnp.exp2(p_ref[0, c[2]:c[2] + c[3], :] - m)
                p_ref[0, c[2]:c[2] + c[3], :] = p
                lv = lv + jnp.sum(p, axis=0, keepdims=True)
            lv_ref[0] = jnp.broadcast_to(lv, (8, 2 * tq))
            rows = rows_of(t)
            o_ref[0, rows, :] = finish(values_phase(0), rows).astype(BF16)
        return carry

    lax.fori_loop(0, nq, exact_tile, 0)


def _attention(q, k, v, g_att, cache, layer, lam_params, subln_w, lam_init, tq, ck):
    b, l, _ = q.shape
    tq = _row_tile(l, tq)
    ck = _row_tile(l, ck)
    nq = l // tq
    assert nq >= 2 and nq % 2 == 0
    has_cache = cache is not None
    past = cache[0].shape[3] if has_cache else 0
    lk = past + l
    seq = pl.BlockSpec((1, l, HEAD_W), lambda bi, h: (bi, 0, h))
    vec = lambda w: pl.BlockSpec((1, 1, w), lambda bi, h: (layer, 0, 0))
    vt_spec = pl.BlockSpec((1, HEAD_W, l), lambda bi, h: (bi, h, 0))
    in_specs = [seq, seq, vt_spec]
    args = [q, k, v]
    if has_cache:
        cspec = pl.BlockSpec((1, 1, 1, past, HEAD_W), lambda bi, h: (bi, layer, h, 0, 0))
        in_specs += [cspec, cspec]
        args += list(cache)
    in_specs += [seq] + [vec(HEAD_DIM)] * 4 + [vec(HEAD_W)]
    args += [g_att] + list(lam_params) + [subln_w]
    return pl.pallas_call(
        functools.partial(_attn_kernel, lam_init=lam_init, has_cache=has_cache, ck=ck,
                          n_new=l // ck, past=past, tq=tq, nq=nq),
        grid=(b, N_ATT_HEADS),
        in_specs=in_specs,
        out_specs=seq,
        out_shape=jax.ShapeDtypeStruct((b, l, ATT_WIDTH), BF16),
        scratch_shapes=[pltpu.VMEM((2, lk, 2 * tq), F32), pltpu.VMEM((2, 8, 2 * tq), F32),
                        pltpu.VMEM((2, tq, HEAD_W), F32), pltpu.SMEM((nq,), F32)]
        + ([pltpu.VMEM((HEAD_W, past), BF16)] if has_cache else []),
        name=f"attn_{l}",
        compiler_params=_cparams("parallel", "parallel"),
    )(*args)


def _fourier_kernel(g_ref, u_ref, gate_ref, o_ref, *, nb):
    g = g_ref[...]
    for j in range(nb):
        y = jnp.dot(g, u_ref[j], preferred_element_type=F32)
        o_ref[j] = (y * _silu(gate_ref[j])).astype(BF16)


def _fourier(gmat, uf, g_f, tr, nb):
    b, l, w = g_f.shape
    tr = _row_tile(l, tr)
    u2 = uf.reshape(b, 2 * l, w)
    return pl.pallas_call(
        functools.partial(_fourier_kernel, nb=nb),
        grid=(b // nb, l // tr),
        in_specs=[pl.BlockSpec((tr, 2 * l), lambda bi, i: (i, 0)),
                  pl.BlockSpec((nb, 2 * l, w), lambda bi, i: (bi, 0, 0)),
                  pl.BlockSpec((nb, tr, w), lambda bi, i: (bi, i, 0))],
        out_specs=pl.BlockSpec((nb, tr, w), lambda bi, i: (bi, i, 0)),
        out_shape=jax.ShapeDtypeStruct((b, l, w), BF16),
        name=f"fourier_{l}",
        compiler_params=_cparams("parallel", "arbitrary"),
    )(gmat, u2, g_f)


def _gmlp_kernel(vm_ref, u_ref, gm_ref, w_ref, bias_ref, avg_ref, o_ref, *, n_chunks):
    avg = avg_ref[...]

    def group_mean(x):
        hi = x.astype(BF16)
        lo = (x - hi.astype(F32)).astype(BF16)
        return (jnp.dot(hi, avg, preferred_element_type=F32)
                + jnp.dot(lo, avg, preferred_element_type=F32))

    lane = lax.broadcasted_iota(jnp.int32, (CHUNK, GMLP_WIDTH), 1)
    head_mask = [(lane >= hh * GMLP_HEAD_DIM) & (lane < (hh + 1) * GMLP_HEAD_DIM)
                 for hh in range(N_GMLP_HEADS)]
    for c in range(n_chunks):
        rows = slice(c * CHUNK, (c + 1) * CHUNK)
        x = vm_ref[0, rows, :]
        xc = x - group_mean(x)
        vn = (xc * lax.rsqrt(group_mean(xc * xc) + LN_EPS)).astype(BF16)
        stacked = jnp.concatenate(
            [jnp.where(head_mask[hh], vn, jnp.zeros_like(vn)) for hh in range(N_GMLP_HEADS)], axis=0)
        s = jnp.dot(w_ref[0], stacked, preferred_element_type=F32) + bias_ref[0]
        o_ref[0, rows, :] = (u_ref[0, rows, :] * s * _silu(gm_ref[0, rows, :])).astype(BF16)


def _gmlp(vm, u, g_m, w_cat, bias_full, avg, layer, tm):
    b, l, w = vm.shape
    tm = _row_tile(l, tm)
    row = pl.BlockSpec((1, tm, w), lambda bi, i: (bi, i, 0))
    return pl.pallas_call(
        functools.partial(_gmlp_kernel, n_chunks=tm // CHUNK),
        grid=(b, l // tm),
        in_specs=[row, row, row,
                  pl.BlockSpec((1, CHUNK, N_GMLP_HEADS * CHUNK), lambda bi, i: (layer, 0, 0)),
                  pl.BlockSpec((1, CHUNK, w), lambda bi, i: (layer, 0, 0)),
                  pl.BlockSpec((w, w), lambda bi, i: (0, 0))],
        out_specs=row,
        out_shape=jax.ShapeDtypeStruct((b, l, w), BF16),
        name=f"gmlp_{l}",
        compiler_params=_cparams("parallel", "parallel"),
    )(vm, u, g_m, w_cat, bias_full, avg)


def _outproj_kernel(att_ref, fo_ref, gm_ref, x_ref, mod_ref, w_ref, lg_ref, lb_ref, o_ref, *, alpha):
    a0, a1 = ATT_WIDTH, ATT_WIDTH + FOURIER_WIDTH
    y = (jnp.dot(att_ref[0], w_ref[0:a0, :], preferred_element_type=F32)
         + jnp.dot(fo_ref[0], w_ref[a0:a1, :], preferred_element_type=F32)
         + jnp.dot(gm_ref[0], w_ref[a1:, :], preferred_element_type=F32))
    r = alpha * x_ref[0] + mod_ref[0, 2:3, :] * y
    mu = jnp.mean(r, axis=-1, keepdims=True)
    rc = r - mu
    var = jnp.mean(rc * rc, axis=-1, keepdims=True)
    o_ref[0] = rc * lax.rsqrt(var + LN_EPS) * lg_ref[0] + lb_ref[0]


def _out_projection(att, fo, gm, x, mod3, per_batch_mod, w_out, ln_g, ln_b, layer, alpha, tm):
    b, l, d = x.shape
    tm = _row_tile(l, tm)
    row = lambda w: pl.BlockSpec((1, tm, w), lambda bi, i: (bi, i, 0))
    mod_map = (lambda bi, i: (bi, 0, 0)) if per_batch_mod else (lambda bi, i: (0, 0, 0))
    vec = pl.BlockSpec((1, 1, d), lambda bi, i: (layer, 0, 0))
    return pl.pallas_call(
        functools.partial(_outproj_kernel, alpha=alpha),
        grid=(b, l // tm),
        in_specs=[row(ATT_WIDTH), row(FOURIER_WIDTH), row(GMLP_WIDTH), row(d),
                  pl.BlockSpec((1, 3, d), mod_map),
                  pl.BlockSpec(w_out.shape, lambda bi, i: (0, 0)),
                  vec, vec],
        out_specs=row(d),
        out_shape=jax.ShapeDtypeStruct((b, l, d), F32),
        name=f"outproj_{l}",
        compiler_params=_cparams("parallel", "parallel"),
    )(att, fo, gm, x, mod3, w_out, ln_g, ln_b)


def _rope_tables(l):
    n_freq = HEAD_DIM // 4
    rows = l // GRID_W
    row = np.repeat(np.arange(rows, dtype=np.float64), GRID_W)
    col = np.tile(np.arange(GRID_W, dtype=np.float64), rows)
    inv = ROPE_THETA ** (-np.arange(n_freq, dtype=np.float64) / n_freq)
    ang_r, ang_c = row[:, None] * inv, col[:, None] * inv
    zeros = np.zeros_like(ang_r)

    def per_head(r_lo, r_hi, c_lo, c_hi):
        one = np.concatenate([r_lo, r_hi, c_lo, c_hi], axis=-1)
        return jnp.asarray(np.concatenate([one, one], axis=-1), F32)

    cr, sr, cc, sc = np.cos(ang_r), np.sin(ang_r), np.cos(ang_c), np.sin(ang_c)
    cos_t = per_head(cr, cr, cc, cc)
    sa_t = per_head(-sr, zeros, -sc, zeros)
    sb_t = per_head(zeros, sr, zeros, sc)
    return cos_t, sa_t, sb_t


def _run_layer(x, mod3, per_batch_mod, layer, lam_init, alpha, p, gmat, rope_tabs, cache, emit_kv,
               tiles):
    outs = _in_projection(x, mod3, per_batch_mod, p["w_in"][layer], p["fa"], p["fb"], layer,
                          rope_tabs, emit_kv, tiles["tm_in"])
    q, k, v, g_att, uf, g_f, u, vm, g_m = outs[:9]
    att = _attention(q, k, v, g_att, cache, layer, p["lam"], p["subln_w"], lam_init,
                     tiles["tq"], tiles["ck"])
    fo = _fourier(gmat, uf, g_f, tiles["tr"], tiles["nb"])
    gm = _gmlp(vm, u, g_m, p["gmlp_w"], p["gmlp_bias"], p["avg"], layer, tiles["tm_gmlp"])
    x_new = _out_projection(att, fo, gm, x, mod3, per_batch_mod, p["w_out"][layer],
                            p["ln_g"], p["ln_b"], layer, alpha, tiles["tm_out"])
    return x_new, outs[9:]


def kernel(x_prompt, x_sample, c, cache_k, cache_v, c_ctx, w_ada, b_ada, w_in, w_out,
           lam_q1, lam_k1, lam_q2, lam_k2, subln_w, fourier_w, gmlp_ws, gmlp_bs, ln_g, ln_b):
    depth, d, _ = w_ada.shape
    nb_lat = c.shape[0]
    l_ctx, l_lat = x_prompt.shape[1], x_sample.shape[1]
    assert 1 + nb_lat <= MOD_ROWS
    alpha = (2.0 * depth) ** 0.25

    cvec = jnp.concatenate([c_ctx[None, :], c, jnp.zeros((MOD_ROWS - 1 - nb_lat, d), F32)], axis=0)
    mod = _modulation(cvec, w_ada, b_ada)

    fa, fb = _fourier_channel_maps(fourier_w, 1.0 / math.sqrt(FOURIER_GROUP_DIM))
    params = {
        "w_in": w_in.astype(BF16), "w_out": w_out.astype(BF16), "fa": fa, "fb": fb,
        "lam": tuple(t[:, None, :] for t in (lam_q1, lam_k1, lam_q2, lam_k2)),
        "subln_w": subln_w[:, None, :], "ln_g": ln_g[:, None, :], "ln_b": ln_b[:, None, :],
        "gmlp_w": gmlp_ws.astype(BF16).transpose(0, 2, 1, 3).reshape(depth, CHUNK, N_GMLP_HEADS * CHUNK),
        "gmlp_bias": jnp.repeat(gmlp_bs.astype(F32).transpose(0, 2, 1), GMLP_HEAD_DIM, axis=2),
        "avg": jnp.asarray(np.kron(np.eye(N_GMLP_HEADS), np.full((GMLP_HEAD_DIM,) * 2, 1.0 / GMLP_HEAD_DIM)), BF16),
    }
    ctx_tiles = dict(tm_in=256, tq=128, ck=256, tr=256, nb=2, tm_gmlp=256, tm_out=256)
    lat_tiles = dict(tm_in=512, tq=256, ck=512, tr=512, nb=2, tm_gmlp=512, tm_out=512)

    g_ctx = _dft_matrix(l_ctx)
    xp = x_prompt
    ks_new, vs_new = [], []
    for layer in range(depth):
        lam_init = 0.8 - 0.6 * math.exp(-0.3 * layer)
        mod3 = mod[layer, 0:1].reshape(1, 3, d)
        xp, (k_l, v_l) = _run_layer(xp, mod3, False, layer, lam_init, alpha, params, g_ctx,
                                    None, None, True, ctx_tiles)
        ks_new.append(k_l)
        vs_new.append(v_l)
    new_k = jnp.stack(ks_new, axis=1)
    new_v = jnp.stack(vs_new, axis=1)

    g_lat = _dft_matrix(l_lat)
    rope_tabs = _rope_tables(l_lat)
    xs = x_sample
    for layer in range(depth):
        lam_init = 0.8 - 0.6 * math.exp(-0.3 * layer)
        mod3 = mod[layer, 1:1 + nb_lat].reshape(nb_lat, 3, d)
        xs, _ = _run_layer(xs, mod3, True, layer, lam_init, alpha, params, g_lat,
                           rope_tabs, (cache_k, cache_v), False, lat_tiles)
    return (xp, xs, new_k, new_v)
```

```python
import functools
import math

import numpy as np
import jax
import jax.numpy as jnp
from jax import lax
from jax.experimental import pallas as pl
from jax.experimental.pallas import tpu as pltpu

F32 = jnp.float32
BF16 = jnp.bfloat16

GRID_W = 64
HEAD_DIM = 64
N_ATT_HEADS = 4
HEAD_W = 2 * HEAD_DIM
ATT_WIDTH = N_ATT_HEADS * HEAD_W
N_FOURIER_GROUPS = 4
FOURIER_GROUP_DIM = 64
FOURIER_WIDTH = N_FOURIER_GROUPS * FOURIER_GROUP_DIM
N_GMLP_HEADS = 4
GMLP_HEAD_DIM = 64
GMLP_WIDTH = N_GMLP_HEADS * GMLP_HEAD_DIM
CHUNK = 128
ROPE_THETA = 10000.0
LN_EPS = 1e-5
RMS_EPS = 1e-5
QK_SCALE = HEAD_DIM ** -0.5
LOG2_E = math.log2(math.e)

_SEG_NAMES = ("q", "k", "v", "g_att", "f", "g_f", "u", "vm", "g_m")
_SEG_WIDTHS = (ATT_WIDTH,) * 4 + (FOURIER_WIDTH,) * 2 + (GMLP_WIDTH,) * 3
_SEG_START = dict(zip(_SEG_NAMES, np.concatenate([[0], np.cumsum(_SEG_WIDTHS)[:-1]]).tolist()))
_SEG_WIDTH = dict(zip(_SEG_NAMES, _SEG_WIDTHS))

VMEM_LIMIT_BYTES = 52 * 1024 * 1024
MOD_ROWS = 16


def _cparams(*sem):
    return pltpu.CompilerParams(dimension_semantics=sem, vmem_limit_bytes=VMEM_LIMIT_BYTES)


def _silu(x):
    return x * jax.nn.sigmoid(x)


def _row_tile(l, want):
    t = min(l, want)
    assert l % t == 0
    return t


def _mod_kernel(c_ref, w_ref, b_ref, o_ref):
    s = _silu(c_ref[...])
    o_ref[0] = jnp.dot(s, w_ref[0], precision=lax.Precision.HIGHEST,
                       preferred_element_type=F32) + b_ref[0]


def _modulation(cvec, w_ada, b_ada):
    depth, d, d3 = w_ada.shape
    tn = 1024
    return pl.pallas_call(
        _mod_kernel,
        grid=(depth, d3 // tn),
        in_specs=[pl.BlockSpec((MOD_ROWS, d), lambda l, j: (0, 0)),
                  pl.BlockSpec((1, d, tn), lambda l, j: (l, 0, j)),
                  pl.BlockSpec((1, 1, tn), lambda l, j: (l, 0, j))],
        out_specs=pl.BlockSpec((1, MOD_ROWS, tn), lambda l, j: (l, 0, j)),
        out_shape=jax.ShapeDtypeStruct((depth, MOD_ROWS, d3), F32),
        name="adaln_mod",
        compiler_params=_cparams("arbitrary", "arbitrary"),
    )(cvec, w_ada, b_ada.reshape(depth, 1, d3))


def _chanmap_kernel(cc_ref, ss_ref, w_ref, a_ref, b_ref, *, scale):
    w = w_ref[0]
    a_ref[0] = (jnp.dot(cc_ref[...], w, precision=lax.Precision.HIGHEST,
                        preferred_element_type=F32) * scale).astype(BF16)
    b_ref[0] = (jnp.dot(ss_ref[...], w, precision=lax.Precision.HIGHEST,
                        preferred_element_type=F32) * scale).astype(BF16)


def _fourier_channel_maps(fourier_w, scale):
    depth = fourier_w.shape[0]
    n = np.arange(FOURIER_GROUP_DIM)
    ang = 2.0 * np.pi * ((n[:, None] * n[None, :]) % FOURIER_GROUP_DIM) / FOURIER_GROUP_DIM
    eye = np.eye(N_FOURIER_GROUPS)
    cc = jnp.asarray(np.kron(eye, np.cos(ang)), F32)
    ss = jnp.asarray(np.kron(eye, np.sin(ang)), F32)
    wbd = jnp.einsum("gh,lgce->lgche", jnp.eye(N_FOURIER_GROUPS, dtype=F32),
                     fourier_w.astype(F32)).reshape(depth, FOURIER_WIDTH, FOURIER_WIDTH)
    w_spec = pl.BlockSpec((1, FOURIER_WIDTH, FOURIER_WIDTH), lambda l: (l, 0, 0))
    c_spec = pl.BlockSpec((FOURIER_WIDTH, FOURIER_WIDTH), lambda l: (0, 0))
    shp = jax.ShapeDtypeStruct((depth, FOURIER_WIDTH, FOURIER_WIDTH), BF16)
    return pl.pallas_call(
        functools.partial(_chanmap_kernel, scale=scale),
        grid=(depth,),
        in_specs=[c_spec, c_spec, w_spec],
        out_specs=[w_spec, w_spec],
        out_shape=[shp, shp],
        name="fourier_chanmap",
        compiler_params=_cparams("arbitrary"),
    )(cc, ss, wbd)


def _dft_kernel(ca_ref, sa_ref, cb_ref, sb_ref, g_ref, *, l):
    ca, sa = ca_ref[0], sa_ref[0]
    cb, sb = cb_ref[...], sb_ref[...]
    norm = 1.0 / math.sqrt(l)
    g_ref[:, 0:l] = ((ca * cb - sa * sb) * norm).astype(BF16)
    g_ref[:, l:2 * l] = ((sa * cb + ca * sb) * -norm).astype(BF16)


def _dft_matrix(l):
    na = l // GRID_W
    k = np.arange(l, dtype=np.int64)
    ang_a = ((np.arange(na)[:, None] * k[None, :]) % na) * (2.0 * np.pi / na)
    ang_b = ((np.arange(GRID_W)[:, None] * k[None, :]) % l) * (2.0 * np.pi / l)
    ca = jnp.asarray(np.cos(ang_a).reshape(na, 1, l), F32)
    sa = jnp.asarray(np.sin(ang_a).reshape(na, 1, l), F32)
    cb, sb = jnp.asarray(np.cos(ang_b), F32), jnp.asarray(np.sin(ang_b), F32)
    a_spec = pl.BlockSpec((1, 1, l), lambda i: (i, 0, 0))
    b_spec = pl.BlockSpec((GRID_W, l), lambda i: (0, 0))
    return pl.pallas_call(
        functools.partial(_dft_kernel, l=l),
        grid=(na,),
        in_specs=[a_spec, a_spec, b_spec, b_spec],
        out_specs=pl.BlockSpec((GRID_W, 2 * l), lambda i: (i, 0)),
        out_shape=jax.ShapeDtypeStruct((l, 2 * l), BF16),
        name=f"dft_matrix_{l}",
        compiler_params=_cparams("arbitrary"),
    )(ca, sa, cb, sb)


def _layernorm(v):
    mu = jnp.mean(v, axis=-1, keepdims=True)
    vc = v - mu
    return vc * lax.rsqrt(jnp.mean(vc * vc, axis=-1, keepdims=True) + LN_EPS)


def _boundary_kernel(*refs, has_out, has_in, rope, emit_kv, alpha, n_chunks):
    refs = list(refs)
    take = lambda n: [refs.pop(0) for _ in range(n)]
    (x_ref,) = take(1)
    if has_out:
        att_ref, fo_ref, gmo_ref, modo_ref, wout_ref, lg_ref, lb_ref = take(7)
    if has_in:
        modi_ref, win_ref, fa_ref, fb_ref, gw_ref, gbias_ref, avg_ref = take(7)
        if rope:
            cos_ref, sa_ref, sb_ref = take(3)
    if has_out:
        (xo_ref,) = take(1)
    if has_in:
        q_ref, k_ref, vt_ref, ga_ref, uf_ref, gf_ref, gm_ref = take(7)
        if emit_kv:
            kf_ref, vf_ref = take(2)

    x = x_ref[0]
    if has_out:
        a0, a1 = ATT_WIDTH, ATT_WIDTH + FOURIER_WIDTH
        y = (jnp.dot(att_ref[0], wout_ref[0, 0:a0, :], preferred_element_type=F32)
             + jnp.dot(fo_ref[0], wout_ref[0, a0:a1, :], preferred_element_type=F32)
             + jnp.dot(gmo_ref[0], wout_ref[0, a1:, :], preferred_element_type=F32))
        x = _layernorm(alpha * x + modo_ref[0, 0, 2:3, :] * y) * lg_ref[0] + lb_ref[0]
        xo_ref[0] = x
    if not has_in:
        return

    shift, scale = modi_ref[0, 0, 0:1, :], modi_ref[0, 0, 1:2, :]
    h = (_layernorm(x) * (1.0 + scale) + shift).astype(BF16)

    def proj(name):
        c0 = _SEG_START[name]
        return jnp.dot(h, win_ref[0, :, c0:c0 + _SEG_WIDTH[name]], preferred_element_type=F32)

    def rotate(z):
        return (z * cos_ref[...] + pltpu.roll(z, HEAD_W - 16, 1) * sa_ref[...]
                + pltpu.roll(z, 16, 1) * sb_ref[...])

    zq, zk, zv = proj("q"), proj("k"), proj("v")
    for hh in range(N_ATT_HEADS):
        sl = slice(hh * HEAD_W, (hh + 1) * HEAD_W)
        qh, kh = zq[:, sl], zk[:, sl]
        if emit_kv:
            kf_ref[0, hh] = kh
            vf_ref[0, hh] = zv[:, sl]
        if rope:
            qh, kh = rotate(qh), rotate(kh)
        q_ref[0, :, sl] = (qh * (QK_SCALE * LOG2_E)).astype(BF16)
        k_ref[0, :, sl] = kh.astype(BF16)
    vt_ref[0] = zv.T.astype(BF16)
    ga_ref[0] = proj("g_att")
    f = proj("f").astype(BF16)
    uf_ref[0, 0] = jnp.dot(f, fa_ref[0], preferred_element_type=F32).astype(BF16)
    uf_ref[0, 1] = jnp.dot(f, fb_ref[0], preferred_element_type=F32).astype(BF16)
    gf_ref[0] = proj("g_f")

    zu, zvm, zgm = proj("u"), proj("vm"), proj("g_m")
    avg = avg_ref[...]

    def group_mean(v):
        hi = v.astype(BF16)
        lo = (v - hi.astype(F32)).astype(BF16)
        return (jnp.dot(hi, avg, preferred_element_type=F32)
                + jnp.dot(lo, avg, preferred_element_type=F32))

    lane = lax.broadcasted_iota(jnp.int32, (CHUNK, GMLP_WIDTH), 1)
    head_mask = [(lane >= hh * GMLP_HEAD_DIM) & (lane < (hh + 1) * GMLP_HEAD_DIM)
                 for hh in range(N_GMLP_HEADS)]
    for c in range(n_chunks):
        rows = slice(c * CHUNK, (c + 1) * CHUNK)
        v = zvm[rows]
        vc = v - group_mean(v)
        vn = (vc * lax.rsqrt(group_mean(vc * vc) + LN_EPS)).astype(BF16)
        stacked = jnp.concatenate(
            [jnp.where(head_mask[hh], vn, jnp.zeros_like(vn)) for hh in range(N_GMLP_HEADS)], axis=0)
        s = jnp.dot(gw_ref[0], stacked, preferred_element_type=F32) + gbias_ref[0]
        gm_ref[0, rows, :] = (zu[rows] * s * _silu(zgm[rows])).astype(BF16)


def _boundary(x, branches, mod3, per_batch_mod, p, out_layer, in_layer, rope_tabs, emit_kv, alpha, tm):
    b, l, d = x.shape
    tm = _row_tile(l, tm)
    has_out, has_in = out_layer is not None, in_layer is not None
    row = lambda w: pl.BlockSpec((1, tm, w), lambda bi, i: (bi, i, 0))
    const2 = lambda shp: pl.BlockSpec(shp, lambda bi, i: (0, 0))
    per_layer = lambda shp, layer: pl.BlockSpec((1,) + shp, lambda bi, i: (layer, 0, 0))
    mod_spec = lambda layer: pl.BlockSpec(
        (1, 1, 3, d), (lambda bi, i: (layer, bi, 0, 0)) if per_batch_mod else (lambda bi, i: (layer, 0, 0, 0)))
    sds = jax.ShapeDtypeStruct
    in_specs, args, out_specs, out_shape = [row(d)], [x], [], []
    if has_out:
        in_specs += [row(ATT_WIDTH), row(FOURIER_WIDTH), row(GMLP_WIDTH), mod_spec(out_layer),
                     per_layer((d, d), out_layer), per_layer((1, d), out_layer), per_layer((1, d), out_layer)]
        args += list(branches) + [mod3, p["w_out"], p["ln_g"], p["ln_b"]]
        out_specs += [row(d)]
        out_shape += [sds((b, l, d), F32)]
    if has_in:
        n_in = p["w_in"].shape[2]
        in_specs += [mod_spec(in_layer), per_layer((d, n_in), in_layer),
                     per_layer((FOURIER_WIDTH, FOURIER_WIDTH), in_layer),
                     per_layer((FOURIER_WIDTH, FOURIER_WIDTH), in_layer),
                     per_layer((CHUNK, N_GMLP_HEADS * CHUNK), in_layer),
                     per_layer((CHUNK, GMLP_WIDTH), in_layer), const2((GMLP_WIDTH, GMLP_WIDTH))]
        args += [mod3, p["w_in"], p["fa"], p["fb"], p["gmlp_w"], p["gmlp_bias"], p["avg"]]
        if rope_tabs is not None:
            in_specs += [pl.BlockSpec((tm, HEAD_W), lambda bi, i: (i, 0))] * 3
            args += list(rope_tabs)
        out_shape += [sds((b, l, ATT_WIDTH), BF16)] * 2 + [
            sds((b, ATT_WIDTH, l), BF16), sds((b, l, ATT_WIDTH), F32),
            sds((b, 2, l, FOURIER_WIDTH), BF16), sds((b, l, FOURIER_WIDTH), F32),
            sds((b, l, GMLP_WIDTH), BF16)]
        out_specs += [row(ATT_WIDTH)] * 2 + [
            pl.BlockSpec((1, ATT_WIDTH, tm), lambda bi, i: (bi, 0, i)), row(ATT_WIDTH),
            pl.BlockSpec((1, 2, tm, FOURIER_WIDTH), lambda bi, i: (bi, 0, i, 0)),
            row(FOURIER_WIDTH), row(GMLP_WIDTH)]
        if emit_kv:
            kv = sds((b, N_ATT_HEADS, l, HEAD_W), F32)
            out_shape += [kv, kv]
            out_specs += [pl.BlockSpec((1, N_ATT_HEADS, tm, HEAD_W), lambda bi, i: (bi, 0, i, 0))] * 2
    kern = functools.partial(_boundary_kernel, has_out=has_out, has_in=has_in,
                             rope=rope_tabs is not None, emit_kv=emit_kv and has_in, alpha=alpha,
                             n_chunks=tm // CHUNK)

    return pl.pallas_call(
        kern, grid=(b, l // tm), in_specs=in_specs, out_specs=out_specs, out_shape=out_shape,
        name=f"boundary_{l}_{int(has_out)}{int(has_in)}",
        compiler_params=_cparams("parallel", "parallel"),
    )(*args)


def _attn_kernel(*refs, lam_init, has_cache, ck, n_new, past, tq, nq):
    if has_cache:
        (q_ref, k_ref, vt_ref, pk_ref, pv_ref, g_ref, lq1_ref, lk1_ref, lq2_ref, lk2_ref,
         sw_ref, o_ref, p_ref, lv_ref, pend_ref, bad_ref, pvt_ref) = refs
    else:
        (q_ref, k_ref, vt_ref, g_ref, lq1_ref, lk1_ref, lq2_ref, lk2_ref,
         sw_ref, o_ref, p_ref, lv_ref, pend_ref, bad_ref) = refs

    chunks = []
    if has_cache:
        chunks.append((True, 0, 0, past))
    for j in range(n_new):
        chunks.append((False, j * ck, past + j * ck, ck))

    if has_cache:
        pvt_ref[...] = pv_ref[0, 0, 0].T.astype(BF16)

    def keys(c):
        return pk_ref[0, 0, 0].astype(BF16) if c[0] else k_ref[0, c[1]:c[1] + c[3], :]

    def vals_t(c):
        return pvt_ref[...] if c[0] else vt_ref[0, :, c[1]:c[1] + c[3]]

    def scores_t(qq, c):
        return lax.dot_general(keys(c), qq, (((1,), (1,)), ((), ())), preferred_element_type=F32)

    lam = (jnp.exp(jnp.sum(lq1_ref[0] * lk1_ref[0], axis=-1, keepdims=True))
           - jnp.exp(jnp.sum(lq2_ref[0] * lk2_ref[0], axis=-1, keepdims=True)) + lam_init)

    def rows_of(t):
        return pl.ds(pl.multiple_of(t * tq, tq), tq)

    def stacked_q(t):
        q = q_ref[0, rows_of(t), :]
        lane = lax.broadcasted_iota(jnp.int32, q.shape, 1)
        zero = jnp.zeros_like(q)
        return jnp.concatenate([jnp.where(lane < HEAD_DIM, q, zero),
                                jnp.where(lane >= HEAD_DIM, q, zero)], axis=0)

    def scores_phase(t, slot):
        qq = stacked_q(t)
        s0 = scores_t(qq, chunks[0])
        stab = jnp.max(s0, axis=0, keepdims=True)
        lv = jnp.zeros((1, 2 * tq), F32)
        for ci, c in enumerate(chunks):
            s = s0 if ci == 0 else scores_t(qq, c)
            p = jnp.exp2(s - stab)
            p_ref[slot, c[2]:c[2] + c[3], :] = p
            lv = lv + jnp.sum(p, axis=0, keepdims=True)
        lv_ref[slot] = jnp.broadcast_to(lv, (8, 2 * tq))

    def values_phase(slot):
        lsum = lv_ref[slot][0:1, :]
        l1, l2 = lsum[:, :tq], lsum[:, tq:]
        rho = lam * l1 / l2
        acc = jnp.zeros((HEAD_W, tq), F32)
        for c in chunks:
            a = (p_ref[slot, c[2]:c[2] + c[3], 0:tq]
                 - rho * p_ref[slot, c[2]:c[2] + c[3], tq:2 * tq]).astype(BF16)
            acc = acc + jnp.dot(vals_t(c), a, preferred_element_type=F32)
        return (acc / l1).T

    def finish(o, rows):
        o = (o * lax.rsqrt(jnp.mean(o * o, axis=-1, keepdims=True) + RMS_EPS)
             * sw_ref[0] * (1.0 - lam_init))
        return o * _silu(g_ref[0, rows, :])

    def emit(t, slot):
        rows = rows_of(t)
        out = finish(pend_ref[slot], rows)
        o_ref[0, rows, :] = out.astype(BF16)
        bad_ref[t] = jnp.sum(jnp.where(jnp.isfinite(out), 0.0, 1.0))

    scores_phase(0, 0)
    pend_ref[0] = values_phase(0)
    scores_phase(1, 1)

    def pair(i, carry):
        t = 2 * i + 2
        emit(t - 2, 0)
        pend_ref[1] = values_phase(1)
        scores_phase(t, 0)
        emit(t - 1, 1)
        pend_ref[0] = values_phase(0)
        scores_phase(t + 1, 1)
        return carry

    lax.fori_loop(0, nq // 2 - 1, pair, 0)
    emit(nq - 2, 0)
    pend_ref[1] = values_phase(1)
    emit(nq - 1, 1)

    def exact_tile(t, carry):
        @pl.when(bad_ref[t] > 0.0)
        def _():
            qq = stacked_q(t)
            m = None
            for c in chunks:
                s = scores_t(qq, c)
                p_ref[0, c[2]:c[2] + c[3], :] = s
                cm = jnp.max(s, axis=0, keepdims=True)
                m = cm if m is None else jnp.maximum(m, cm)
            lv = jnp.zeros((1, 2 * tq), F32)
            for c in chunks:
                p = jnp.exp2(p_ref[0, c[2]:c[2] + c[3], :] - m)
                p_ref[0, c[2]:c[2] + c[3], :] = p
                lv = lv + jnp.sum(p, axis=0, keepdims=True)
            lv_ref[0] = jnp.broadcast_to(lv, (8, 2 * tq))
            rows = rows_of(t)
            o_ref[0, rows, :] = finish(values_phase(0), rows).astype(BF16)
        return carry

    lax.fori_loop(0, nq, exact_tile, 0)


def _attention(q, k, v, g_att, cache, layer, lam_params, subln_w, lam_init, tq, ck):
    b, l, _ = q.shape
    tq = _row_tile(l, tq)
    ck = _row_tile(l, ck)
    nq = l // tq
    assert nq >= 2 and nq % 2 == 0
    has_cache = cache is not None
    past = cache[0].shape[3] if has_cache else 0
    lk = past + l
    seq = pl.BlockSpec((1, l, HEAD_W), lambda bi, h: (bi, 0, h))
    vec = lambda w: pl.BlockSpec((1, 1, w), lambda bi, h: (layer, 0, 0))
    vt_spec = pl.BlockSpec((1, HEAD_W, l), lambda bi, h: (bi, h, 0))
    in_specs = [seq, seq, vt_spec]
    args = [q, k, v]
    if has_cache:
        cspec = pl.BlockSpec((1, 1, 1, past, HEAD_W), lambda bi, h: (bi, layer, h, 0, 0))
        in_specs += [cspec, cspec]
        args += list(cache)
    in_specs += [seq] + [vec(HEAD_DIM)] * 4 + [vec(HEAD_W)]
    args += [g_att] + list(lam_params) + [subln_w]
    return pl.pallas_call(
        functools.partial(_attn_kernel, lam_init=lam_init, has_cache=has_cache, ck=ck,
                          n_new=l // ck, past=past, tq=tq, nq=nq),
        grid=(b, N_ATT_HEADS),
        in_specs=in_specs,
        out_specs=seq,
        out_shape=jax.ShapeDtypeStruct((b, l, ATT_WIDTH), BF16),
        scratch_shapes=[pltpu.VMEM((2, lk, 2 * tq), F32), pltpu.VMEM((2, 8, 2 * tq), F32),
                        pltpu.VMEM((2, tq, HEAD_W), F32), pltpu.SMEM((nq,), F32)]
        + ([pltpu.VMEM((HEAD_W, past), BF16)] if has_cache else []),
        name=f"attn_{l}",
        compiler_params=_cparams("parallel", "parallel"),
    )(*args)


def _fourier_kernel(g_ref, u_ref, gate_ref, o_ref, *, nb):
    g = g_ref[...]
    for j in range(nb):
        y = jnp.dot(g, u_ref[j], preferred_element_type=F32)
        o_ref[j] = (y * _silu(gate_ref[j])).astype(BF16)


def _fourier(gmat, uf, g_f, tr, nb):
    b, l, w = g_f.shape
    tr = _row_tile(l, tr)
    u2 = uf.reshape(b, 2 * l, w)
    return pl.pallas_call(
        functools.partial(_fourier_kernel, nb=nb),
        grid=(b // nb, l // tr),
        in_specs=[pl.BlockSpec((tr, 2 * l), lambda bi, i: (i, 0)),
                  pl.BlockSpec((nb, 2 * l, w), lambda bi, i: (bi, 0, 0)),
                  pl.BlockSpec((nb, tr, w), lambda bi, i: (bi, i, 0))],
        out_specs=pl.BlockSpec((nb, tr, w), lambda bi, i: (bi, i, 0)),
        out_shape=jax.ShapeDtypeStruct((b, l, w), BF16),
        name=f"fourier_{l}",
        compiler_params=_cparams("parallel", "arbitrary"),
    )(gmat, u2, g_f)


def _rope_tables(l):
    n_freq = HEAD_DIM // 4
    rows = l // GRID_W
    row = np.repeat(np.arange(rows, dtype=np.float64), GRID_W)
    col = np.tile(np.arange(GRID_W, dtype=np.float64), rows)
    inv = ROPE_THETA ** (-np.arange(n_freq, dtype=np.float64) / n_freq)
    ang_r, ang_c = row[:, None] * inv, col[:, None] * inv
    zeros = np.zeros_like(ang_r)

    def per_head(r_lo, r_hi, c_lo, c_hi):
        one = np.concatenate([r_lo, r_hi, c_lo, c_hi], axis=-1)
        return jnp.asarray(np.concatenate([one, one], axis=-1), F32)

    cr, sr, cc, sc = np.cos(ang_r), np.sin(ang_r), np.cos(ang_c), np.sin(ang_c)
    cos_t = per_head(cr, cr, cc, cc)
    sa_t = per_head(-sr, zeros, -sc, zeros)
    sb_t = per_head(zeros, sr, zeros, sc)
    return cos_t, sa_t, sb_t


def _run_group(x, mod3, per_batch_mod, p, gmat, rope_tabs, cache, emit_kv, alpha, tiles):
    depth = p["w_in"].shape[0]
    kv_new = []
    outs = _boundary(x, None, mod3, per_batch_mod, p, None, 0, rope_tabs, emit_kv, alpha, tiles["tm"])
    for layer in range(depth):
        lam_init = 0.8 - 0.6 * math.exp(-0.3 * layer)
        q, k, vt, g_att, uf, g_f, gm = outs[:7]
        kv_new.append(outs[7:])
        att = _attention(q, k, vt, g_att, cache, layer, p["lam"], p["subln_w"], lam_init,
                         tiles["tq"], tiles["ck"])
        fo = _fourier(gmat, uf, g_f, tiles["tr"], tiles["nb"])
        nxt = layer + 1 if layer + 1 < depth else None
        res = _boundary(x, (att, fo, gm), mod3, per_batch_mod, p, layer, nxt, rope_tabs, emit_kv,
                        alpha, tiles["tm"])
        x, outs = res[0], res[1:]
    return x, kv_new


def kernel(x_prompt, x_sample, c, cache_k, cache_v, c_ctx, w_ada, b_ada, w_in, w_out,
           lam_q1, lam_k1, lam_q2, lam_k2, subln_w, fourier_w, gmlp_ws, gmlp_bs, ln_g, ln_b):
    depth, d, _ = w_ada.shape
    nb_lat = c.shape[0]
    l_ctx, l_lat = x_prompt.shape[1], x_sample.shape[1]
    assert 1 + nb_lat <= MOD_ROWS
    alpha = (2.0 * depth) ** 0.25

    cvec = jnp.concatenate([c_ctx[None, :], c, jnp.zeros((MOD_ROWS - 1 - nb_lat, d), F32)], axis=0)
    mod = _modulation(cvec, w_ada, b_ada)

    fa, fb = _fourier_channel_maps(fourier_w, 1.0 / math.sqrt(FOURIER_GROUP_DIM))
    params = {
        "w_in": w_in.astype(BF16), "w_out": w_out.astype(BF16), "fa": fa, "fb": fb,
        "lam": tuple(t[:, None, :] for t in (lam_q1, lam_k1, lam_q2, lam_k2)),
        "subln_w": subln_w[:, None, :], "ln_g": ln_g[:, None, :], "ln_b": ln_b[:, None, :],
        "gmlp_w": gmlp_ws.astype(BF16).transpose(0, 2, 1, 3).reshape(depth, CHUNK, N_GMLP_HEADS * CHUNK),
        "gmlp_bias": jnp.repeat(gmlp_bs.astype(F32).transpose(0, 2, 1), GMLP_HEAD_DIM, axis=2),
        "avg": jnp.asarray(np.kron(np.eye(N_GMLP_HEADS), np.full((GMLP_HEAD_DIM,) * 2, 1.0 / GMLP_HEAD_DIM)), BF16),
    }
    ctx_tiles = dict(tm=256, tq=128, ck=256, tr=256, nb=2)
    lat_tiles = dict(tm=512, tq=256, ck=512, tr=512, nb=2)

    xp, kv_new = _run_group(x_prompt, mod[:, 0:1].reshape(depth, 1, 3, d), False, params,
                            _dft_matrix(l_ctx), None, None, True, alpha, ctx_tiles)
    new_k = jnp.stack([kv[0] for kv in kv_new], axis=1)
    new_v = jnp.stack([kv[1] for kv in kv_new], axis=1)

    xs, _ = _run_group(x_sample, mod[:, 1:1 + nb_lat].reshape(depth, nb_lat, 3, d), True, params,
                       _dft_matrix(l_lat), _rope_tables(l_lat), (cache_k, cache_v), False, alpha,
                       lat_tiles)
    return (xp, xs, new_k, new_v)
```

```python
import functools
import math

import numpy as np
import jax
import jax.numpy as jnp
from jax import lax
from jax.experimental import pallas as pl
from jax.experimental.pallas import tpu as pltpu

F32 = jnp.float32
BF16 = jnp.bfloat16

GRID_W = 64
HEAD_DIM = 64
N_ATT_HEADS = 4
HEAD_W = 2 * HEAD_DIM
ATT_WIDTH = N_ATT_HEADS * HEAD_W
N_FOURIER_GROUPS = 4
FOURIER_GROUP_DIM = 64
FOURIER_WIDTH = N_FOURIER_GROUPS * FOURIER_GROUP_DIM
N_GMLP_HEADS = 4
GMLP_HEAD_DIM = 64
GMLP_WIDTH = N_GMLP_HEADS * GMLP_HEAD_DIM
CHUNK = 128
ROPE_THETA = 10000.0
LN_EPS = 1e-5
RMS_EPS = 1e-5
QK_SCALE = HEAD_DIM ** -0.5
LOG2_E = math.log2(math.e)

_SEG_NAMES = ("q", "k", "v", "g_att", "f", "g_f", "u", "vm", "g_m")
_SEG_WIDTHS = (ATT_WIDTH,) * 4 + (FOURIER_WIDTH,) * 2 + (GMLP_WIDTH,) * 3
_SEG_START = dict(zip(_SEG_NAMES, np.concatenate([[0], np.cumsum(_SEG_WIDTHS)[:-1]]).tolist()))
_SEG_WIDTH = dict(zip(_SEG_NAMES, _SEG_WIDTHS))

VMEM_LIMIT_BYTES = 52 * 1024 * 1024
MOD_ROWS = 16


def _cparams(*sem):
    return pltpu.CompilerParams(dimension_semantics=sem, vmem_limit_bytes=VMEM_LIMIT_BYTES)


def _silu(x):
    return x * jax.nn.sigmoid(x)


def _row_tile(l, want):
    t = min(l, want)
    assert l % t == 0
    return t


def _mod_kernel(c_ref, w_ref, b_ref, o_ref):
    s = _silu(c_ref[...])
    o_ref[0] = jnp.dot(s, w_ref[0], precision=lax.Precision.HIGHEST,
                       preferred_element_type=F32) + b_ref[0]


def _modulation(cvec, w_ada, b_ada):
    depth, d, d3 = w_ada.shape
    tn = 1024
    return pl.pallas_call(
        _mod_kernel,
        grid=(depth, d3 // tn),
        in_specs=[pl.BlockSpec((MOD_ROWS, d), lambda l, j: (0, 0)),
                  pl.BlockSpec((1, d, tn), lambda l, j: (l, 0, j)),
                  pl.BlockSpec((1, 1, tn), lambda l, j: (l, 0, j))],
        out_specs=pl.BlockSpec((1, MOD_ROWS, tn), lambda l, j: (l, 0, j)),
        out_shape=jax.ShapeDtypeStruct((depth, MOD_ROWS, d3), F32),
        name="adaln_mod",
        compiler_params=_cparams("arbitrary", "arbitrary"),
    )(cvec, w_ada, b_ada.reshape(depth, 1, d3))


def _chanmap_kernel(cc_ref, ss_ref, w_ref, a_ref, b_ref, *, scale):
    w = w_ref[0]
    a_ref[0] = (jnp.dot(cc_ref[...], w, precision=lax.Precision.HIGHEST,
                        preferred_element_type=F32) * scale).astype(BF16)
    b_ref[0] = (jnp.dot(ss_ref[...], w, precision=lax.Precision.HIGHEST,
                        preferred_element_type=F32) * scale).astype(BF16)


def _fourier_channel_maps(fourier_w, scale):
    depth = fourier_w.shape[0]
    n = np.arange(FOURIER_GROUP_DIM)
    ang = 2.0 * np.pi * ((n[:, None] * n[None, :]) % FOURIER_GROUP_DIM) / FOURIER_GROUP_DIM
    eye = np.eye(N_FOURIER_GROUPS)
    cc = jnp.asarray(np.kron(eye, np.cos(ang)), F32)
    ss = jnp.asarray(np.kron(eye, np.sin(ang)), F32)
    wbd = jnp.einsum("gh,lgce->lgche", jnp.eye(N_FOURIER_GROUPS, dtype=F32),
                     fourier_w.astype(F32)).reshape(depth, FOURIER_WIDTH, FOURIER_WIDTH)
    w_spec = pl.BlockSpec((1, FOURIER_WIDTH, FOURIER_WIDTH), lambda l: (l, 0, 0))
    c_spec = pl.BlockSpec((FOURIER_WIDTH, FOURIER_WIDTH), lambda l: (0, 0))
    shp = jax.ShapeDtypeStruct((depth, FOURIER_WIDTH, FOURIER_WIDTH), BF16)
    return pl.pallas_call(
        functools.partial(_chanmap_kernel, scale=scale),
        grid=(depth,),
        in_specs=[c_spec, c_spec, w_spec],
        out_specs=[w_spec, w_spec],
        out_shape=[shp, shp],
        name="fourier_chanmap",
        compiler_params=_cparams("arbitrary"),
    )(cc, ss, wbd)


def _dft_kernel(ca_ref, sa_ref, cb_ref, sb_ref, g_ref, *, l):
    ca, sa = ca_ref[0], sa_ref[0]
    cb, sb = cb_ref[...], sb_ref[...]
    norm = 1.0 / math.sqrt(l)
    g_ref[:, 0:l] = ((ca * cb - sa * sb) * norm).astype(BF16)
    g_ref[:, l:2 * l] = ((sa * cb + ca * sb) * -norm).astype(BF16)


def _dft_matrix(l):
    na = l // GRID_W
    k = np.arange(l, dtype=np.int64)
    ang_a = ((np.arange(na)[:, None] * k[None, :]) % na) * (2.0 * np.pi / na)
    ang_b = ((np.arange(GRID_W)[:, None] * k[None, :]) % l) * (2.0 * np.pi / l)
    ca = jnp.asarray(np.cos(ang_a).reshape(na, 1, l), F32)
    sa = jnp.asarray(np.sin(ang_a).reshape(na, 1, l), F32)
    cb, sb = jnp.asarray(np.cos(ang_b), F32), jnp.asarray(np.sin(ang_b), F32)
    a_spec = pl.BlockSpec((1, 1, l), lambda i: (i, 0, 0))
    b_spec = pl.BlockSpec((GRID_W, l), lambda i: (0, 0))
    return pl.pallas_call(
        functools.partial(_dft_kernel, l=l),
        grid=(na,),
        in_specs=[a_spec, a_spec, b_spec, b_spec],
        out_specs=pl.BlockSpec((GRID_W, 2 * l), lambda i: (i, 0)),
        out_shape=jax.ShapeDtypeStruct((l, 2 * l), BF16),
        name=f"dft_matrix_{l}",
        compiler_params=_cparams("arbitrary"),
    )(ca, sa, cb, sb)


def _layernorm(v):
    mu = jnp.mean(v, axis=-1, keepdims=True)
    vc = v - mu
    return vc * lax.rsqrt(jnp.mean(vc * vc, axis=-1, keepdims=True) + LN_EPS)


def _boundary_kernel(*refs, has_out, has_in, rope, emit_kv, alpha, n_chunks):
    refs = list(refs)
    take = lambda n: [refs.pop(0) for _ in range(n)]
    (x_ref,) = take(1)
    if has_out:
        att_ref, fo_ref, gmo_ref, modo_ref, wout_ref, lg_ref, lb_ref = take(7)
    if has_in:
        modi_ref, win_ref, fa_ref, fb_ref, gw_ref, gbias_ref, avg_ref = take(7)
        if rope:
            cos_ref, sa_ref, sb_ref = take(3)
    if has_out:
        (xo_ref,) = take(1)
    if has_in:
        q_ref, k_ref, vt_ref, ga_ref, uf_ref, gf_ref, gm_ref = take(7)
        if emit_kv:
            kf_ref, vf_ref = take(2)

    x = x_ref[0]
    if has_out:
        a0, a1 = ATT_WIDTH, ATT_WIDTH + FOURIER_WIDTH
        y = (jnp.dot(att_ref[0], wout_ref[0, 0:a0, :], preferred_element_type=F32)
             + jnp.dot(fo_ref[0], wout_ref[0, a0:a1, :], preferred_element_type=F32)
             + jnp.dot(gmo_ref[0], wout_ref[0, a1:, :], preferred_element_type=F32))
        x = _layernorm(alpha * x + modo_ref[0, 0, 2:3, :] * y) * lg_ref[0] + lb_ref[0]
        xo_ref[0] = x
    if not has_in:
        return

    shift, scale = modi_ref[0, 0, 0:1, :], modi_ref[0, 0, 1:2, :]
    h = (_layernorm(x) * (1.0 + scale) + shift).astype(BF16)

    def proj(name):
        c0 = _SEG_START[name]
        return jnp.dot(h, win_ref[0, :, c0:c0 + _SEG_WIDTH[name]], preferred_element_type=F32)

    def rotate(z):
        return (z * cos_ref[...] + pltpu.roll(z, HEAD_W - 16, 1) * sa_ref[...]
                + pltpu.roll(z, 16, 1) * sb_ref[...])

    zq, zk, zv = proj("q"), proj("k"), proj("v")
    for hh in range(N_ATT_HEADS):
        sl = slice(hh * HEAD_W, (hh + 1) * HEAD_W)
        qh, kh = zq[:, sl], zk[:, sl]
        if emit_kv:
            kf_ref[0, hh] = kh
            vf_ref[0, hh] = zv[:, sl]
        if rope:
            qh, kh = rotate(qh), rotate(kh)
        q_ref[0, :, sl] = (qh * (QK_SCALE * LOG2_E)).astype(BF16)
        k_ref[0, :, sl] = kh.astype(BF16)
    vt_ref[0] = zv.T.astype(BF16)
    ga_ref[0] = proj("g_att")
    f = proj("f").astype(BF16)
    uf_ref[0, 0] = jnp.dot(f, fa_ref[0], preferred_element_type=F32).astype(BF16)
    uf_ref[0, 1] = jnp.dot(f, fb_ref[0], preferred_element_type=F32).astype(BF16)
    gf_ref[0] = proj("g_f")

    zu, zvm, zgm = proj("u"), proj("vm"), proj("g_m")
    avg = avg_ref[...]

    def group_mean(v):
        hi = v.astype(BF16)
        lo = (v - hi.astype(F32)).astype(BF16)
        return (jnp.dot(hi, avg, preferred_element_type=F32)
                + jnp.dot(lo, avg, preferred_element_type=F32))

    lane = lax.broadcasted_iota(jnp.int32, (CHUNK, GMLP_WIDTH), 1)
    head_mask = [(lane >= hh * GMLP_HEAD_DIM) & (lane < (hh + 1) * GMLP_HEAD_DIM)
                 for hh in range(N_GMLP_HEADS)]
    for c in range(n_chunks):
        rows = slice(c * CHUNK, (c + 1) * CHUNK)
        v = zvm[rows]
        vc = v - group_mean(v)
        vn = (vc * lax.rsqrt(group_mean(vc * vc) + LN_EPS)).astype(BF16)
        stacked = jnp.concatenate(
            [jnp.where(head_mask[hh], vn, jnp.zeros_like(vn)) for hh in range(N_GMLP_HEADS)], axis=0)
        s = jnp.dot(gw_ref[0], stacked, preferred_element_type=F32) + gbias_ref[0]
        gm_ref[0, rows, :] = (zu[rows] * s * _silu(zgm[rows])).astype(BF16)


def _boundary(x, branches, mod3, per_batch_mod, p, out_layer, in_layer, rope_tabs, emit_kv, alpha, tm):
    b, l, d = x.shape
    tm = _row_tile(l, tm)
    has_out, has_in = out_layer is not None, in_layer is not None
    row = lambda w: pl.BlockSpec((1, tm, w), lambda bi, i: (bi, i, 0))
    const2 = lambda shp: pl.BlockSpec(shp, lambda bi, i: (0, 0))
    per_layer = lambda shp, layer: pl.BlockSpec((1,) + shp, lambda bi, i: (layer, 0, 0))
    mod_spec = lambda layer: pl.BlockSpec(
        (1, 1, 3, d), (lambda bi, i: (layer, bi, 0, 0)) if per_batch_mod else (lambda bi, i: (layer, 0, 0, 0)))
    sds = jax.ShapeDtypeStruct
    in_specs, args, out_specs, out_shape = [row(d)], [x], [], []
    if has_out:
        in_specs += [row(ATT_WIDTH), row(FOURIER_WIDTH), row(GMLP_WIDTH), mod_spec(out_layer),
                     per_layer((d, d), out_layer), per_layer((1, d), out_layer), per_layer((1, d), out_layer)]
        args += list(branches) + [mod3, p["w_out"], p["ln_g"], p["ln_b"]]
        out_specs += [row(d)]
        out_shape += [sds((b, l, d), F32)]
    if has_in:
        n_in = p["w_in"].shape[2]
        in_specs += [mod_spec(in_layer), per_layer((d, n_in), in_layer),
                     per_layer((FOURIER_WIDTH, FOURIER_WIDTH), in_layer),
                     per_layer((FOURIER_WIDTH, FOURIER_WIDTH), in_layer),
                     per_layer((CHUNK, N_GMLP_HEADS * CHUNK), in_layer),
                     per_layer((CHUNK, GMLP_WIDTH), in_layer), const2((GMLP_WIDTH, GMLP_WIDTH))]
        args += [mod3, p["w_in"], p["fa"], p["fb"], p["gmlp_w"], p["gmlp_bias"], p["avg"]]
        if rope_tabs is not None:
            in_specs += [pl.BlockSpec((tm, HEAD_W), lambda bi, i: (i, 0))] * 3
            args += list(rope_tabs)
        out_shape += [sds((b, l, ATT_WIDTH), BF16)] * 2 + [
            sds((b, ATT_WIDTH, l), BF16), sds((b, l, ATT_WIDTH), F32),
            sds((b, 2, l, FOURIER_WIDTH), BF16), sds((b, l, FOURIER_WIDTH), F32),
            sds((b, l, GMLP_WIDTH), BF16)]
        out_specs += [row(ATT_WIDTH)] * 2 + [
            pl.BlockSpec((1, ATT_WIDTH, tm), lambda bi, i: (bi, 0, i)), row(ATT_WIDTH),
            pl.BlockSpec((1, 2, tm, FOURIER_WIDTH), lambda bi, i: (bi, 0, i, 0)),
            row(FOURIER_WIDTH), row(GMLP_WIDTH)]
        if emit_kv:
            kv = sds((b, N_ATT_HEADS, l, HEAD_W), F32)
            out_shape += [kv, kv]
            out_specs += [pl.BlockSpec((1, N_ATT_HEADS, tm, HEAD_W), lambda bi, i: (bi, 0, i, 0))] * 2
    kern = functools.partial(_boundary_kernel, has_out=has_out, has_in=has_in,
                             rope=rope_tabs is not None, emit_kv=emit_kv and has_in, alpha=alpha,
                             n_chunks=tm // CHUNK)

    return pl.pallas_call(
        kern, grid=(b, l // tm), in_specs=in_specs, out_specs=out_specs, out_shape=out_shape,
        name=f"boundary_{l}_{int(has_out)}{int(has_in)}",
        compiler_params=_cparams("parallel", "parallel"),
    )(*args)


def _attn_kernel(*refs, lam_init, has_cache, ck, n_new, past, tq, nq):
    if has_cache:
        (q_ref, k_ref, vt_ref, pk_ref, pv_ref, g_ref, lq1_ref, lk1_ref, lq2_ref, lk2_ref,
         sw_ref, o_ref, p_ref, lv_ref, pend_ref, bad_ref, pvt_ref) = refs
    else:
        (q_ref, k_ref, vt_ref, g_ref, lq1_ref, lk1_ref, lq2_ref, lk2_ref,
         sw_ref, o_ref, p_ref, lv_ref, pend_ref, bad_ref) = refs

    chunks = []
    if has_cache:
        chunks.append((True, 0, 0, past))
    for j in range(n_new):
        chunks.append((False, j * ck, past + j * ck, ck))

    if has_cache:
        pvt_ref[...] = pv_ref[0, 0, 0].T.astype(BF16)

    def keys(c):
        return pk_ref[0, 0, 0].astype(BF16) if c[0] else k_ref[0, c[1]:c[1] + c[3], :]

    def vals_t(c):
        return pvt_ref[...] if c[0] else vt_ref[0, :, c[1]:c[1] + c[3]]

    def scores_t(qq, c):
        return lax.dot_general(keys(c), qq, (((1,), (1,)), ((), ())), preferred_element_type=F32)

    lam = (jnp.exp(jnp.sum(lq1_ref[0] * lk1_ref[0], axis=-1, keepdims=True))
           - jnp.exp(jnp.sum(lq2_ref[0] * lk2_ref[0], axis=-1, keepdims=True)) + lam_init)

    def rows_of(t):
        return pl.ds(pl.multiple_of(t * tq, tq), tq)

    def stacked_q(t):
        q = q_ref[0, rows_of(t), :]
        lane = lax.broadcasted_iota(jnp.int32, q.shape, 1)
        zero = jnp.zeros_like(q)
        return jnp.concatenate([jnp.where(lane < HEAD_DIM, q, zero),
                                jnp.where(lane >= HEAD_DIM, q, zero)], axis=0)

    def scores_phase(t, slot):
        qq = stacked_q(t)
        s0 = scores_t(qq, chunks[0])
        stab = jnp.max(s0, axis=0, keepdims=True)
        lv = jnp.zeros((1, 2 * tq), F32)
        for ci, c in enumerate(chunks):
            s = s0 if ci == 0 else scores_t(qq, c)
            p = jnp.exp2(s - stab)
            p_ref[slot, c[2]:c[2] + c[3], :] = p
            lv = lv + jnp.sum(p, axis=0, keepdims=True)
        lv_ref[slot] = jnp.broadcast_to(lv, (8, 2 * tq))

    def values_phase(slot):
        lsum = lv_ref[slot][0:1, :]
        l1, l2 = lsum[:, :tq], lsum[:, tq:]
        rho = lam * l1 / l2
        acc = jnp.zeros((HEAD_W, tq), F32)
        for c in chunks:
            a = (p_ref[slot, c[2]:c[2] + c[3], 0:tq]
                 - rho * p_ref[slot, c[2]:c[2] + c[3], tq:2 * tq]).astype(BF16)
            acc = acc + jnp.dot(vals_t(c), a, preferred_element_type=F32)
        return (acc / l1).T

    def finish(o, rows):
        o = (o * lax.rsqrt(jnp.mean(o * o, axis=-1, keepdims=True) + RMS_EPS)
             * sw_ref[0] * (1.0 - lam_init))
        return o * _silu(g_ref[0, rows, :])

    def emit(t, slot):
        rows = rows_of(t)
        out = finish(pend_ref[slot], rows)
        o_ref[0, rows, :] = out.astype(BF16)
        bad_ref[t] = jnp.sum(jnp.where(jnp.isfinite(out), 0.0, 1.0))

    scores_phase(0, 0)
    pend_ref[0] = values_phase(0)
    scores_phase(1, 1)

    def pair(i, carry):
        t = 2 * i + 2
        emit(t - 2, 0)
        pend_ref[1] = values_phase(1)
        scores_phase(t, 0)
        emit(t - 1, 1)
        pend_ref[0] = values_phase(0)
        scores_phase(t + 1, 1)
        return carry

    lax.fori_loop(0, nq // 2 - 1, pair, 0)
    emit(nq - 2, 0)
    pend_ref[1] = values_phase(1)
    emit(nq - 1, 1)

    def exact_tile(t, carry):
        @pl.when(bad_ref[t] > 0.0)
        def _():
            qq = stacked_q(t)
            m = None
            for c in chunks:
                s = scores_t(qq, c)
                p_ref[0, c[2]:c[2] + c[3], :] = s
                cm = jnp.max(s, axis=0, keepdims=True)
                m = cm if m is None else jnp.maximum(m, cm)
            lv = jnp.zeros((1, 2 * tq), F32)
            for c in chunks:
                p = jnp.exp2(p_ref[0, c[2]:c[2] + c[3], :] - m)
                p_ref[0, c[2]:c[2] + c[3], :] = p
                lv = lv + jnp.sum(p, axis=0, keepdims=True)
            lv_ref[0] = jnp.broadcast_to(lv, (8, 2 * tq))
            rows = rows_of(t)
            o_ref[0, rows, :] = finish(values_phase(0), rows).astype(BF16)
        return carry

    lax.fori_loop(0, nq, exact_tile, 0)


def _attention(q, k, v, g_att, cache, layer, lam_params, subln_w, lam_init, tq, ck):
    b, l, _ = q.shape
    tq = _row_tile(l, tq)
    ck = _row_tile(l, ck)
    nq = l // tq
    assert nq >= 2 and nq % 2 == 0
    has_cache = cache is not None
    past = cache[0].shape[3] if has_cache else 0
    lk = past + l
    seq = pl.BlockSpec((1, l, HEAD_W), lambda bi, h: (bi, 0, h))
    vec = lambda w: pl.BlockSpec((1, 1, w), lambda bi, h: (layer, 0, 0))
    vt_spec = pl.BlockSpec((1, HEAD_W, l), lambda bi, h: (bi, h, 0))
    in_specs = [seq, seq, vt_spec]
    args = [q, k, v]
    if has_cache:
        cspec = pl.BlockSpec((1, 1, 1, past, HEAD_W), lambda bi, h: (bi, layer, h, 0, 0))
        in_specs += [cspec, cspec]
        args += list(cache)
    in_specs += [seq] + [vec(HEAD_DIM)] * 4 + [vec(HEAD_W)]
    args += [g_att] + list(lam_params) + [subln_w]
    return pl.pallas_call(
        functools.partial(_attn_kernel, lam_init=lam_init, has_cache=has_cache, ck=ck,
                          n_new=l // ck, past=past, tq=tq, nq=nq),
        grid=(b, N_ATT_HEADS),
        in_specs=in_specs,
        out_specs=seq,
        out_shape=jax.ShapeDtypeStruct((b, l, ATT_WIDTH), BF16),
        scratch_shapes=[pltpu.VMEM((2, lk, 2 * tq), F32), pltpu.VMEM((2, 8, 2 * tq), F32),
                        pltpu.VMEM((2, tq, HEAD_W), F32), pltpu.SMEM((nq,), F32)]
        + ([pltpu.VMEM((HEAD_W, past), BF16)] if has_cache else []),
        name=f"attn_{l}",
        compiler_params=_cparams("parallel", "parallel"),
    )(*args)


def _fourier_kernel(g_ref, u_ref, gate_ref, o_ref, *, nb):
    g = g_ref[...]
    for j in range(nb):
        y = jnp.dot(g, u_ref[j], preferred_element_type=F32)
        o_ref[j] = (y * _silu(gate_ref[j])).astype(BF16)


def _fourier(gmat, uf, g_f, tr, nb):
    b, l, w = g_f.shape
    assert b % nb == 0
    tr = _row_tile(l, tr)
    u2 = uf.reshape(b, 2 * l, w)
    return pl.pallas_call(
        functools.partial(_fourier_kernel, nb=nb),
        grid=(b // nb, l // tr),
        in_specs=[pl.BlockSpec((tr, 2 * l), lambda bi, i: (i, 0)),
                  pl.BlockSpec((nb, 2 * l, w), lambda bi, i: (bi, 0, 0)),
                  pl.BlockSpec((nb, tr, w), lambda bi, i: (bi, i, 0))],
        out_specs=pl.BlockSpec((nb, tr, w), lambda bi, i: (bi, i, 0)),
        out_shape=jax.ShapeDtypeStruct((b, l, w), BF16),
        name=f"fourier_{l}",
        compiler_params=_cparams("parallel", "arbitrary"),
    )(gmat, u2, g_f)


def _fft_stage1_kernel(u_ref, m_ref, tc_ref, ts_ref, z_ref, *, width):
    m1 = m_ref[...]
    for b in range(GRID_W):
        win = slice(b * width, (b + 1) * width)
        rhs = jnp.concatenate([u_ref[0, 0, :, win], u_ref[0, 1, :, win]], axis=0)
        y = jnp.dot(m1, rhs, preferred_element_type=F32)
        yr, yi = y[:GRID_W], y[GRID_W:]
        tc, ts = tc_ref[b], ts_ref[b]
        for h0 in range(0, width, 128):
            ls = slice(h0, h0 + 128)
            z_ref[0, 0, b, :, ls] = (yr[:, ls] * tc + yi[:, ls] * ts).astype(BF16)
            z_ref[0, 1, b, :, ls] = (yi[:, ls] * tc - yr[:, ls] * ts).astype(BF16)


def _fft_stage2_kernel(z_ref, m_ref, gate_ref, o_ref, *, width):
    m3 = m_ref[...]
    for c in range(GRID_W):
        win = slice(c * width, (c + 1) * width)
        rhs = jnp.concatenate([z_ref[0, 0, :, win], z_ref[0, 1, :, win]], axis=0)
        y = jnp.dot(m3, rhs, preferred_element_type=F32)
        o_ref[0, :, win] = (y * _silu(gate_ref[0, :, win])).astype(BF16)


def _fourier_grid(uf, g_f):
    b, l, w = g_f.shape
    n = GRID_W
    assert l == n * n and w % 128 == 0
    idx = np.arange(n)
    ang = 2.0 * np.pi * ((idx[:, None] * idx[None, :]) % n) / n
    cn, sn = np.cos(ang), np.sin(ang)
    m1 = jnp.asarray(np.block([[cn, -sn], [-sn, -cn]]), BF16)
    m3 = jnp.asarray(np.concatenate([cn, sn], axis=1), BF16)
    tw = 2.0 * np.pi * ((idx[:, None] * idx[None, :]) % l) / l
    rep = lambda t: jnp.asarray(np.broadcast_to((t / n)[:, :, None], (n, n, 128)), F32)
    tc, ts = rep(np.cos(tw)), rep(np.sin(tw))
    const = lambda shp: pl.BlockSpec(shp, lambda bi: (0,) * len(shp))
    z = pl.pallas_call(
        functools.partial(_fft_stage1_kernel, width=w),
        grid=(b,),
        in_specs=[pl.BlockSpec((1, 2, n, n * w), lambda bi: (bi, 0, 0, 0)),
                  const((2 * n, 2 * n)), const((n, n, 128)), const((n, n, 128))],
        out_specs=pl.BlockSpec((1, 2, n, n, w), lambda bi: (bi, 0, 0, 0, 0)),
        out_shape=jax.ShapeDtypeStruct((b, 2, n, n, w), BF16),
        name=f"fourier_stage1_{l}",
        compiler_params=_cparams("parallel"),
    )(uf.reshape(b, 2, n, n * w), m1, tc, ts)
    row = pl.BlockSpec((1, n, n * w), lambda bi: (bi, 0, 0))
    out = pl.pallas_call(
        functools.partial(_fft_stage2_kernel, width=w),
        grid=(b,),
        in_specs=[pl.BlockSpec((1, 2, n, n * w), lambda bi: (bi, 0, 0, 0)), const((n, 2 * n)), row],
        out_specs=row,
        out_shape=jax.ShapeDtypeStruct((b, n, n * w), BF16),
        name=f"fourier_stage2_{l}",
        compiler_params=_cparams("parallel"),
    )(z.reshape(b, 2, n, n * w), m3, g_f.reshape(b, n, n * w))
    return out.reshape(b, l, w)


def _rope_tables(l):
    n_freq = HEAD_DIM // 4
    rows = l // GRID_W
    row = np.repeat(np.arange(rows, dtype=np.float64), GRID_W)
    col = np.tile(np.arange(GRID_W, dtype=np.float64), rows)
    inv = ROPE_THETA ** (-np.arange(n_freq, dtype=np.float64) / n_freq)
    ang_r, ang_c = row[:, None] * inv, col[:, None] * inv
    zeros = np.zeros_like(ang_r)

    def per_head(r_lo, r_hi, c_lo, c_hi):
        one = np.concatenate([r_lo, r_hi, c_lo, c_hi], axis=-1)
        return jnp.asarray(np.concatenate([one, one], axis=-1), F32)

    cr, sr, cc, sc = np.cos(ang_r), np.sin(ang_r), np.cos(ang_c), np.sin(ang_c)
    cos_t = per_head(cr, cr, cc, cc)
    sa_t = per_head(-sr, zeros, -sc, zeros)
    sb_t = per_head(zeros, sr, zeros, sc)
    return cos_t, sa_t, sb_t


def _run_group(x, mod3, per_batch_mod, p, gmat, rope_tabs, cache, emit_kv, alpha, tiles):
    depth = p["w_in"].shape[0]
    kv_new = []
    outs = _boundary(x, None, mod3, per_batch_mod, p, None, 0, rope_tabs, emit_kv, alpha, tiles["tm"])
    for layer in range(depth):
        lam_init = 0.8 - 0.6 * math.exp(-0.3 * layer)
        q, k, vt, g_att, uf, g_f, gm = outs[:7]
        kv_new.append(outs[7:])
        att = _attention(q, k, vt, g_att, cache, layer, p["lam"], p["subln_w"], lam_init,
                         tiles["tq"], tiles["ck"])
        fo = _fourier_grid(uf, g_f) if gmat is None else _fourier(gmat, uf, g_f, tiles["tr"], tiles["nb"])
        nxt = layer + 1 if layer + 1 < depth else None
        res = _boundary(x, (att, fo, gm), mod3, per_batch_mod, p, layer, nxt, rope_tabs, emit_kv,
                        alpha, tiles["tm"])
        x, outs = res[0], res[1:]
    return x, kv_new


def kernel(x_prompt, x_sample, c, cache_k, cache_v, c_ctx, w_ada, b_ada, w_in, w_out,
           lam_q1, lam_k1, lam_q2, lam_k2, subln_w, fourier_w, gmlp_ws, gmlp_bs, ln_g, ln_b):
    depth, d, _ = w_ada.shape
    nb_lat = c.shape[0]
    l_ctx, l_lat = x_prompt.shape[1], x_sample.shape[1]
    assert 1 + nb_lat <= MOD_ROWS
    alpha = (2.0 * depth) ** 0.25

    cvec = jnp.concatenate([c_ctx[None, :], c, jnp.zeros((MOD_ROWS - 1 - nb_lat, d), F32)], axis=0)
    mod = _modulation(cvec, w_ada, b_ada)

    fa, fb = _fourier_channel_maps(fourier_w, 1.0 / math.sqrt(FOURIER_GROUP_DIM))
    params = {
        "w_in": w_in.astype(BF16), "w_out": w_out.astype(BF16), "fa": fa, "fb": fb,
        "lam": tuple(t[:, None, :] for t in (lam_q1, lam_k1, lam_q2, lam_k2)),
        "subln_w": subln_w[:, None, :], "ln_g": ln_g[:, None, :], "ln_b": ln_b[:, None, :],
        "gmlp_w": gmlp_ws.astype(BF16).transpose(0, 2, 1, 3).reshape(depth, CHUNK, N_GMLP_HEADS * CHUNK),
        "gmlp_bias": jnp.repeat(gmlp_bs.astype(F32).transpose(0, 2, 1), GMLP_HEAD_DIM, axis=2),
        "avg": jnp.asarray(np.kron(np.eye(N_GMLP_HEADS), np.full((GMLP_HEAD_DIM,) * 2, 1.0 / GMLP_HEAD_DIM)), BF16),
    }
    ctx_tiles = dict(tm=256, tq=128, ck=256, tr=256, nb=2)
    lat_tiles = dict(tm=512, tq=256, ck=512, tr=512, nb=2)

    xp, kv_new = _run_group(x_prompt, mod[:, 0:1].reshape(depth, 1, 3, d), False, params,
                            _dft_matrix(l_ctx), None, None, True, alpha, ctx_tiles)
    new_k = jnp.stack([kv[0] for kv in kv_new], axis=1)
    new_v = jnp.stack([kv[1] for kv in kv_new], axis=1)

    xs, _ = _run_group(x_sample, mod[:, 1:1 + nb_lat].reshape(depth, nb_lat, 3, d), True, params,
                       None if l_lat == GRID_W * GRID_W else _dft_matrix(l_lat), _rope_tables(l_lat),
                       (cache_k, cache_v), False, alpha,
                       lat_tiles)
    return (xp, xs, new_k, new_v)
```

```python
import functools
import math

import numpy as np
import jax
import jax.numpy as jnp
from jax import lax
from jax.experimental import pallas as pl
from jax.experimental.pallas import tpu as pltpu

F32 = jnp.float32
BF16 = jnp.bfloat16

GRID_W = 64
HEAD_DIM = 64
N_ATT_HEADS = 4
HEAD_W = 2 * HEAD_DIM
ATT_WIDTH = N_ATT_HEADS * HEAD_W
N_FOURIER_GROUPS = 4
FOURIER_GROUP_DIM = 64
FOURIER_WIDTH = N_FOURIER_GROUPS * FOURIER_GROUP_DIM
N_GMLP_HEADS = 4
GMLP_HEAD_DIM = 64
GMLP_WIDTH = N_GMLP_HEADS * GMLP_HEAD_DIM
CHUNK = 128
ROPE_THETA = 10000.0
LN_EPS = 1e-5
RMS_EPS = 1e-5
QK_SCALE = HEAD_DIM ** -0.5
LOG2_E = math.log2(math.e)

_SEG_NAMES = ("q", "k", "v", "g_att", "f", "g_f", "u", "vm", "g_m")
_SEG_WIDTHS = (ATT_WIDTH,) * 4 + (FOURIER_WIDTH,) * 2 + (GMLP_WIDTH,) * 3
_SEG_START = dict(zip(_SEG_NAMES, np.concatenate([[0], np.cumsum(_SEG_WIDTHS)[:-1]]).tolist()))
_SEG_WIDTH = dict(zip(_SEG_NAMES, _SEG_WIDTHS))

VMEM_LIMIT_BYTES = 52 * 1024 * 1024
MOD_ROWS = 16


def _cparams(*sem):
    return pltpu.CompilerParams(dimension_semantics=sem, vmem_limit_bytes=VMEM_LIMIT_BYTES)


def _silu(x):
    return x * jax.nn.sigmoid(x)


def _row_tile(l, want):
    t = min(l, want)
    assert l % t == 0
    return t


def _mod_kernel(c_ref, w_ref, b_ref, o_ref):
    s = _silu(c_ref[...])
    o_ref[0] = jnp.dot(s, w_ref[0], precision=lax.Precision.HIGHEST,
                       preferred_element_type=F32) + b_ref[0]


def _modulation(cvec, w_ada, b_ada):
    depth, d, d3 = w_ada.shape
    tn = 1024
    return pl.pallas_call(
        _mod_kernel,
        grid=(depth, d3 // tn),
        in_specs=[pl.BlockSpec((MOD_ROWS, d), lambda l, j: (0, 0)),
                  pl.BlockSpec((1, d, tn), lambda l, j: (l, 0, j)),
                  pl.BlockSpec((1, 1, tn), lambda l, j: (l, 0, j))],
        out_specs=pl.BlockSpec((1, MOD_ROWS, tn), lambda l, j: (l, 0, j)),
        out_shape=jax.ShapeDtypeStruct((depth, MOD_ROWS, d3), F32),
        name="adaln_mod",
        compiler_params=_cparams("arbitrary", "arbitrary"),
    )(cvec, w_ada, b_ada.reshape(depth, 1, d3))


def _chanmap_kernel(cc_ref, ss_ref, w_ref, a_ref, b_ref, *, scale):
    w = w_ref[0]
    a_ref[0] = (jnp.dot(cc_ref[...], w, precision=lax.Precision.HIGHEST,
                        preferred_element_type=F32) * scale).astype(BF16)
    b_ref[0] = (jnp.dot(ss_ref[...], w, precision=lax.Precision.HIGHEST,
                        preferred_element_type=F32) * scale).astype(BF16)


def _fourier_channel_maps(fourier_w, scale):
    depth = fourier_w.shape[0]
    n = np.arange(FOURIER_GROUP_DIM)
    ang = 2.0 * np.pi * ((n[:, None] * n[None, :]) % FOURIER_GROUP_DIM) / FOURIER_GROUP_DIM
    eye = np.eye(N_FOURIER_GROUPS)
    cc = jnp.asarray(np.kron(eye, np.cos(ang)), F32)
    ss = jnp.asarray(np.kron(eye, np.sin(ang)), F32)
    wbd = jnp.einsum("gh,lgce->lgche", jnp.eye(N_FOURIER_GROUPS, dtype=F32),
                     fourier_w.astype(F32)).reshape(depth, FOURIER_WIDTH, FOURIER_WIDTH)
    w_spec = pl.BlockSpec((1, FOURIER_WIDTH, FOURIER_WIDTH), lambda l: (l, 0, 0))
    c_spec = pl.BlockSpec((FOURIER_WIDTH, FOURIER_WIDTH), lambda l: (0, 0))
    shp = jax.ShapeDtypeStruct((depth, FOURIER_WIDTH, FOURIER_WIDTH), BF16)
    return pl.pallas_call(
        functools.partial(_chanmap_kernel, scale=scale),
        grid=(depth,),
        in_specs=[c_spec, c_spec, w_spec],
        out_specs=[w_spec, w_spec],
        out_shape=[shp, shp],
        name="fourier_chanmap",
        compiler_params=_cparams("arbitrary"),
    )(cc, ss, wbd)


def _dft_kernel(ca_ref, sa_ref, cb_ref, sb_ref, g_ref, *, l):
    ca, sa = ca_ref[0], sa_ref[0]
    cb, sb = cb_ref[...], sb_ref[...]
    norm = 1.0 / math.sqrt(l)
    g_ref[:, 0:l] = ((ca * cb - sa * sb) * norm).astype(BF16)
    g_ref[:, l:2 * l] = ((sa * cb + ca * sb) * -norm).astype(BF16)


def _dft_matrix(l):
    assert l % (2 * GRID_W) == 0
    na = l // GRID_W
    k = np.concatenate([np.arange(0, l, 2), np.arange(1, l, 2)]).astype(np.int64)
    ang_a = ((np.arange(na // 2)[:, None] * k[None, :]) % na) * (2.0 * np.pi / na)
    ang_b = ((np.arange(GRID_W)[:, None] * k[None, :]) % l) * (2.0 * np.pi / l)
    ca = jnp.asarray(np.cos(ang_a).reshape(na // 2, 1, l), F32)
    sa = jnp.asarray(np.sin(ang_a).reshape(na // 2, 1, l), F32)
    cb, sb = jnp.asarray(np.cos(ang_b), F32), jnp.asarray(np.sin(ang_b), F32)
    a_spec = pl.BlockSpec((1, 1, l), lambda i: (i, 0, 0))
    b_spec = pl.BlockSpec((GRID_W, l), lambda i: (0, 0))
    return pl.pallas_call(
        functools.partial(_dft_kernel, l=l),
        grid=(na // 2,),
        in_specs=[a_spec, a_spec, b_spec, b_spec],
        out_specs=pl.BlockSpec((GRID_W, 2 * l), lambda i: (i, 0)),
        out_shape=jax.ShapeDtypeStruct((l // 2, 2 * l), BF16),
        name=f"dft_matrix_{l}",
        compiler_params=_cparams("arbitrary"),
    )(ca, sa, cb, sb)


def _layernorm(v):
    mu = jnp.mean(v, axis=-1, keepdims=True)
    vc = v - mu
    return vc * lax.rsqrt(jnp.mean(vc * vc, axis=-1, keepdims=True) + LN_EPS)


def _boundary_kernel(*refs, has_out, has_in, rope, emit_kv, alpha, n_chunks):
    refs = list(refs)
    take = lambda n: [refs.pop(0) for _ in range(n)]
    (x_ref,) = take(1)
    if has_out:
        att_ref, fo_ref, gmo_ref, modo_ref, wout_ref, lg_ref, lb_ref = take(7)
    if has_in:
        modi_ref, win_ref, fa_ref, fb_ref, gw_ref, gbias_ref, avg_ref, sel_ref = take(8)
        if rope:
            cos_ref, sa_ref, sb_ref = take(3)
    if has_out:
        (xo_ref,) = take(1)
    if has_in:
        q_ref, k_ref, vt_ref, ga_ref, uf_ref, gf_ref, gm_ref = take(7)
        if emit_kv:
            kf_ref, vf_ref = take(2)

    x = x_ref[0]
    if has_out:
        a0, a1 = ATT_WIDTH, ATT_WIDTH + FOURIER_WIDTH
        y = (jnp.dot(att_ref[0], wout_ref[0, 0:a0, :], preferred_element_type=F32)
             + jnp.dot(fo_ref[0], wout_ref[0, a0:a1, :], preferred_element_type=F32)
             + jnp.dot(gmo_ref[0], wout_ref[0, a1:, :], preferred_element_type=F32))
        x = _layernorm(alpha * x + modo_ref[0, 0, 2:3, :] * y) * lg_ref[0] + lb_ref[0]
        xo_ref[0] = x
    if not has_in:
        return

    shift, scale = modi_ref[0, 0, 0:1, :], modi_ref[0, 0, 1:2, :]
    h = (_layernorm(x) * (1.0 + scale) + shift).astype(BF16)

    def proj(name):
        c0 = _SEG_START[name]
        return jnp.dot(h, win_ref[0, :, c0:c0 + _SEG_WIDTH[name]], preferred_element_type=F32)

    def rotate(z):
        return (z * cos_ref[...] + pltpu.roll(z, HEAD_W - 16, 1) * sa_ref[...]
                + pltpu.roll(z, 16, 1) * sb_ref[...])

    zq, zk, zv = proj("q"), proj("k"), proj("v")
    for hh in range(N_ATT_HEADS):
        sl = slice(hh * HEAD_W, (hh + 1) * HEAD_W)
        qh, kh = zq[:, sl], zk[:, sl]
        if emit_kv:
            kf_ref[0, hh] = kh
            vf_ref[0, hh] = zv[:, sl]
        if rope:
            qh, kh = rotate(qh), rotate(kh)
        q_ref[0, :, sl] = (qh * (QK_SCALE * LOG2_E)).astype(BF16)
        k_ref[0, :, sl] = kh.astype(BF16)
    vt_ref[0] = zv.T.astype(BF16)
    ga_ref[0] = proj("g_att")
    f = proj("f").astype(BF16)
    half = sel_ref.shape[0] // 2
    for j, fm_ref in enumerate((fa_ref, fb_ref)):
        u = jnp.dot(f, fm_ref[0], preferred_element_type=F32).astype(BF16)
        u = jnp.dot(sel_ref[...], u, preferred_element_type=F32).astype(BF16)
        uf_ref[0, j, 0] = u[:half]
        uf_ref[0, j, 1] = u[half:]
    gf_ref[0] = proj("g_f")

    zu, zvm, zgm = proj("u"), proj("vm"), proj("g_m")
    avg = avg_ref[...]

    def group_mean(v):
        hi = v.astype(BF16)
        lo = (v - hi.astype(F32)).astype(BF16)
        return (jnp.dot(hi, avg, preferred_element_type=F32)
                + jnp.dot(lo, avg, preferred_element_type=F32))

    lane = lax.broadcasted_iota(jnp.int32, (CHUNK, GMLP_WIDTH), 1)
    head_mask = [(lane >= hh * GMLP_HEAD_DIM) & (lane < (hh + 1) * GMLP_HEAD_DIM)
                 for hh in range(N_GMLP_HEADS)]
    for c in range(n_chunks):
        rows = slice(c * CHUNK, (c + 1) * CHUNK)
        v = zvm[rows]
        vc = v - group_mean(v)
        vn = (vc * lax.rsqrt(group_mean(vc * vc) + LN_EPS)).astype(BF16)
        stacked = jnp.concatenate(
            [jnp.where(head_mask[hh], vn, jnp.zeros_like(vn)) for hh in range(N_GMLP_HEADS)], axis=0)
        s = jnp.dot(gw_ref[0], stacked, preferred_element_type=F32) + gbias_ref[0]
        gm_ref[0, rows, :] = (zu[rows] * s * _silu(zgm[rows])).astype(BF16)


def _boundary(x, branches, mod3, per_batch_mod, p, out_layer, in_layer, rope_tabs, emit_kv, alpha, tm):
    b, l, d = x.shape
    tm = _row_tile(l, tm)
    has_out, has_in = out_layer is not None, in_layer is not None
    row = lambda w: pl.BlockSpec((1, tm, w), lambda bi, i: (bi, i, 0))
    const2 = lambda shp: pl.BlockSpec(shp, lambda bi, i: (0, 0))
    per_layer = lambda shp, layer: pl.BlockSpec((1,) + shp, lambda bi, i: (layer, 0, 0))
    mod_spec = lambda layer: pl.BlockSpec(
        (1, 1, 3, d), (lambda bi, i: (layer, bi, 0, 0)) if per_batch_mod else (lambda bi, i: (layer, 0, 0, 0)))
    sds = jax.ShapeDtypeStruct
    in_specs, args, out_specs, out_shape = [row(d)], [x], [], []
    if has_out:
        in_specs += [row(ATT_WIDTH), row(FOURIER_WIDTH), row(GMLP_WIDTH), mod_spec(out_layer),
                     per_layer((d, d), out_layer), per_layer((1, d), out_layer), per_layer((1, d), out_layer)]
        args += list(branches) + [mod3, p["w_out"], p["ln_g"], p["ln_b"]]
        out_specs += [row(d)]
        out_shape += [sds((b, l, d), F32)]
    if has_in:
        n_in = p["w_in"].shape[2]
        in_specs += [mod_spec(in_layer), per_layer((d, n_in), in_layer),
                     per_layer((FOURIER_WIDTH, FOURIER_WIDTH), in_layer),
                     per_layer((FOURIER_WIDTH, FOURIER_WIDTH), in_layer),
                     per_layer((CHUNK, N_GMLP_HEADS * CHUNK), in_layer),
                     per_layer((CHUNK, GMLP_WIDTH), in_layer), const2((GMLP_WIDTH, GMLP_WIDTH)),
                     const2((tm, tm))]
        sel = np.zeros((tm, tm), np.float32)
        sel[np.arange(tm), np.concatenate([np.arange(0, tm, 2), np.arange(1, tm, 2)])] = 1.0
        args += [mod3, p["w_in"], p["fa"], p["fb"], p["gmlp_w"], p["gmlp_bias"], p["avg"],
                 jnp.asarray(sel, BF16)]
        if rope_tabs is not None:
            in_specs += [pl.BlockSpec((tm, HEAD_W), lambda bi, i: (i, 0))] * 3
            args += list(rope_tabs)
        out_shape += [sds((b, l, ATT_WIDTH), BF16)] * 2 + [
            sds((b, ATT_WIDTH, l), BF16), sds((b, l, ATT_WIDTH), F32),
            sds((b, 2, 2, l // 2, FOURIER_WIDTH), BF16), sds((b, l, FOURIER_WIDTH), F32),
            sds((b, l, GMLP_WIDTH), BF16)]
        out_specs += [row(ATT_WIDTH)] * 2 + [
            pl.BlockSpec((1, ATT_WIDTH, tm), lambda bi, i: (bi, 0, i)), row(ATT_WIDTH),
            pl.BlockSpec((1, 2, 2, tm // 2, FOURIER_WIDTH), lambda bi, i: (bi, 0, 0, i, 0)),
            row(FOURIER_WIDTH), row(GMLP_WIDTH)]
        if emit_kv:
            kv = sds((b, N_ATT_HEADS, l, HEAD_W), F32)
            out_shape += [kv, kv]
            out_specs += [pl.BlockSpec((1, N_ATT_HEADS, tm, HEAD_W), lambda bi, i: (bi, 0, i, 0))] * 2
    kern = functools.partial(_boundary_kernel, has_out=has_out, has_in=has_in,
                             rope=rope_tabs is not None, emit_kv=emit_kv and has_in, alpha=alpha,
                             n_chunks=tm // CHUNK)

    return pl.pallas_call(
        kern, grid=(b, l // tm), in_specs=in_specs, out_specs=out_specs, out_shape=out_shape,
        name=f"boundary_{l}_{int(has_out)}{int(has_in)}",
        compiler_params=_cparams("parallel", "parallel"),
    )(*args)


def _attn_kernel(*refs, lam_init, has_cache, ck, n_new, past, tq, nq):
    if has_cache:
        (q_ref, k_ref, vt_ref, pk_ref, pv_ref, g_ref, lq1_ref, lk1_ref, lq2_ref, lk2_ref,
         sw_ref, o_ref, p_ref, lv_ref, pend_ref, bad_ref, pvt_ref) = refs
    else:
        (q_ref, k_ref, vt_ref, g_ref, lq1_ref, lk1_ref, lq2_ref, lk2_ref,
         sw_ref, o_ref, p_ref, lv_ref, pend_ref, bad_ref) = refs

    chunks = []
    if has_cache:
        chunks.append((True, 0, 0, past))
    for j in range(n_new):
        chunks.append((False, j * ck, past + j * ck, ck))

    if has_cache:
        pvt_ref[...] = pv_ref[0, 0, 0].T.astype(BF16)

    def keys(c):
        return pk_ref[0, 0, 0].astype(BF16) if c[0] else k_ref[0, c[1]:c[1] + c[3], :]

    def vals_t(c):
        return pvt_ref[...] if c[0] else vt_ref[0, :, c[1]:c[1] + c[3]]

    def scores_t(qq, c):
        return lax.dot_general(keys(c), qq, (((1,), (1,)), ((), ())), preferred_element_type=F32)

    lam = (jnp.exp(jnp.sum(lq1_ref[0] * lk1_ref[0], axis=-1, keepdims=True))
           - jnp.exp(jnp.sum(lq2_ref[0] * lk2_ref[0], axis=-1, keepdims=True)) + lam_init)

    def rows_of(t):
        return pl.ds(pl.multiple_of(t * tq, tq), tq)

    def stacked_q(t):
        q = q_ref[0, rows_of(t), :]
        lane = lax.broadcasted_iota(jnp.int32, q.shape, 1)
        zero = jnp.zeros_like(q)
        return jnp.concatenate([jnp.where(lane < HEAD_DIM, q, zero),
                                jnp.where(lane >= HEAD_DIM, q, zero)], axis=0)

    def scores_phase(t, slot):
        qq = stacked_q(t)
        s0 = scores_t(qq, chunks[0])
        stab = jnp.max(s0, axis=0, keepdims=True)
        lv = jnp.zeros((1, 2 * tq), F32)
        for ci, c in enumerate(chunks):
            s = s0 if ci == 0 else scores_t(qq, c)
            p = jnp.exp2(s - stab)
            p_ref[slot, c[2]:c[2] + c[3], :] = p
            lv = lv + jnp.sum(p, axis=0, keepdims=True)
        lv_ref[slot] = jnp.broadcast_to(lv, (8, 2 * tq))

    def values_phase(slot):
        lsum = lv_ref[slot][0:1, :]
        l1, l2 = lsum[:, :tq], lsum[:, tq:]
        rho = lam * l1 / l2
        acc = jnp.zeros((HEAD_W, tq), F32)
        for c in chunks:
            a = (p_ref[slot, c[2]:c[2] + c[3], 0:tq]
                 - rho * p_ref[slot, c[2]:c[2] + c[3], tq:2 * tq]).astype(BF16)
            acc = acc + jnp.dot(vals_t(c), a, preferred_element_type=F32)
        return (acc / l1).T

    def finish(o, rows):
        o = (o * lax.rsqrt(jnp.mean(o * o, axis=-1, keepdims=True) + RMS_EPS)
             * sw_ref[0] * (1.0 - lam_init))
        return o * _silu(g_ref[0, rows, :])

    def emit(t, slot):
        rows = rows_of(t)
        out = finish(pend_ref[slot], rows)
        o_ref[0, rows, :] = out.astype(BF16)
        bad_ref[t] = jnp.sum(jnp.where(jnp.isfinite(out), 0.0, 1.0))

    scores_phase(0, 0)
    pend_ref[0] = values_phase(0)
    scores_phase(1, 1)

    def pair(i, carry):
        t = 2 * i + 2
        emit(t - 2, 0)
        pend_ref[1] = values_phase(1)
        scores_phase(t, 0)
        emit(t - 1, 1)
        pend_ref[0] = values_phase(0)
        scores_phase(t + 1, 1)
        return carry

    lax.fori_loop(0, nq // 2 - 1, pair, 0)
    emit(nq - 2, 0)
    pend_ref[1] = values_phase(1)
    emit(nq - 1, 1)

    def exact_tile(t, carry):
        @pl.when(bad_ref[t] > 0.0)
        def _():
            qq = stacked_q(t)
            m = None
            for c in chunks:
                s = scores_t(qq, c)
                p_ref[0, c[2]:c[2] + c[3], :] = s
                cm = jnp.max(s, axis=0, keepdims=True)
                m = cm if m is None else jnp.maximum(m, cm)
            lv = jnp.zeros((1, 2 * tq), F32)
            for c in chunks:
                p = jnp.exp2(p_ref[0, c[2]:c[2] + c[3], :] - m)
                p_ref[0, c[2]:c[2] + c[3], :] = p
                lv = lv + jnp.sum(p, axis=0, keepdims=True)
            lv_ref[0] = jnp.broadcast_to(lv, (8, 2 * tq))
            rows = rows_of(t)
            o_ref[0, rows, :] = finish(values_phase(0), rows).astype(BF16)
        return carry

    lax.fori_loop(0, nq, exact_tile, 0)


def _attention(q, k, v, g_att, cache, layer, lam_params, subln_w, lam_init, tq, ck):
    b, l, _ = q.shape
    tq = _row_tile(l, tq)
    ck = _row_tile(l, ck)
    nq = l // tq
    assert nq >= 2 and nq % 2 == 0
    has_cache = cache is not None
    past = cache[0].shape[3] if has_cache else 0
    lk = past + l
    seq = pl.BlockSpec((1, l, HEAD_W), lambda bi, h: (bi, 0, h))
    vec = lambda w: pl.BlockSpec((1, 1, w), lambda bi, h: (layer, 0, 0))
    vt_spec = pl.BlockSpec((1, HEAD_W, l), lambda bi, h: (bi, h, 0))
    in_specs = [seq, seq, vt_spec]
    args = [q, k, v]
    if has_cache:
        cspec = pl.BlockSpec((1, 1, 1, past, HEAD_W), lambda bi, h: (bi, layer, h, 0, 0))
        in_specs += [cspec, cspec]
        args += list(cache)
    in_specs += [seq] + [vec(HEAD_DIM)] * 4 + [vec(HEAD_W)]
    args += [g_att] + list(lam_params) + [subln_w]
    return pl.pallas_call(
        functools.partial(_attn_kernel, lam_init=lam_init, has_cache=has_cache, ck=ck,
                          n_new=l // ck, past=past, tq=tq, nq=nq),
        grid=(b, N_ATT_HEADS),
        in_specs=in_specs,
        out_specs=seq,
        out_shape=jax.ShapeDtypeStruct((b, l, ATT_WIDTH), BF16),
        scratch_shapes=[pltpu.VMEM((2, lk, 2 * tq), F32), pltpu.VMEM((2, 8, 2 * tq), F32),
                        pltpu.VMEM((2, tq, HEAD_W), F32), pltpu.SMEM((nq,), F32)]
        + ([pltpu.VMEM((HEAD_W, past), BF16)] if has_cache else []),
        name=f"attn_{l}",
        compiler_params=_cparams("parallel", "parallel"),
    )(*args)


def _fourier_kernel(g_ref, u_ref, gate_ref, o_ref, *, nb, l):
    h = l // 2
    dot = functools.partial(jnp.dot, preferred_element_type=F32)
    for j in range(nb):
        even = dot(g_ref[:, 0:h], u_ref[j, 0, 0]) + dot(g_ref[:, l:l + h], u_ref[j, 1, 0])
        odd = dot(g_ref[:, h:l], u_ref[j, 0, 1]) + dot(g_ref[:, l + h:2 * l], u_ref[j, 1, 1])
        o_ref[j, 0] = ((even + odd) * _silu(gate_ref[j, 0])).astype(BF16)
        o_ref[j, 1] = ((even - odd) * _silu(gate_ref[j, 1])).astype(BF16)


def _fourier(gmat, uf, g_f, tr, nb):
    b, l, w = g_f.shape
    assert b % nb == 0
    h = l // 2
    tr = _row_tile(h, tr)
    halves = pl.BlockSpec((nb, 2, tr, w), lambda bi, i: (bi, 0, i, 0))
    out = pl.pallas_call(
        functools.partial(_fourier_kernel, nb=nb, l=l),
        grid=(b // nb, h // tr),
        in_specs=[pl.BlockSpec((tr, 2 * l), lambda bi, i: (i, 0)),
                  pl.BlockSpec((nb, 2, 2, h, w), lambda bi, i: (bi, 0, 0, 0, 0)),
                  halves],
        out_specs=halves,
        out_shape=jax.ShapeDtypeStruct((b, 2, h, w), BF16),
        name=f"fourier_{l}",
        compiler_params=_cparams("parallel", "arbitrary"),
    )(gmat, uf, g_f.reshape(b, 2, h, w))
    return out.reshape(b, l, w)


def _rope_tables(l):
    n_freq = HEAD_DIM // 4
    rows = l // GRID_W
    row = np.repeat(np.arange(rows, dtype=np.float64), GRID_W)
    col = np.tile(np.arange(GRID_W, dtype=np.float64), rows)
    inv = ROPE_THETA ** (-np.arange(n_freq, dtype=np.float64) / n_freq)
    ang_r, ang_c = row[:, None] * inv, col[:, None] * inv
    zeros = np.zeros_like(ang_r)

    def per_head(r_lo, r_hi, c_lo, c_hi):
        one = np.concatenate([r_lo, r_hi, c_lo, c_hi], axis=-1)
        return jnp.asarray(np.concatenate([one, one], axis=-1), F32)

    cr, sr, cc, sc = np.cos(ang_r), np.sin(ang_r), np.cos(ang_c), np.sin(ang_c)
    cos_t = per_head(cr, cr, cc, cc)
    sa_t = per_head(-sr, zeros, -sc, zeros)
    sb_t = per_head(zeros, sr, zeros, sc)
    return cos_t, sa_t, sb_t


def _run_group(x, mod3, per_batch_mod, p, gmat, rope_tabs, cache, emit_kv, alpha, tiles):
    depth = p["w_in"].shape[0]
    kv_new = []
    outs = _boundary(x, None, mod3, per_batch_mod, p, None, 0, rope_tabs, emit_kv, alpha, tiles["tm"])
    for layer in range(depth):
        lam_init = 0.8 - 0.6 * math.exp(-0.3 * layer)
        q, k, vt, g_att, uf, g_f, gm = outs[:7]
        kv_new.append(outs[7:])
        att = _attention(q, k, vt, g_att, cache, layer, p["lam"], p["subln_w"], lam_init,
                         tiles["tq"], tiles["ck"])
        fo = _fourier(gmat, uf, g_f, tiles["tr"], tiles["nb"])
        nxt = layer + 1 if layer + 1 < depth else None
        res = _boundary(x, (att, fo, gm), mod3, per_batch_mod, p, layer, nxt, rope_tabs, emit_kv,
                        alpha, tiles["tm"])
        x, outs = res[0], res[1:]
    return x, kv_new


def kernel(x_prompt, x_sample, c, cache_k, cache_v, c_ctx, w_ada, b_ada, w_in, w_out,
           lam_q1, lam_k1, lam_q2, lam_k2, subln_w, fourier_w, gmlp_ws, gmlp_bs, ln_g, ln_b):
    depth, d, _ = w_ada.shape
    nb_lat = c.shape[0]
    l_ctx, l_lat = x_prompt.shape[1], x_sample.shape[1]
    assert 1 + nb_lat <= MOD_ROWS
    alpha = (2.0 * depth) ** 0.25

    cvec = jnp.concatenate([c_ctx[None, :], c, jnp.zeros((MOD_ROWS - 1 - nb_lat, d), F32)], axis=0)
    mod = _modulation(cvec, w_ada, b_ada)

    fa, fb = _fourier_channel_maps(fourier_w, 1.0 / math.sqrt(FOURIER_GROUP_DIM))
    params = {
        "w_in": w_in.astype(BF16), "w_out": w_out.astype(BF16), "fa": fa, "fb": fb,
        "lam": tuple(t[:, None, :] for t in (lam_q1, lam_k1, lam_q2, lam_k2)),
        "subln_w": subln_w[:, None, :], "ln_g": ln_g[:, None, :], "ln_b": ln_b[:, None, :],
        "gmlp_w": gmlp_ws.astype(BF16).transpose(0, 2, 1, 3).reshape(depth, CHUNK, N_GMLP_HEADS * CHUNK),
        "gmlp_bias": jnp.repeat(gmlp_bs.astype(F32).transpose(0, 2, 1), GMLP_HEAD_DIM, axis=2),
        "avg": jnp.asarray(np.kron(np.eye(N_GMLP_HEADS), np.full((GMLP_HEAD_DIM,) * 2, 1.0 / GMLP_HEAD_DIM)), BF16),
    }
    ctx_tiles = dict(tm=256, tq=128, ck=256, tr=256, nb=2)
    lat_tiles = dict(tm=512, tq=256, ck=512, tr=512, nb=2)

    xp, kv_new = _run_group(x_prompt, mod[:, 0:1].reshape(depth, 1, 3, d), False, params,
                            _dft_matrix(l_ctx), None, None, True, alpha, ctx_tiles)
    new_k = jnp.stack([kv[0] for kv in kv_new], axis=1)
    new_v = jnp.stack([kv[1] for kv in kv_new], axis=1)

    xs, _ = _run_group(x_sample, mod[:, 1:1 + nb_lat].reshape(depth, nb_lat, 3, d), True, params,
                       _dft_matrix(l_lat), _rope_tables(l_lat), (cache_k, cache_v), False, alpha,
                       lat_tiles)
    return (xp, xs, new_k, new_v)
```

```python
import functools
import math

import numpy as np
import jax
import jax.numpy as jnp
from jax import lax
from jax.experimental import pallas as pl
from jax.experimental.pallas import tpu as pltpu

F32 = jnp.float32
BF16 = jnp.bfloat16

GRID_W = 64
HEAD_DIM = 64
N_ATT_HEADS = 4
HEAD_W = 2 * HEAD_DIM
ATT_WIDTH = N_ATT_HEADS * HEAD_W
N_FOURIER_GROUPS = 4
FOURIER_GROUP_DIM = 64
FOURIER_WIDTH = N_FOURIER_GROUPS * FOURIER_GROUP_DIM
N_GMLP_HEADS = 4
GMLP_HEAD_DIM = 64
GMLP_WIDTH = N_GMLP_HEADS * GMLP_HEAD_DIM
CHUNK = 128
ROPE_THETA = 10000.0
LN_EPS = 1e-5
RMS_EPS = 1e-5
QK_SCALE = HEAD_DIM ** -0.5
LOG2_E = math.log2(math.e)

_SEG_NAMES = ("q", "k", "v", "g_att", "f", "g_f", "u", "vm", "g_m")
_SEG_WIDTHS = (ATT_WIDTH,) * 4 + (FOURIER_WIDTH,) * 2 + (GMLP_WIDTH,) * 3
_SEG_START = dict(zip(_SEG_NAMES, np.concatenate([[0], np.cumsum(_SEG_WIDTHS)[:-1]]).tolist()))
_SEG_WIDTH = dict(zip(_SEG_NAMES, _SEG_WIDTHS))

VMEM_LIMIT_BYTES = 52 * 1024 * 1024
MOD_ROWS = 16


def _cparams(*sem):
    return pltpu.CompilerParams(dimension_semantics=sem, vmem_limit_bytes=VMEM_LIMIT_BYTES)


def _silu(x):
    return x * jax.nn.sigmoid(x)


def _row_tile(l, want):
    t = min(l, want)
    assert l % t == 0
    return t


def _mod_kernel(c_ref, w_ref, b_ref, o_ref):
    s = _silu(c_ref[...])
    o_ref[0] = jnp.dot(s, w_ref[0], precision=lax.Precision.HIGHEST,
                       preferred_element_type=F32) + b_ref[0]


def _modulation(cvec, w_ada, b_ada):
    depth, d, d3 = w_ada.shape
    tn = 1024
    return pl.pallas_call(
        _mod_kernel,
        grid=(depth, d3 // tn),
        in_specs=[pl.BlockSpec((MOD_ROWS, d), lambda l, j: (0, 0)),
                  pl.BlockSpec((1, d, tn), lambda l, j: (l, 0, j)),
                  pl.BlockSpec((1, 1, tn), lambda l, j: (l, 0, j))],
        out_specs=pl.BlockSpec((1, MOD_ROWS, tn), lambda l, j: (l, 0, j)),
        out_shape=jax.ShapeDtypeStruct((depth, MOD_ROWS, d3), F32),
        name="adaln_mod",
        compiler_params=_cparams("arbitrary", "arbitrary"),
    )(cvec, w_ada, b_ada.reshape(depth, 1, d3))


def _chanmap_kernel(cc_ref, ss_ref, w_ref, a_ref, b_ref, *, scale):
    w = w_ref[0]
    a_ref[0] = (jnp.dot(cc_ref[...], w, precision=lax.Precision.HIGHEST,
                        preferred_element_type=F32) * scale).astype(BF16)
    b_ref[0] = (jnp.dot(ss_ref[...], w, precision=lax.Precision.HIGHEST,
                        preferred_element_type=F32) * scale).astype(BF16)


def _fourier_channel_maps(fourier_w, scale):
    depth = fourier_w.shape[0]
    n = np.arange(FOURIER_GROUP_DIM)
    ang = 2.0 * np.pi * ((n[:, None] * n[None, :]) % FOURIER_GROUP_DIM) / FOURIER_GROUP_DIM
    eye = np.eye(N_FOURIER_GROUPS)
    cc = jnp.asarray(np.kron(eye, np.cos(ang)), F32)
    ss = jnp.asarray(np.kron(eye, np.sin(ang)), F32)
    wbd = jnp.einsum("gh,lgce->lgche", jnp.eye(N_FOURIER_GROUPS, dtype=F32),
                     fourier_w.astype(F32)).reshape(depth, FOURIER_WIDTH, FOURIER_WIDTH)
    w_spec = pl.BlockSpec((1, FOURIER_WIDTH, FOURIER_WIDTH), lambda l: (l, 0, 0))
    c_spec = pl.BlockSpec((FOURIER_WIDTH, FOURIER_WIDTH), lambda l: (0, 0))
    shp = jax.ShapeDtypeStruct((depth, FOURIER_WIDTH, FOURIER_WIDTH), BF16)
    return pl.pallas_call(
        functools.partial(_chanmap_kernel, scale=scale),
        grid=(depth,),
        in_specs=[c_spec, c_spec, w_spec],
        out_specs=[w_spec, w_spec],
        out_shape=[shp, shp],
        name="fourier_chanmap",
        compiler_params=_cparams("arbitrary"),
    )(cc, ss, wbd)


def _dft_kernel(ca_ref, sa_ref, cb_ref, sb_ref, g_ref, *, l):
    ca, sa = ca_ref[0], sa_ref[0]
    cb, sb = cb_ref[...], sb_ref[...]
    norm = 1.0 / math.sqrt(l)
    g_ref[:, 0:l] = ((ca * cb - sa * sb) * norm).astype(BF16)
    g_ref[:, l:2 * l] = ((sa * cb + ca * sb) * -norm).astype(BF16)


def _dft_matrix(l):
    assert l % (2 * GRID_W) == 0
    na = l // GRID_W
    k = np.concatenate([np.arange(0, l, 2), np.arange(1, l, 2)]).astype(np.int64)
    ang_a = ((np.arange(na // 2)[:, None] * k[None, :]) % na) * (2.0 * np.pi / na)
    ang_b = ((np.arange(GRID_W)[:, None] * k[None, :]) % l) * (2.0 * np.pi / l)
    ca = jnp.asarray(np.cos(ang_a).reshape(na // 2, 1, l), F32)
    sa = jnp.asarray(np.sin(ang_a).reshape(na // 2, 1, l), F32)
    cb, sb = jnp.asarray(np.cos(ang_b), F32), jnp.asarray(np.sin(ang_b), F32)
    a_spec = pl.BlockSpec((1, 1, l), lambda i: (i, 0, 0))
    b_spec = pl.BlockSpec((GRID_W, l), lambda i: (0, 0))
    return pl.pallas_call(
        functools.partial(_dft_kernel, l=l),
        grid=(na // 2,),
        in_specs=[a_spec, a_spec, b_spec, b_spec],
        out_specs=pl.BlockSpec((GRID_W, 2 * l), lambda i: (i, 0)),
        out_shape=jax.ShapeDtypeStruct((l // 2, 2 * l), BF16),
        name=f"dft_matrix_{l}",
        compiler_params=_cparams("arbitrary"),
    )(ca, sa, cb, sb)


def _layernorm(v):
    mu = jnp.mean(v, axis=-1, keepdims=True)
    vc = v - mu
    return vc * lax.rsqrt(jnp.mean(vc * vc, axis=-1, keepdims=True) + LN_EPS)


def _boundary_kernel(*refs, has_out, has_in, rope, emit_kv, alpha, n_chunks):
    refs = list(refs)
    take = lambda n: [refs.pop(0) for _ in range(n)]
    (x_ref,) = take(1)
    if has_out:
        att_ref, fo_ref, gmo_ref, modo_ref, wout_ref, lg_ref, lb_ref = take(7)
    if has_in:
        modi_ref, win_ref, fa_ref, fb_ref, gw_ref, gbias_ref, avg_ref, sel_ref = take(8)
        if rope:
            cos_ref, sa_ref, sb_ref = take(3)
    if has_out:
        (xo_ref,) = take(1)
    if has_in:
        q_ref, k_ref, vt_ref, ga_ref, uf_ref, gf_ref, gm_ref = take(7)
        if emit_kv:
            kf_ref, vf_ref = take(2)

    x = x_ref[0]
    if has_out:
        a0, a1 = ATT_WIDTH, ATT_WIDTH + FOURIER_WIDTH
        y = (jnp.dot(att_ref[0], wout_ref[0, 0:a0, :], preferred_element_type=F32)
             + jnp.dot(fo_ref[0], wout_ref[0, a0:a1, :], preferred_element_type=F32)
             + jnp.dot(gmo_ref[0], wout_ref[0, a1:, :], preferred_element_type=F32))
        x = _layernorm(alpha * x + modo_ref[0, 0, 2:3, :] * y) * lg_ref[0] + lb_ref[0]
        xo_ref[0] = x
    if not has_in:
        return

    shift, scale = modi_ref[0, 0, 0:1, :], modi_ref[0, 0, 1:2, :]
    h = (_layernorm(x) * (1.0 + scale) + shift).astype(BF16)

    def proj(name):
        c0 = _SEG_START[name]
        return jnp.dot(h, win_ref[0, :, c0:c0 + _SEG_WIDTH[name]], preferred_element_type=F32)

    def rotate(z):
        return (z * cos_ref[...] + pltpu.roll(z, HEAD_W - 16, 1) * sa_ref[...]
                + pltpu.roll(z, 16, 1) * sb_ref[...])

    zq, zk, zv = proj("q"), proj("k"), proj("v")
    for hh in range(N_ATT_HEADS):
        sl = slice(hh * HEAD_W, (hh + 1) * HEAD_W)
        qh, kh = zq[:, sl], zk[:, sl]
        if emit_kv:
            kf_ref[0, hh] = kh
            vf_ref[0, hh] = zv[:, sl]
        if rope:
            qh, kh = rotate(qh), rotate(kh)
        q_ref[0, :, sl] = (qh * (QK_SCALE * LOG2_E)).astype(BF16)
        k_ref[0, :, sl] = kh.astype(BF16)
    vt_ref[0] = zv.T.astype(BF16)
    ga_ref[0] = proj("g_att")
    f = proj("f").astype(BF16)
    half = sel_ref.shape[0] // 2
    for j, fm_ref in enumerate((fa_ref, fb_ref)):
        u = jnp.dot(f, fm_ref[0], preferred_element_type=F32).astype(BF16)
        u = jnp.dot(sel_ref[...], u, preferred_element_type=F32).astype(BF16)
        uf_ref[0, j, 0] = u[:half]
        uf_ref[0, j, 1] = u[half:]
    gf_ref[0] = proj("g_f")

    zu, zvm, zgm = proj("u"), proj("vm"), proj("g_m")
    avg = avg_ref[...]

    def group_mean(v):
        hi = v.astype(BF16)
        lo = (v - hi.astype(F32)).astype(BF16)
        return (jnp.dot(hi, avg, preferred_element_type=F32)
                + jnp.dot(lo, avg, preferred_element_type=F32))

    lane = lax.broadcasted_iota(jnp.int32, (CHUNK, GMLP_WIDTH), 1)
    head_mask = [(lane >= hh * GMLP_HEAD_DIM) & (lane < (hh + 1) * GMLP_HEAD_DIM)
                 for hh in range(N_GMLP_HEADS)]
    vc_all = zvm - group_mean(zvm)
    vn_all = (vc_all * lax.rsqrt(group_mean(vc_all * vc_all) + LN_EPS)).astype(BF16)
    for c in range(n_chunks):
        rows = slice(c * CHUNK, (c + 1) * CHUNK)
        vn = vn_all[rows]
        stacked = jnp.concatenate(
            [jnp.where(head_mask[hh], vn, jnp.zeros_like(vn)) for hh in range(N_GMLP_HEADS)], axis=0)
        s = jnp.dot(gw_ref[0], stacked, preferred_element_type=F32) + gbias_ref[0]
        gm_ref[0, rows, :] = (zu[rows] * s * _silu(zgm[rows])).astype(BF16)


def _boundary(x, branches, mod3, per_batch_mod, p, out_layer, in_layer, rope_tabs, emit_kv, alpha, tm):
    b, l, d = x.shape
    tm = _row_tile(l, tm)
    has_out, has_in = out_layer is not None, in_layer is not None
    row = lambda w: pl.BlockSpec((1, tm, w), lambda bi, i: (bi, i, 0))
    const2 = lambda shp: pl.BlockSpec(shp, lambda bi, i: (0, 0))
    per_layer = lambda shp, layer: pl.BlockSpec((1,) + shp, lambda bi, i: (layer, 0, 0))
    mod_spec = lambda layer: pl.BlockSpec(
        (1, 1, 3, d), (lambda bi, i: (layer, bi, 0, 0)) if per_batch_mod else (lambda bi, i: (layer, 0, 0, 0)))
    sds = jax.ShapeDtypeStruct
    in_specs, args, out_specs, out_shape = [row(d)], [x], [], []
    if has_out:
        in_specs += [row(ATT_WIDTH), row(FOURIER_WIDTH), row(GMLP_WIDTH), mod_spec(out_layer),
                     per_layer((d, d), out_layer), per_layer((1, d), out_layer), per_layer((1, d), out_layer)]
        args += list(branches) + [mod3, p["w_out"], p["ln_g"], p["ln_b"]]
        out_specs += [row(d)]
        out_shape += [sds((b, l, d), F32)]
    if has_in:
        n_in = p["w_in"].shape[2]
        in_specs += [mod_spec(in_layer), per_layer((d, n_in), in_layer),
                     per_layer((FOURIER_WIDTH, FOURIER_WIDTH), in_layer),
                     per_layer((FOURIER_WIDTH, FOURIER_WIDTH), in_layer),
                     per_layer((CHUNK, N_GMLP_HEADS * CHUNK), in_layer),
                     per_layer((CHUNK, GMLP_WIDTH), in_layer), const2((GMLP_WIDTH, GMLP_WIDTH)),
                     const2((tm, tm))]
        sel = np.zeros((tm, tm), np.float32)
        sel[np.arange(tm), np.concatenate([np.arange(0, tm, 2), np.arange(1, tm, 2)])] = 1.0
        args += [mod3, p["w_in"], p["fa"], p["fb"], p["gmlp_w"], p["gmlp_bias"], p["avg"],
                 jnp.asarray(sel, BF16)]
        if rope_tabs is not None:
            in_specs += [pl.BlockSpec((tm, HEAD_W), lambda bi, i: (i, 0))] * 3
            args += list(rope_tabs)
        out_shape += [sds((b, l, ATT_WIDTH), BF16)] * 2 + [
            sds((b, ATT_WIDTH, l), BF16), sds((b, l, ATT_WIDTH), F32),
            sds((b, 2, 2, l // 2, FOURIER_WIDTH), BF16), sds((b, l, FOURIER_WIDTH), F32),
            sds((b, l, GMLP_WIDTH), BF16)]
        out_specs += [row(ATT_WIDTH)] * 2 + [
            pl.BlockSpec((1, ATT_WIDTH, tm), lambda bi, i: (bi, 0, i)), row(ATT_WIDTH),
            pl.BlockSpec((1, 2, 2, tm // 2, FOURIER_WIDTH), lambda bi, i: (bi, 0, 0, i, 0)),
            row(FOURIER_WIDTH), row(GMLP_WIDTH)]
        if emit_kv:
            kv = sds((b, N_ATT_HEADS, l, HEAD_W), F32)
            out_shape += [kv, kv]
            out_specs += [pl.BlockSpec((1, N_ATT_HEADS, tm, HEAD_W), lambda bi, i: (bi, 0, i, 0))] * 2
    kern = functools.partial(_boundary_kernel, has_out=has_out, has_in=has_in,
                             rope=rope_tabs is not None, emit_kv=emit_kv and has_in, alpha=alpha,
                             n_chunks=tm // CHUNK)

    return pl.pallas_call(
        kern, grid=(b, l // tm), in_specs=in_specs, out_specs=out_specs, out_shape=out_shape,
        name=f"boundary_{l}_{int(has_out)}{int(has_in)}",
        compiler_params=_cparams("parallel", "parallel"),
    )(*args)


def _attn_kernel(*refs, lam_init, has_cache, ck, n_new, past, tq, nq):
    if has_cache:
        (q_ref, k_ref, vt_ref, pk_ref, pv_ref, g_ref, lq1_ref, lk1_ref, lq2_ref, lk2_ref,
         sw_ref, o_ref, p_ref, lv_ref, pend_ref, bad_ref, pvt_ref) = refs
    else:
        (q_ref, k_ref, vt_ref, g_ref, lq1_ref, lk1_ref, lq2_ref, lk2_ref,
         sw_ref, o_ref, p_ref, lv_ref, pend_ref, bad_ref) = refs

    chunks = []
    if has_cache:
        chunks.append((True, 0, 0, past))
    for j in range(n_new):
        chunks.append((False, j * ck, past + j * ck, ck))

    if has_cache:
        pvt_ref[...] = pv_ref[0, 0, 0].T.astype(BF16)

    def keys(c):
        return pk_ref[0, 0, 0].astype(BF16) if c[0] else k_ref[0, c[1]:c[1] + c[3], :]

    def vals_t(c):
        return pvt_ref[...] if c[0] else vt_ref[0, :, c[1]:c[1] + c[3]]

    def scores_t(qq, c):
        return lax.dot_general(keys(c), qq, (((1,), (1,)), ((), ())), preferred_element_type=F32)

    lam = (jnp.exp(jnp.sum(lq1_ref[0] * lk1_ref[0], axis=-1, keepdims=True))
           - jnp.exp(jnp.sum(lq2_ref[0] * lk2_ref[0], axis=-1, keepdims=True)) + lam_init)

    def rows_of(t):
        return pl.ds(pl.multiple_of(t * tq, tq), tq)

    def stacked_q(t):
        q = q_ref[0, rows_of(t), :]
        lane = lax.broadcasted_iota(jnp.int32, q.shape, 1)
        zero = jnp.zeros_like(q)
        return jnp.concatenate([jnp.where(lane < HEAD_DIM, q, zero),
                                jnp.where(lane >= HEAD_DIM, q, zero)], axis=0)

    def scores_phase(t, slot):
        qq = stacked_q(t)
        s0 = scores_t(qq, chunks[0])
        stab = jnp.max(s0, axis=0, keepdims=True)
        lv = jnp.zeros((1, 2 * tq), F32)
        for ci, c in enumerate(chunks):
            s = s0 if ci == 0 else scores_t(qq, c)
            p = jnp.exp2(s - stab)
            p_ref[slot, c[2]:c[2] + c[3], :] = p
            lv = lv + jnp.sum(p, axis=0, keepdims=True)
        lv_ref[slot] = jnp.broadcast_to(lv, (8, 2 * tq))

    def values_phase(slot):
        lsum = lv_ref[slot][0:1, :]
        l1, l2 = lsum[:, :tq], lsum[:, tq:]
        rho = lam * l1 / l2
        acc = jnp.zeros((HEAD_W, tq), F32)
        for c in chunks:
            a = (p_ref[slot, c[2]:c[2] + c[3], 0:tq]
                 - rho * p_ref[slot, c[2]:c[2] + c[3], tq:2 * tq]).astype(BF16)
            acc = acc + jnp.dot(vals_t(c), a, preferred_element_type=F32)
        return (acc / l1).T

    def finish(o, rows):
        o = (o * lax.rsqrt(jnp.mean(o * o, axis=-1, keepdims=True) + RMS_EPS)
             * sw_ref[0] * (1.0 - lam_init))
        return o * _silu(g_ref[0, rows, :])

    def emit(t, slot):
        rows = rows_of(t)
        out = finish(pend_ref[slot], rows)
        o_ref[0, rows, :] = out.astype(BF16)
        bad_ref[t] = jnp.sum(jnp.where(jnp.isfinite(out), 0.0, 1.0))

    scores_phase(0, 0)
    pend_ref[0] = values_phase(0)
    scores_phase(1, 1)

    def pair(i, carry):
        t = 2 * i + 2
        emit(t - 2, 0)
        pend_ref[1] = values_phase(1)
        scores_phase(t, 0)
        emit(t - 1, 1)
        pend_ref[0] = values_phase(0)
        scores_phase(t + 1, 1)
        return carry

    lax.fori_loop(0, nq // 2 - 1, pair, 0)
    emit(nq - 2, 0)
    pend_ref[1] = values_phase(1)
    emit(nq - 1, 1)

    def exact_tile(t, carry):
        @pl.when(bad_ref[t] > 0.0)
        def _():
            qq = stacked_q(t)
            m = None
            for c in chunks:
                s = scores_t(qq, c)
                p_ref[0, c[2]:c[2] + c[3], :] = s
                cm = jnp.max(s, axis=0, keepdims=True)
                m = cm if m is None else jnp.maximum(m, cm)
            lv = jnp.zeros((1, 2 * tq), F32)
            for c in chunks:
                p = jnp.exp2(p_ref[0, c[2]:c[2] + c[3], :] - m)
                p_ref[0, c[2]:c[2] + c[3], :] = p
                lv = lv + jnp.sum(p, axis=0, keepdims=True)
            lv_ref[0] = jnp.broadcast_to(lv, (8, 2 * tq))
            rows = rows_of(t)
            o_ref[0, rows, :] = finish(values_phase(0), rows).astype(BF16)
        return carry

    lax.fori_loop(0, nq, exact_tile, 0)


def _attention(q, k, v, g_att, cache, layer, lam_params, subln_w, lam_init, tq, ck):
    b, l, _ = q.shape
    tq = _row_tile(l, tq)
    ck = _row_tile(l, ck)
    nq = l // tq
    assert nq >= 2 and nq % 2 == 0
    has_cache = cache is not None
    past = cache[0].shape[3] if has_cache else 0
    lk = past + l
    seq = pl.BlockSpec((1, l, HEAD_W), lambda bi, h: (bi, 0, h))
    vec = lambda w: pl.BlockSpec((1, 1, w), lambda bi, h: (layer, 0, 0))
    vt_spec = pl.BlockSpec((1, HEAD_W, l), lambda bi, h: (bi, h, 0))
    in_specs = [seq, seq, vt_spec]
    args = [q, k, v]
    if has_cache:
        cspec = pl.BlockSpec((1, 1, 1, past, HEAD_W), lambda bi, h: (bi, layer, h, 0, 0))
        in_specs += [cspec, cspec]
        args += list(cache)
    in_specs += [seq] + [vec(HEAD_DIM)] * 4 + [vec(HEAD_W)]
    args += [g_att] + list(lam_params) + [subln_w]
    return pl.pallas_call(
        functools.partial(_attn_kernel, lam_init=lam_init, has_cache=has_cache, ck=ck,
                          n_new=l // ck, past=past, tq=tq, nq=nq),
        grid=(b, N_ATT_HEADS),
        in_specs=in_specs,
        out_specs=seq,
        out_shape=jax.ShapeDtypeStruct((b, l, ATT_WIDTH), BF16),
        scratch_shapes=[pltpu.VMEM((2, lk, 2 * tq), F32), pltpu.VMEM((2, 8, 2 * tq), F32),
                        pltpu.VMEM((2, tq, HEAD_W), F32), pltpu.SMEM((nq,), F32)]
        + ([pltpu.VMEM((HEAD_W, past), BF16)] if has_cache else []),
        name=f"attn_{l}",
        compiler_params=_cparams("parallel", "parallel"),
    )(*args)


def _fourier_kernel(g_ref, u_ref, gate_ref, o_ref, *, nb, l):
    h = l // 2
    dot = functools.partial(jnp.dot, preferred_element_type=F32)
    for j in range(nb):
        even = dot(g_ref[:, 0:h], u_ref[j, 0, 0]) + dot(g_ref[:, l:l + h], u_ref[j, 1, 0])
        odd = dot(g_ref[:, h:l], u_ref[j, 0, 1]) + dot(g_ref[:, l + h:2 * l], u_ref[j, 1, 1])
        o_ref[j, 0] = ((even + odd) * _silu(gate_ref[j, 0])).astype(BF16)
        o_ref[j, 1] = ((even - odd) * _silu(gate_ref[j, 1])).astype(BF16)


def _fourier(gmat, uf, g_f, tr, nb):
    b, l, w = g_f.shape
    assert b % nb == 0
    h = l // 2
    tr = _row_tile(h, tr)
    halves = pl.BlockSpec((nb, 2, tr, w), lambda bi, i: (bi, 0, i, 0))
    out = pl.pallas_call(
        functools.partial(_fourier_kernel, nb=nb, l=l),
        grid=(b // nb, h // tr),
        in_specs=[pl.BlockSpec((tr, 2 * l), lambda bi, i: (i, 0)),
                  pl.BlockSpec((nb, 2, 2, h, w), lambda bi, i: (bi, 0, 0, 0, 0)),
                  halves],
        out_specs=halves,
        out_shape=jax.ShapeDtypeStruct((b, 2, h, w), BF16),
        name=f"fourier_{l}",
        compiler_params=_cparams("parallel", "arbitrary"),
    )(gmat, uf, g_f.reshape(b, 2, h, w))
    return out.reshape(b, l, w)


def _rope_tables(l):
    n_freq = HEAD_DIM // 4
    rows = l // GRID_W
    row = np.repeat(np.arange(rows, dtype=np.float64), GRID_W)
    col = np.tile(np.arange(GRID_W, dtype=np.float64), rows)
    inv = ROPE_THETA ** (-np.arange(n_freq, dtype=np.float64) / n_freq)
    ang_r, ang_c = row[:, None] * inv, col[:, None] * inv
    zeros = np.zeros_like(ang_r)

    def per_head(r_lo, r_hi, c_lo, c_hi):
        one = np.concatenate([r_lo, r_hi, c_lo, c_hi], axis=-1)
        return jnp.asarray(np.concatenate([one, one], axis=-1), F32)

    cr, sr, cc, sc = np.cos(ang_r), np.sin(ang_r), np.cos(ang_c), np.sin(ang_c)
    cos_t = per_head(cr, cr, cc, cc)
    sa_t = per_head(-sr, zeros, -sc, zeros)
    sb_t = per_head(zeros, sr, zeros, sc)
    return cos_t, sa_t, sb_t


def _run_group(x, mod3, per_batch_mod, p, gmat, rope_tabs, cache, emit_kv, alpha, tiles):
    depth = p["w_in"].shape[0]
    kv_new = []
    outs = _boundary(x, None, mod3, per_batch_mod, p, None, 0, rope_tabs, emit_kv, alpha, tiles["tm"])
    for layer in range(depth):
        lam_init = 0.8 - 0.6 * math.exp(-0.3 * layer)
        q, k, vt, g_att, uf, g_f, gm = outs[:7]
        kv_new.append(outs[7:])
        att = _attention(q, k, vt, g_att, cache, layer, p["lam"], p["subln_w"], lam_init,
                         tiles["tq"], tiles["ck"])
        fo = _fourier(gmat, uf, g_f, tiles["tr"], tiles["nb"])
        nxt = layer + 1 if layer + 1 < depth else None
        res = _boundary(x, (att, fo, gm), mod3, per_batch_mod, p, layer, nxt, rope_tabs, emit_kv,
                        alpha, tiles["tm"])
        x, outs = res[0], res[1:]
    return x, kv_new


def kernel(x_prompt, x_sample, c, cache_k, cache_v, c_ctx, w_ada, b_ada, w_in, w_out,
           lam_q1, lam_k1, lam_q2, lam_k2, subln_w, fourier_w, gmlp_ws, gmlp_bs, ln_g, ln_b):
    depth, d, _ = w_ada.shape
    nb_lat = c.shape[0]
    l_ctx, l_lat = x_prompt.shape[1], x_sample.shape[1]
    assert 1 + nb_lat <= MOD_ROWS
    alpha = (2.0 * depth) ** 0.25

    cvec = jnp.concatenate([c_ctx[None, :], c, jnp.zeros((MOD_ROWS - 1 - nb_lat, d), F32)], axis=0)
    mod = _modulation(cvec, w_ada, b_ada)

    fa, fb = _fourier_channel_maps(fourier_w, 1.0 / math.sqrt(FOURIER_GROUP_DIM))
    params = {
        "w_in": w_in.astype(BF16), "w_out": w_out.astype(BF16), "fa": fa, "fb": fb,
        "lam": tuple(t[:, None, :] for t in (lam_q1, lam_k1, lam_q2, lam_k2)),
        "subln_w": subln_w[:, None, :], "ln_g": ln_g[:, None, :], "ln_b": ln_b[:, None, :],
        "gmlp_w": gmlp_ws.astype(BF16).transpose(0, 2, 1, 3).reshape(depth, CHUNK, N_GMLP_HEADS * CHUNK),
        "gmlp_bias": jnp.repeat(gmlp_bs.astype(F32).transpose(0, 2, 1), GMLP_HEAD_DIM, axis=2),
        "avg": jnp.asarray(np.kron(np.eye(N_GMLP_HEADS), np.full((GMLP_HEAD_DIM,) * 2, 1.0 / GMLP_HEAD_DIM)), BF16),
    }
    ctx_tiles = dict(tm=256, tq=128, ck=256, tr=256, nb=2)
    lat_tiles = dict(tm=512, tq=256, ck=512, tr=512, nb=2)

    xp, kv_new = _run_group(x_prompt, mod[:, 0:1].reshape(depth, 1, 3, d), False, params,
                            _dft_matrix(l_ctx), None, None, True, alpha, ctx_tiles)
    new_k = jnp.stack([kv[0] for kv in kv_new], axis=1)
    new_v = jnp.stack([kv[1] for kv in kv_new], axis=1)

    xs, _ = _run_group(x_sample, mod[:, 1:1 + nb_lat].reshape(depth, nb_lat, 3, d), True, params,
                       _dft_matrix(l_lat), _rope_tables(l_lat), (cache_k, cache_v), False, alpha,
                       lat_tiles)
    return (xp, xs, new_k, new_v)
```

```python
import functools
import math

import numpy as np
import jax
import jax.numpy as jnp
from jax import lax
from jax.experimental import pallas as pl
from jax.experimental.pallas import tpu as pltpu

F32 = jnp.float32
BF16 = jnp.bfloat16

GRID_W = 64
HEAD_DIM = 64
N_ATT_HEADS = 4
HEAD_W = 2 * HEAD_DIM
ATT_WIDTH = N_ATT_HEADS * HEAD_W
N_FOURIER_GROUPS = 4
FOURIER_GROUP_DIM = 64
FOURIER_WIDTH = N_FOURIER_GROUPS * FOURIER_GROUP_DIM
N_GMLP_HEADS = 4
GMLP_HEAD_DIM = 64
GMLP_WIDTH = N_GMLP_HEADS * GMLP_HEAD_DIM
CHUNK = 128
ROPE_THETA = 10000.0
LN_EPS = 1e-5
RMS_EPS = 1e-5
QK_SCALE = HEAD_DIM ** -0.5
LOG2_E = math.log2(math.e)

_SEG_NAMES = ("q", "k", "v", "g_att", "f", "g_f", "u", "vm", "g_m")
_SEG_WIDTHS = (ATT_WIDTH,) * 4 + (FOURIER_WIDTH,) * 2 + (GMLP_WIDTH,) * 3
_SEG_START = dict(zip(_SEG_NAMES, np.concatenate([[0], np.cumsum(_SEG_WIDTHS)[:-1]]).tolist()))
_SEG_WIDTH = dict(zip(_SEG_NAMES, _SEG_WIDTHS))

VMEM_LIMIT_BYTES = 52 * 1024 * 1024
MOD_ROWS = 16


def _cparams(*sem):
    return pltpu.CompilerParams(dimension_semantics=sem, vmem_limit_bytes=VMEM_LIMIT_BYTES)


def _silu(x):
    return x * jax.nn.sigmoid(x)


def _row_tile(l, want):
    t = min(l, want)
    assert l % t == 0
    return t


def _mod_kernel(c_ref, w_ref, b_ref, o_ref):
    s = _silu(c_ref[...])
    o_ref[0] = jnp.dot(s, w_ref[0], precision=lax.Precision.HIGHEST,
                       preferred_element_type=F32) + b_ref[0]


def _modulation(cvec, w_ada, b_ada):
    depth, d, d3 = w_ada.shape
    tn = 1024
    return pl.pallas_call(
        _mod_kernel,
        grid=(depth, d3 // tn),
        in_specs=[pl.BlockSpec((MOD_ROWS, d), lambda l, j: (0, 0)),
                  pl.BlockSpec((1, d, tn), lambda l, j: (l, 0, j)),
                  pl.BlockSpec((1, 1, tn), lambda l, j: (l, 0, j))],
        out_specs=pl.BlockSpec((1, MOD_ROWS, tn), lambda l, j: (l, 0, j)),
        out_shape=jax.ShapeDtypeStruct((depth, MOD_ROWS, d3), F32),
        name="adaln_mod",
        compiler_params=_cparams("arbitrary", "arbitrary"),
    )(cvec, w_ada, b_ada.reshape(depth, 1, d3))


def _chanmap_kernel(cc_ref, ss_ref, w_ref, a_ref, b_ref, *, scale):
    w = w_ref[0]
    a_ref[0] = (jnp.dot(cc_ref[...], w, precision=lax.Precision.HIGHEST,
                        preferred_element_type=F32) * scale).astype(BF16)
    b_ref[0] = (jnp.dot(ss_ref[...], w, precision=lax.Precision.HIGHEST,
                        preferred_element_type=F32) * scale).astype(BF16)


def _fourier_channel_maps(fourier_w, scale):
    depth = fourier_w.shape[0]
    n = np.arange(FOURIER_GROUP_DIM)
    ang = 2.0 * np.pi * ((n[:, None] * n[None, :]) % FOURIER_GROUP_DIM) / FOURIER_GROUP_DIM
    eye = np.eye(N_FOURIER_GROUPS)
    cc = jnp.asarray(np.kron(eye, np.cos(ang)), F32)
    ss = jnp.asarray(np.kron(eye, np.sin(ang)), F32)
    wbd = jnp.einsum("gh,lgce->lgche", jnp.eye(N_FOURIER_GROUPS, dtype=F32),
                     fourier_w.astype(F32)).reshape(depth, FOURIER_WIDTH, FOURIER_WIDTH)
    w_spec = pl.BlockSpec((1, FOURIER_WIDTH, FOURIER_WIDTH), lambda l: (l, 0, 0))
    c_spec = pl.BlockSpec((FOURIER_WIDTH, FOURIER_WIDTH), lambda l: (0, 0))
    shp = jax.ShapeDtypeStruct((depth, FOURIER_WIDTH, FOURIER_WIDTH), BF16)
    return pl.pallas_call(
        functools.partial(_chanmap_kernel, scale=scale),
        grid=(depth,),
        in_specs=[c_spec, c_spec, w_spec],
        out_specs=[w_spec, w_spec],
        out_shape=[shp, shp],
        name="fourier_chanmap",
        compiler_params=_cparams("arbitrary"),
    )(cc, ss, wbd)


def _dft_kernel(ca_ref, sa_ref, cb_ref, sb_ref, g_ref, *, l):
    ca, sa = ca_ref[0], sa_ref[0]
    cb, sb = cb_ref[...], sb_ref[...]
    norm = 1.0 / math.sqrt(l)
    g_ref[:, 0:l] = ((ca * cb - sa * sb) * norm).astype(BF16)
    g_ref[:, l:2 * l] = ((sa * cb + ca * sb) * -norm).astype(BF16)


def _dft_matrix(l):
    assert l % (2 * GRID_W) == 0
    na = l // GRID_W
    k = np.concatenate([np.arange(0, l, 2), np.arange(1, l, 2)]).astype(np.int64)
    ang_a = ((np.arange(na // 2)[:, None] * k[None, :]) % na) * (2.0 * np.pi / na)
    ang_b = ((np.arange(GRID_W)[:, None] * k[None, :]) % l) * (2.0 * np.pi / l)
    ca = jnp.asarray(np.cos(ang_a).reshape(na // 2, 1, l), F32)
    sa = jnp.asarray(np.sin(ang_a).reshape(na // 2, 1, l), F32)
    cb, sb = jnp.asarray(np.cos(ang_b), F32), jnp.asarray(np.sin(ang_b), F32)
    a_spec = pl.BlockSpec((1, 1, l), lambda i: (i, 0, 0))
    b_spec = pl.BlockSpec((GRID_W, l), lambda i: (0, 0))
    return pl.pallas_call(
        functools.partial(_dft_kernel, l=l),
        grid=(na // 2,),
        in_specs=[a_spec, a_spec, b_spec, b_spec],
        out_specs=pl.BlockSpec((GRID_W, 2 * l), lambda i: (i, 0)),
        out_shape=jax.ShapeDtypeStruct((l // 2, 2 * l), BF16),
        name=f"dft_matrix_{l}",
        compiler_params=_cparams("arbitrary"),
    )(ca, sa, cb, sb)


def _layernorm(v, eps=LN_EPS):
    mu = jnp.mean(v, axis=-1, keepdims=True)
    vc = v - mu
    return vc * lax.rsqrt(jnp.mean(vc * vc, axis=-1, keepdims=True) + eps)


def _boundary_kernel(*refs, has_out, has_in, rope, emit_kv, alpha, n_chunks):
    refs = list(refs)
    take = lambda n: [refs.pop(0) for _ in range(n)]
    (x_ref,) = take(1)
    if has_out:
        att_ref, fo_ref, gmo_ref, modo_ref, wout_ref, lg_ref, lb_ref = take(7)
    if has_in:
        modi_ref, win_ref, fa_ref, fb_ref, gw_ref, gbias_ref, avg_ref, sel_ref = take(8)
        if rope:
            cos_ref, sa_ref, sb_ref = take(3)
        if emit_kv:
            take(2)
    if has_out:
        (xo_ref,) = take(1)
    if has_in:
        q_ref, k_ref, vt_ref, ga_ref, uf_ref, gf_ref, gm_ref = take(7)
        if emit_kv:
            kf_ref, vf_ref = take(2)

    x = x_ref[0]
    if has_out:
        a0, a1 = ATT_WIDTH, ATT_WIDTH + FOURIER_WIDTH
        y = (jnp.dot(att_ref[0], wout_ref[0, 0:a0, :], preferred_element_type=F32)
             + jnp.dot(fo_ref[0], wout_ref[0, a0:a1, :], preferred_element_type=F32)
             + jnp.dot(gmo_ref[0], wout_ref[0, a1:, :], preferred_element_type=F32))
        gate = modo_ref[0, 0, 2:3, :] * (1.0 / alpha)
        x = _layernorm(x + gate * y, LN_EPS / (alpha * alpha)) * lg_ref[0] + lb_ref[0]
        xo_ref[0] = x
    if not has_in:
        return

    shift, scale = modi_ref[0, 0, 0:1, :], modi_ref[0, 0, 1:2, :]
    h = (_layernorm(x) * (1.0 + scale) + shift).astype(BF16)

    def proj(name):
        c0 = _SEG_START[name]
        return jnp.dot(h, win_ref[0, :, c0:c0 + _SEG_WIDTH[name]], preferred_element_type=F32)

    def rotate(z):
        return (z * cos_ref[...] + pltpu.roll(z, HEAD_W - 16, 1) * sa_ref[...]
                + pltpu.roll(z, 16, 1) * sb_ref[...])

    zq, zk, zv = proj("q"), proj("k"), proj("v")
    for hh in range(N_ATT_HEADS):
        sl = slice(hh * HEAD_W, (hh + 1) * HEAD_W)
        qh, kh = zq[:, sl], zk[:, sl]
        if emit_kv:
            kf_ref[0, 0, hh] = kh
            vf_ref[0, 0, hh] = zv[:, sl]
        if rope:
            qh, kh = rotate(qh), rotate(kh)
        q_ref[0, :, sl] = (qh * (QK_SCALE * LOG2_E)).astype(BF16)
        k_ref[0, :, sl] = kh.astype(BF16)
    vt_ref[0] = zv.T.astype(BF16)
    ga_ref[0] = proj("g_att")
    f = jnp.dot(sel_ref[...], proj("f").astype(BF16), preferred_element_type=F32).astype(BF16)
    half = sel_ref.shape[0] // 2
    for j, fm_ref in enumerate((fa_ref, fb_ref)):
        u = jnp.dot(f, fm_ref[0], preferred_element_type=F32).astype(BF16)
        uf_ref[0, j, 0] = u[:half]
        uf_ref[0, j, 1] = u[half:]
    gf_ref[0] = proj("g_f")

    zu, zvm, zgm = proj("u"), proj("vm"), proj("g_m")
    avg = avg_ref[...]

    def group_mean(v):
        hi = v.astype(BF16)
        lo = (v - hi.astype(F32)).astype(BF16)
        return (jnp.dot(hi, avg, preferred_element_type=F32)
                + jnp.dot(lo, avg, preferred_element_type=F32))

    lane = lax.broadcasted_iota(jnp.int32, (CHUNK, GMLP_WIDTH), 1)
    head_mask = [(lane >= hh * GMLP_HEAD_DIM) & (lane < (hh + 1) * GMLP_HEAD_DIM)
                 for hh in range(N_GMLP_HEADS)]
    vc_all = zvm - group_mean(zvm)
    vn_all = (vc_all * lax.rsqrt(group_mean(vc_all * vc_all) + LN_EPS)).astype(BF16)
    for c in range(n_chunks):
        rows = slice(c * CHUNK, (c + 1) * CHUNK)
        vn = vn_all[rows]
        stacked = jnp.concatenate(
            [jnp.where(head_mask[hh], vn, jnp.zeros_like(vn)) for hh in range(N_GMLP_HEADS)], axis=0)
        s = jnp.dot(gw_ref[0], stacked, preferred_element_type=F32) + gbias_ref[0]
        gm_ref[0, rows, :] = (zu[rows] * s * _silu(zgm[rows])).astype(BF16)


def _boundary(x, branches, mod3, per_batch_mod, p, out_layer, in_layer, rope_tabs, kv_buf, alpha, tm):
    b, l, d = x.shape
    tm = _row_tile(l, tm)
    has_out, has_in = out_layer is not None, in_layer is not None
    emit_kv, aliases = kv_buf is not None and has_in, {}
    row = lambda w: pl.BlockSpec((1, tm, w), lambda bi, i: (bi, i, 0))
    const2 = lambda shp: pl.BlockSpec(shp, lambda bi, i: (0, 0))
    per_layer = lambda shp, layer: pl.BlockSpec((1,) + shp, lambda bi, i: (layer, 0, 0))
    mod_spec = lambda layer: pl.BlockSpec(
        (1, 1, 3, d), (lambda bi, i: (layer, bi, 0, 0)) if per_batch_mod else (lambda bi, i: (layer, 0, 0, 0)))
    sds = jax.ShapeDtypeStruct
    in_specs, args, out_specs, out_shape = [row(d)], [x], [], []
    if has_out:
        in_specs += [row(ATT_WIDTH), row(FOURIER_WIDTH), row(GMLP_WIDTH), mod_spec(out_layer),
                     per_layer((d, d), out_layer), per_layer((1, d), out_layer), per_layer((1, d), out_layer)]
        args += list(branches) + [mod3, p["w_out"], p["ln_g"], p["ln_b"]]
        out_specs += [row(d)]
        out_shape += [sds((b, l, d), F32)]
    if has_in:
        n_in = p["w_in"].shape[2]
        in_specs += [mod_spec(in_layer), per_layer((d, n_in), in_layer),
                     per_layer((FOURIER_WIDTH, FOURIER_WIDTH), in_layer),
                     per_layer((FOURIER_WIDTH, FOURIER_WIDTH), in_layer),
                     per_layer((CHUNK, N_GMLP_HEADS * CHUNK), in_layer),
                     per_layer((CHUNK, GMLP_WIDTH), in_layer), const2((GMLP_WIDTH, GMLP_WIDTH)),
                     const2((tm, tm))]
        sel = np.zeros((tm, tm), np.float32)
        sel[np.arange(tm), np.concatenate([np.arange(0, tm, 2), np.arange(1, tm, 2)])] = 1.0
        args += [mod3, p["w_in"], p["fa"], p["fb"], p["gmlp_w"], p["gmlp_bias"], p["avg"],
                 jnp.asarray(sel, BF16)]
        if rope_tabs is not None:
            in_specs += [pl.BlockSpec((tm, HEAD_W), lambda bi, i: (i, 0))] * 3
            args += list(rope_tabs)
        out_shape += [sds((b, l, ATT_WIDTH), BF16)] * 2 + [
            sds((b, ATT_WIDTH, l), BF16), sds((b, l, ATT_WIDTH), F32),
            sds((b, 2, 2, l // 2, FOURIER_WIDTH), BF16), sds((b, l, FOURIER_WIDTH), F32),
            sds((b, l, GMLP_WIDTH), BF16)]
        out_specs += [row(ATT_WIDTH)] * 2 + [
            pl.BlockSpec((1, ATT_WIDTH, tm), lambda bi, i: (bi, 0, i)), row(ATT_WIDTH),
            pl.BlockSpec((1, 2, 2, tm // 2, FOURIER_WIDTH), lambda bi, i: (bi, 0, 0, i, 0)),
            row(FOURIER_WIDTH), row(GMLP_WIDTH)]
        if emit_kv:
            aliases = {len(args): len(out_shape), len(args) + 1: len(out_shape) + 1}
            in_specs += [pl.BlockSpec(memory_space=pl.ANY)] * 2
            args += list(kv_buf)
            out_shape += [sds(kv_buf[0].shape, F32)] * 2
            out_specs += [pl.BlockSpec((1, 1, N_ATT_HEADS, tm, HEAD_W),
                                       lambda bi, i: (bi, in_layer, 0, i, 0))] * 2
    kern = functools.partial(_boundary_kernel, has_out=has_out, has_in=has_in,
                             rope=rope_tabs is not None, emit_kv=emit_kv, alpha=alpha,
                             n_chunks=tm // CHUNK)

    return pl.pallas_call(
        kern, grid=(b, l // tm), in_specs=in_specs, out_specs=out_specs, out_shape=out_shape,
        input_output_aliases=aliases,
        name=f"boundary_{l}_{int(has_out)}{int(has_in)}",
        compiler_params=_cparams("parallel", "parallel"),
    )(*args)


def _attn_kernel(*refs, lam_init, has_cache, ck, n_new, past, tq, nq):
    if has_cache:
        (q_ref, k_ref, vt_ref, pk_ref, pv_ref, g_ref, lq1_ref, lk1_ref, lq2_ref, lk2_ref,
         sw_ref, o_ref, p_ref, lv_ref, pend_ref, bad_ref, pvt_ref) = refs
    else:
        (q_ref, k_ref, vt_ref, g_ref, lq1_ref, lk1_ref, lq2_ref, lk2_ref,
         sw_ref, o_ref, p_ref, lv_ref, pend_ref, bad_ref) = refs

    chunks = []
    if has_cache:
        chunks.append((True, 0, 0, past))
    for j in range(n_new):
        chunks.append((False, j * ck, past + j * ck, ck))

    if has_cache:
        pvt_ref[...] = pv_ref[0, 0, 0].T.astype(BF16)

    def keys(c):
        return pk_ref[0, 0, 0].astype(BF16) if c[0] else k_ref[0, c[1]:c[1] + c[3], :]

    def vals_t(c):
        return pvt_ref[...] if c[0] else vt_ref[0, :, c[1]:c[1] + c[3]]

    def scores_t(qq, c):
        return lax.dot_general(keys(c), qq, (((1,), (1,)), ((), ())), preferred_element_type=F32)

    lam = (jnp.exp(jnp.sum(lq1_ref[0] * lk1_ref[0], axis=-1, keepdims=True))
           - jnp.exp(jnp.sum(lq2_ref[0] * lk2_ref[0], axis=-1, keepdims=True)) + lam_init)

    def rows_of(t):
        return pl.ds(pl.multiple_of(t * tq, tq), tq)

    def stacked_q(t):
        q = q_ref[0, rows_of(t), :]
        lane = lax.broadcasted_iota(jnp.int32, q.shape, 1)
        zero = jnp.zeros_like(q)
        return jnp.concatenate([jnp.where(lane < HEAD_DIM, q, zero),
                                jnp.where(lane >= HEAD_DIM, q, zero)], axis=0)

    def scores_phase(t, slot):
        qq = stacked_q(t)
        s0 = scores_t(qq, chunks[0])
        stab = jnp.max(s0, axis=0, keepdims=True)
        lv = jnp.zeros((1, 2 * tq), F32)
        for ci, c in enumerate(chunks):
            s = s0 if ci == 0 else scores_t(qq, c)
            p = jnp.exp2(s - stab)
            p_ref[slot, c[2]:c[2] + c[3], :] = p
            lv = lv + jnp.sum(p, axis=0, keepdims=True)
        lv_ref[slot] = jnp.broadcast_to(lv, (8, 2 * tq))

    def values_phase(slot):
        lsum = lv_ref[slot][0:1, :]
        l1, l2 = lsum[:, :tq], lsum[:, tq:]
        rho = lam * l1 / l2
        acc = jnp.zeros((HEAD_W, tq), F32)
        for c in chunks:
            a = (p_ref[slot, c[2]:c[2] + c[3], 0:tq]
                 - rho * p_ref[slot, c[2]:c[2] + c[3], tq:2 * tq]).astype(BF16)
            acc = acc + jnp.dot(vals_t(c), a, preferred_element_type=F32)
        return (acc / l1).T

    def finish(o, rows):
        o = (o * lax.rsqrt(jnp.mean(o * o, axis=-1, keepdims=True) + RMS_EPS)
             * sw_ref[0] * (1.0 - lam_init))
        return o * _silu(g_ref[0, rows, :])

    def emit(t, slot):
        rows = rows_of(t)
        out = finish(pend_ref[slot], rows)
        o_ref[0, rows, :] = out.astype(BF16)
        bad_ref[t] = jnp.sum(jnp.where(jnp.isfinite(out), 0.0, 1.0))

    scores_phase(0, 0)
    pend_ref[0] = values_phase(0)
    scores_phase(1, 1)

    def pair(i, carry):
        t = 2 * i + 2
        emit(t - 2, 0)
        pend_ref[1] = values_phase(1)
        scores_phase(t, 0)
        emit(t - 1, 1)
        pend_ref[0] = values_phase(0)
        scores_phase(t + 1, 1)
        return carry

    lax.fori_loop(0, nq // 2 - 1, pair, 0)
    emit(nq - 2, 0)
    pend_ref[1] = values_phase(1)
    emit(nq - 1, 1)

    def exact_tile(t, carry):
        @pl.when(bad_ref[t] > 0.0)
        def _():
            qq = stacked_q(t)
            m = None
            for c in chunks:
                s = scores_t(qq, c)
                p_ref[0, c[2]:c[2] + c[3], :] = s
                cm = jnp.max(s, axis=0, keepdims=True)
                m = cm if m is None else jnp.maximum(m, cm)
            lv = jnp.zeros((1, 2 * tq), F32)
            for c in chunks:
                p = jnp.exp2(p_ref[0, c[2]:c[2] + c[3], :] - m)
                p_ref[0, c[2]:c[2] + c[3], :] = p
                lv = lv + jnp.sum(p, axis=0, keepdims=True)
            lv_ref[0] = jnp.broadcast_to(lv, (8, 2 * tq))
            rows = rows_of(t)
            o_ref[0, rows, :] = finish(values_phase(0), rows).astype(BF16)
        return carry

    lax.fori_loop(0, nq, exact_tile, 0)


def _attention(q, k, v, g_att, cache, layer, lam_params, subln_w, lam_init, tq, ck):
    b, l, _ = q.shape
    tq = _row_tile(l, tq)
    ck = _row_tile(l, ck)
    nq = l // tq
    assert nq >= 2 and nq % 2 == 0
    has_cache = cache is not None
    past = cache[0].shape[3] if has_cache else 0
    lk = past + l
    seq = pl.BlockSpec((1, l, HEAD_W), lambda bi, h: (bi, 0, h))
    vec = lambda w: pl.BlockSpec((1, 1, w), lambda bi, h: (layer, 0, 0))
    vt_spec = pl.BlockSpec((1, HEAD_W, l), lambda bi, h: (bi, h, 0))
    in_specs = [seq, seq, vt_spec]
    args = [q, k, v]
    if has_cache:
        cspec = pl.BlockSpec((1, 1, 1, past, HEAD_W), lambda bi, h: (bi, layer, h, 0, 0))
        in_specs += [cspec, cspec]
        args += list(cache)
    in_specs += [seq] + [vec(HEAD_DIM)] * 4 + [vec(HEAD_W)]
    args += [g_att] + list(lam_params) + [subln_w]
    return pl.pallas_call(
        functools.partial(_attn_kernel, lam_init=lam_init, has_cache=has_cache, ck=ck,
                          n_new=l // ck, past=past, tq=tq, nq=nq),
        grid=(b, N_ATT_HEADS),
        in_specs=in_specs,
        out_specs=seq,
        out_shape=jax.ShapeDtypeStruct((b, l, ATT_WIDTH), BF16),
        scratch_shapes=[pltpu.VMEM((2, lk, 2 * tq), F32), pltpu.VMEM((2, 8, 2 * tq), F32),
                        pltpu.VMEM((2, tq, HEAD_W), F32), pltpu.SMEM((nq,), F32)]
        + ([pltpu.VMEM((HEAD_W, past), BF16)] if has_cache else []),
        name=f"attn_{l}",
        compiler_params=_cparams("parallel", "parallel"),
    )(*args)


def _fourier_kernel(g_ref, u_ref, gate_ref, o_ref, *, nb, l):
    h = l // 2
    dot = functools.partial(jnp.dot, preferred_element_type=F32)
    for j in range(nb):
        even = dot(g_ref[:, 0:h], u_ref[j, 0, 0]) + dot(g_ref[:, l:l + h], u_ref[j, 1, 0])
        odd = dot(g_ref[:, h:l], u_ref[j, 0, 1]) + dot(g_ref[:, l + h:2 * l], u_ref[j, 1, 1])
        o_ref[j, 0] = ((even + odd) * _silu(gate_ref[j, 0])).astype(BF16)
        o_ref[j, 1] = ((even - odd) * _silu(gate_ref[j, 1])).astype(BF16)


def _fourier(gmat, uf, g_f, tr, nb):
    b, l, w = g_f.shape
    assert b % nb == 0
    h = l // 2
    tr = _row_tile(h, tr)
    halves = pl.BlockSpec((nb, 2, tr, w), lambda bi, i: (bi, 0, i, 0))
    out = pl.pallas_call(
        functools.partial(_fourier_kernel, nb=nb, l=l),
        grid=(b // nb, h // tr),
        in_specs=[pl.BlockSpec((tr, 2 * l), lambda bi, i: (i, 0)),
                  pl.BlockSpec((nb, 2, 2, h, w), lambda bi, i: (bi, 0, 0, 0, 0)),
                  halves],
        out_specs=halves,
        out_shape=jax.ShapeDtypeStruct((b, 2, h, w), BF16),
        name=f"fourier_{l}",
        compiler_params=_cparams("parallel", "arbitrary"),
    )(gmat, uf, g_f.reshape(b, 2, h, w))
    return out.reshape(b, l, w)


def _rope_tables(l):
    n_freq = HEAD_DIM // 4
    rows = l // GRID_W
    row = np.repeat(np.arange(rows, dtype=np.float64), GRID_W)
    col = np.tile(np.arange(GRID_W, dtype=np.float64), rows)
    inv = ROPE_THETA ** (-np.arange(n_freq, dtype=np.float64) / n_freq)
    ang_r, ang_c = row[:, None] * inv, col[:, None] * inv
    zeros = np.zeros_like(ang_r)

    def per_head(r_lo, r_hi, c_lo, c_hi):
        one = np.concatenate([r_lo, r_hi, c_lo, c_hi], axis=-1)
        return jnp.asarray(np.concatenate([one, one], axis=-1), F32)

    cr, sr, cc, sc = np.cos(ang_r), np.sin(ang_r), np.cos(ang_c), np.sin(ang_c)
    cos_t = per_head(cr, cr, cc, cc)
    sa_t = per_head(-sr, zeros, -sc, zeros)
    sb_t = per_head(zeros, sr, zeros, sc)
    return cos_t, sa_t, sb_t


def _run_group(x, mod3, per_batch_mod, p, gmat, rope_tabs, cache, emit_kv, alpha, tiles):
    depth = p["w_in"].shape[0]
    kv_buf = None
    if emit_kv:
        kv_shape = (x.shape[0], depth, N_ATT_HEADS, x.shape[1], HEAD_W)
        kv_buf = (jnp.zeros(kv_shape, F32), jnp.zeros(kv_shape, F32))
    outs = _boundary(x, None, mod3, per_batch_mod, p, None, 0, rope_tabs, kv_buf, alpha, tiles["tm"])
    for layer in range(depth):
        lam_init = 0.8 - 0.6 * math.exp(-0.3 * layer)
        q, k, vt, g_att, uf, g_f, gm = outs[:7]
        if emit_kv:
            kv_buf = tuple(outs[7:9])
        att = _attention(q, k, vt, g_att, cache, layer, p["lam"], p["subln_w"], lam_init,
                         tiles["tq"], tiles["ck"])
        fo = _fourier(gmat, uf, g_f, tiles["tr"], tiles["nb"])
        nxt = layer + 1 if layer + 1 < depth else None
        res = _boundary(x, (att, fo, gm), mod3, per_batch_mod, p, layer, nxt, rope_tabs, kv_buf,
                        alpha, tiles["tm"])
        x, outs = res[0], res[1:]
    return x, kv_buf


def kernel(x_prompt, x_sample, c, cache_k, cache_v, c_ctx, w_ada, b_ada, w_in, w_out,
           lam_q1, lam_k1, lam_q2, lam_k2, subln_w, fourier_w, gmlp_ws, gmlp_bs, ln_g, ln_b):
    depth, d, _ = w_ada.shape
    nb_lat = c.shape[0]
    l_ctx, l_lat = x_prompt.shape[1], x_sample.shape[1]
    assert 1 + nb_lat <= MOD_ROWS
    alpha = (2.0 * depth) ** 0.25

    cvec = jnp.concatenate([c_ctx[None, :], c, jnp.zeros((MOD_ROWS - 1 - nb_lat, d), F32)], axis=0)
    mod = _modulation(cvec, w_ada, b_ada)

    fa, fb = _fourier_channel_maps(fourier_w, 1.0 / math.sqrt(FOURIER_GROUP_DIM))
    params = {
        "w_in": w_in.astype(BF16), "w_out": w_out.astype(BF16), "fa": fa, "fb": fb,
        "lam": tuple(t[:, None, :] for t in (lam_q1, lam_k1, lam_q2, lam_k2)),
        "subln_w": subln_w[:, None, :], "ln_g": ln_g[:, None, :], "ln_b": ln_b[:, None, :],
        "gmlp_w": gmlp_ws.astype(BF16).transpose(0, 2, 1, 3).reshape(depth, CHUNK, N_GMLP_HEADS * CHUNK),
        "gmlp_bias": jnp.repeat(gmlp_bs.astype(F32).transpose(0, 2, 1), GMLP_HEAD_DIM, axis=2),
        "avg": jnp.asarray(np.kron(np.eye(N_GMLP_HEADS), np.full((GMLP_HEAD_DIM,) * 2, 1.0 / GMLP_HEAD_DIM)), BF16),
    }
    ctx_tiles = dict(tm=256, tq=128, ck=256, tr=256, nb=2)
    lat_tiles = dict(tm=512, tq=256, ck=512, tr=512, nb=2)

    xp, (new_k, new_v) = _run_group(x_prompt, mod[:, 0:1].reshape(depth, 1, 3, d), False, params,
                                    _dft_matrix(l_ctx), None, None, True, alpha, ctx_tiles)

    xs, _ = _run_group(x_sample, mod[:, 1:1 + nb_lat].reshape(depth, nb_lat, 3, d), True, params,
                       _dft_matrix(l_lat), _rope_tables(l_lat), (cache_k, cache_v), False, alpha,
                       lat_tiles)
    return (xp, xs, new_k, new_v)
```

```python
import functools
import math

import numpy as np
import jax
import jax.numpy as jnp
from jax import lax
from jax.experimental import pallas as pl
from jax.experimental.pallas import tpu as pltpu

F32 = jnp.float32
BF16 = jnp.bfloat16

GRID_W = 64
HEAD_DIM = 64
N_ATT_HEADS = 4
HEAD_W = 2 * HEAD_DIM
ATT_WIDTH = N_ATT_HEADS * HEAD_W
N_FOURIER_GROUPS = 4
FOURIER_GROUP_DIM = 64
FOURIER_WIDTH = N_FOURIER_GROUPS * FOURIER_GROUP_DIM
N_GMLP_HEADS = 4
GMLP_HEAD_DIM = 64
GMLP_WIDTH = N_GMLP_HEADS * GMLP_HEAD_DIM
CHUNK = 128
ROPE_THETA = 10000.0
LN_EPS = 1e-5
RMS_EPS = 1e-5
QK_SCALE = HEAD_DIM ** -0.5
LOG2_E = math.log2(math.e)

_SEG_NAMES = ("q", "k", "v", "g_att", "f", "g_f", "u", "vm", "g_m")
_SEG_WIDTHS = (ATT_WIDTH,) * 4 + (FOURIER_WIDTH,) * 2 + (GMLP_WIDTH,) * 3
_SEG_START = dict(zip(_SEG_NAMES, np.concatenate([[0], np.cumsum(_SEG_WIDTHS)[:-1]]).tolist()))
_SEG_WIDTH = dict(zip(_SEG_NAMES, _SEG_WIDTHS))

VMEM_LIMIT_BYTES = 52 * 1024 * 1024
MOD_ROWS = 16


def _cparams(*sem):
    return pltpu.CompilerParams(dimension_semantics=sem, vmem_limit_bytes=VMEM_LIMIT_BYTES)


def _silu(x):
    return x * jax.nn.sigmoid(x)


def _row_tile(l, want):
    t = min(l, want)
    assert l % t == 0
    return t


def _mod_kernel(c_ref, w_ref, b_ref, o_ref):
    s = _silu(c_ref[...])
    o_ref[0] = jnp.dot(s, w_ref[0], precision=lax.Precision.HIGHEST,
                       preferred_element_type=F32) + b_ref[0]


def _modulation(cvec, w_ada, b_ada):
    depth, d, d3 = w_ada.shape
    tn = 1024
    return pl.pallas_call(
        _mod_kernel,
        grid=(depth, d3 // tn),
        in_specs=[pl.BlockSpec((MOD_ROWS, d), lambda l, j: (0, 0)),
                  pl.BlockSpec((1, d, tn), lambda l, j: (l, 0, j)),
                  pl.BlockSpec((1, 1, tn), lambda l, j: (l, 0, j))],
        out_specs=pl.BlockSpec((1, MOD_ROWS, tn), lambda l, j: (l, 0, j)),
        out_shape=jax.ShapeDtypeStruct((depth, MOD_ROWS, d3), F32),
        name="adaln_mod",
        compiler_params=_cparams("arbitrary", "arbitrary"),
    )(cvec, w_ada, b_ada.reshape(depth, 1, d3))


def _chanmap_kernel(cc_ref, ss_ref, w_ref, a_ref, b_ref, *, scale):
    w = w_ref[0]
    a_ref[0] = (jnp.dot(cc_ref[...], w, precision=lax.Precision.HIGHEST,
                        preferred_element_type=F32) * scale).astype(BF16)
    b_ref[0] = (jnp.dot(ss_ref[...], w, precision=lax.Precision.HIGHEST,
                        preferred_element_type=F32) * scale).astype(BF16)


def _fourier_channel_maps(fourier_w, scale):
    depth = fourier_w.shape[0]
    n = np.arange(FOURIER_GROUP_DIM)
    ang = 2.0 * np.pi * ((n[:, None] * n[None, :]) % FOURIER_GROUP_DIM) / FOURIER_GROUP_DIM
    eye = np.eye(N_FOURIER_GROUPS)
    cc = jnp.asarray(np.kron(eye, np.cos(ang)), F32)
    ss = jnp.asarray(np.kron(eye, np.sin(ang)), F32)
    wbd = jnp.einsum("gh,lgce->lgche", jnp.eye(N_FOURIER_GROUPS, dtype=F32),
                     fourier_w.astype(F32)).reshape(depth, FOURIER_WIDTH, FOURIER_WIDTH)
    w_spec = pl.BlockSpec((1, FOURIER_WIDTH, FOURIER_WIDTH), lambda l: (l, 0, 0))
    c_spec = pl.BlockSpec((FOURIER_WIDTH, FOURIER_WIDTH), lambda l: (0, 0))
    shp = jax.ShapeDtypeStruct((depth, FOURIER_WIDTH, FOURIER_WIDTH), BF16)
    return pl.pallas_call(
        functools.partial(_chanmap_kernel, scale=scale),
        grid=(depth,),
        in_specs=[c_spec, c_spec, w_spec],
        out_specs=[w_spec, w_spec],
        out_shape=[shp, shp],
        name="fourier_chanmap",
        compiler_params=_cparams("arbitrary"),
    )(cc, ss, wbd)


def _dft_kernel(ca_ref, sa_ref, cb_ref, sb_ref, g_ref, *, l):
    ca, sa = ca_ref[0], sa_ref[0]
    cb, sb = cb_ref[...], sb_ref[...]
    norm = 1.0 / math.sqrt(l)
    g_ref[:, 0:l] = ((ca * cb - sa * sb) * norm).astype(BF16)
    g_ref[:, l:2 * l] = ((sa * cb + ca * sb) * -norm).astype(BF16)


def _dft_matrix(l):
    assert l % (2 * GRID_W) == 0
    na = l // GRID_W
    k = np.concatenate([np.arange(0, l, 2), np.arange(1, l, 2)]).astype(np.int64)
    ang_a = ((np.arange(na // 2)[:, None] * k[None, :]) % na) * (2.0 * np.pi / na)
    ang_b = ((np.arange(GRID_W)[:, None] * k[None, :]) % l) * (2.0 * np.pi / l)
    ca = jnp.asarray(np.cos(ang_a).reshape(na // 2, 1, l), F32)
    sa = jnp.asarray(np.sin(ang_a).reshape(na // 2, 1, l), F32)
    cb, sb = jnp.asarray(np.cos(ang_b), F32), jnp.asarray(np.sin(ang_b), F32)
    a_spec = pl.BlockSpec((1, 1, l), lambda i: (i, 0, 0))
    b_spec = pl.BlockSpec((GRID_W, l), lambda i: (0, 0))
    return pl.pallas_call(
        functools.partial(_dft_kernel, l=l),
        grid=(na // 2,),
        in_specs=[a_spec, a_spec, b_spec, b_spec],
        out_specs=pl.BlockSpec((GRID_W, 2 * l), lambda i: (i, 0)),
        out_shape=jax.ShapeDtypeStruct((l // 2, 2 * l), BF16),
        name=f"dft_matrix_{l}",
        compiler_params=_cparams("arbitrary"),
    )(ca, sa, cb, sb)


def _layernorm(v, eps=LN_EPS):
    mu = jnp.mean(v, axis=-1, keepdims=True)
    vc = v - mu
    return vc * lax.rsqrt(jnp.mean(vc * vc, axis=-1, keepdims=True) + eps)


def _boundary_kernel(*refs, has_out, has_in, rope, emit_kv, alpha, n_chunks):
    refs = list(refs)
    take = lambda n: [refs.pop(0) for _ in range(n)]
    (x_ref,) = take(1)
    if has_out:
        att_ref, fo_ref, gmo_ref, modo_ref, wout_ref, lg_ref, lb_ref = take(7)
    if has_in:
        modi_ref, win_ref, fa_ref, fb_ref, gw_ref, gbias_ref, avg_ref, sel_ref = take(8)
        if rope:
            cos_ref, sa_ref, sb_ref = take(3)
        if emit_kv:
            take(2)
    if has_out:
        (xo_ref,) = take(1)
    if has_in:
        q_ref, k_ref, vt_ref, ga_ref, uf_ref, gf_ref, gm_ref = take(7)
        if emit_kv:
            kf_ref, vf_ref = take(2)

    x = x_ref[0]
    if has_out:
        a0, a1 = ATT_WIDTH, ATT_WIDTH + FOURIER_WIDTH
        y = (jnp.dot(att_ref[0], wout_ref[0, 0:a0, :], preferred_element_type=F32)
             + jnp.dot(fo_ref[0], wout_ref[0, a0:a1, :], preferred_element_type=F32)
             + jnp.dot(gmo_ref[0], wout_ref[0, a1:, :], preferred_element_type=F32))
        gate = modo_ref[0, 0, 2:3, :] * (1.0 / alpha)
        x = _layernorm(x + gate * y, LN_EPS / (alpha * alpha)) * lg_ref[0] + lb_ref[0]
        xo_ref[0] = x
    if not has_in:
        return

    shift, scale = modi_ref[0, 0, 0:1, :], modi_ref[0, 0, 1:2, :]
    h = (_layernorm(x) * (1.0 + scale) + shift).astype(BF16)

    def proj(name):
        c0 = _SEG_START[name]
        return jnp.dot(h, win_ref[0, :, c0:c0 + _SEG_WIDTH[name]], preferred_element_type=F32)

    def rotate(z):
        return (z * cos_ref[...] + pltpu.roll(z, HEAD_W - 16, 1) * sa_ref[...]
                + pltpu.roll(z, 16, 1) * sb_ref[...])

    zq, zk, zv = proj("q"), proj("k"), proj("v")
    for hh in range(N_ATT_HEADS):
        sl = slice(hh * HEAD_W, (hh + 1) * HEAD_W)
        qh, kh = zq[:, sl], zk[:, sl]
        if emit_kv:
            kf_ref[0, 0, hh] = kh
            vf_ref[0, 0, hh] = zv[:, sl]
        if rope:
            qh, kh = rotate(qh), rotate(kh)
        q_ref[0, :, sl] = (qh * (QK_SCALE * LOG2_E)).astype(BF16)
        k_ref[0, :, sl] = kh.astype(BF16)
    vt_ref[0] = zv.T.astype(BF16)
    ga_ref[0] = proj("g_att")
    f = jnp.dot(sel_ref[...], proj("f").astype(BF16), preferred_element_type=F32).astype(BF16)
    half = sel_ref.shape[0] // 2
    for j, fm_ref in enumerate((fa_ref, fb_ref)):
        u = jnp.dot(f, fm_ref[0], preferred_element_type=F32).astype(BF16)
        uf_ref[0, j, 0] = u[:half]
        uf_ref[0, j, 1] = u[half:]
    gf_ref[0] = proj("g_f")

    zu, zvm, zgm = proj("u"), proj("vm"), proj("g_m")
    avg = avg_ref[...]

    def group_mean(v):
        hi = v.astype(BF16)
        lo = (v - hi.astype(F32)).astype(BF16)
        return (jnp.dot(hi, avg, preferred_element_type=F32)
                + jnp.dot(lo, avg, preferred_element_type=F32))

    lane = lax.broadcasted_iota(jnp.int32, (CHUNK, GMLP_WIDTH), 1)
    head_mask = [(lane >= hh * GMLP_HEAD_DIM) & (lane < (hh + 1) * GMLP_HEAD_DIM)
                 for hh in range(N_GMLP_HEADS)]
    vc_all = zvm - group_mean(zvm)
    vn_all = (vc_all * lax.rsqrt(group_mean(vc_all * vc_all) + LN_EPS)).astype(BF16)
    for c in range(n_chunks):
        rows = slice(c * CHUNK, (c + 1) * CHUNK)
        vn = vn_all[rows]
        stacked = jnp.concatenate(
            [jnp.where(head_mask[hh], vn, jnp.zeros_like(vn)) for hh in range(N_GMLP_HEADS)], axis=0)
        s = jnp.dot(gw_ref[0], stacked, preferred_element_type=F32) + gbias_ref[0]
        gm_ref[0, rows, :] = (zu[rows] * s * _silu(zgm[rows])).astype(BF16)


def _boundary(x, branches, mod3, per_batch_mod, p, out_layer, in_layer, rope_tabs, kv_buf, alpha, tm):
    b, l, d = x.shape
    tm = _row_tile(l, tm)
    has_out, has_in = out_layer is not None, in_layer is not None
    emit_kv, aliases = kv_buf is not None and has_in, {}
    row = lambda w: pl.BlockSpec((1, tm, w), lambda bi, i: (bi, i, 0))
    const2 = lambda shp: pl.BlockSpec(shp, lambda bi, i: (0, 0))
    per_layer = lambda shp, layer: pl.BlockSpec((1,) + shp, lambda bi, i: (layer, 0, 0))
    mod_spec = lambda layer: pl.BlockSpec(
        (1, 1, 3, d), (lambda bi, i: (layer, bi, 0, 0)) if per_batch_mod else (lambda bi, i: (layer, 0, 0, 0)))
    sds = jax.ShapeDtypeStruct
    in_specs, args, out_specs, out_shape = [row(d)], [x], [], []
    if has_out:
        in_specs += [row(ATT_WIDTH), row(FOURIER_WIDTH), row(GMLP_WIDTH), mod_spec(out_layer),
                     per_layer((d, d), out_layer), per_layer((1, d), out_layer), per_layer((1, d), out_layer)]
        args += list(branches) + [mod3, p["w_out"], p["ln_g"], p["ln_b"]]
        out_specs += [row(d)]
        out_shape += [sds((b, l, d), F32)]
    if has_in:
        n_in = p["w_in"].shape[2]
        in_specs += [mod_spec(in_layer), per_layer((d, n_in), in_layer),
                     per_layer((FOURIER_WIDTH, FOURIER_WIDTH), in_layer),
                     per_layer((FOURIER_WIDTH, FOURIER_WIDTH), in_layer),
                     per_layer((CHUNK, N_GMLP_HEADS * CHUNK), in_layer),
                     per_layer((CHUNK, GMLP_WIDTH), in_layer), const2((GMLP_WIDTH, GMLP_WIDTH)),
                     const2((tm, tm))]
        sel = np.zeros((tm, tm), np.float32)
        sel[np.arange(tm), np.concatenate([np.arange(0, tm, 2), np.arange(1, tm, 2)])] = 1.0
        args += [mod3, p["w_in"], p["fa"], p["fb"], p["gmlp_w"], p["gmlp_bias"], p["avg"],
                 jnp.asarray(sel, BF16)]
        if rope_tabs is not None:
            in_specs += [pl.BlockSpec((tm, HEAD_W), lambda bi, i: (i, 0))] * 3
            args += list(rope_tabs)
        out_shape += [sds((b, l, ATT_WIDTH), BF16)] * 2 + [
            sds((b, ATT_WIDTH, l), BF16), sds((b, l, ATT_WIDTH), F32),
            sds((b, 2, 2, l // 2, FOURIER_WIDTH), BF16), sds((b, l, FOURIER_WIDTH), F32),
            sds((b, l, GMLP_WIDTH), BF16)]
        out_specs += [row(ATT_WIDTH)] * 2 + [
            pl.BlockSpec((1, ATT_WIDTH, tm), lambda bi, i: (bi, 0, i)), row(ATT_WIDTH),
            pl.BlockSpec((1, 2, 2, tm // 2, FOURIER_WIDTH), lambda bi, i: (bi, 0, 0, i, 0)),
            row(FOURIER_WIDTH), row(GMLP_WIDTH)]
        if emit_kv:
            aliases = {len(args): len(out_shape), len(args) + 1: len(out_shape) + 1}
            in_specs += [pl.BlockSpec(memory_space=pl.ANY)] * 2
            args += list(kv_buf)
            out_shape += [sds(kv_buf[0].shape, F32)] * 2
            out_specs += [pl.BlockSpec((1, 1, N_ATT_HEADS, tm, HEAD_W),
                                       lambda bi, i: (bi, in_layer, 0, i, 0))] * 2
    kern = functools.partial(_boundary_kernel, has_out=has_out, has_in=has_in,
                             rope=rope_tabs is not None, emit_kv=emit_kv, alpha=alpha,
                             n_chunks=tm // CHUNK)

    return pl.pallas_call(
        kern, grid=(b, l // tm), in_specs=in_specs, out_specs=out_specs, out_shape=out_shape,
        input_output_aliases=aliases,
        name=f"boundary_{l}_{int(has_out)}{int(has_in)}",
        compiler_params=_cparams("parallel", "parallel"),
    )(*args)


def _attn_kernel(*refs, lam_init, has_cache, ck, n_new, past, tq, nq, hp):
    if has_cache:
        (q_ref, k_ref, vt_ref, pk_ref, pv_ref, g_ref, lq1_ref, lk1_ref, lq2_ref, lk2_ref,
         sw_ref, o_ref, p_ref, lv_ref, pend_ref, bad_ref, pvt_ref) = refs
    else:
        (q_ref, k_ref, vt_ref, g_ref, lq1_ref, lk1_ref, lq2_ref, lk2_ref,
         sw_ref, o_ref, p_ref, lv_ref, pend_ref, bad_ref) = refs

    chunks = []
    if has_cache:
        chunks.append((True, 0, 0, past))
    for j in range(n_new):
        chunks.append((False, j * ck, past + j * ck, ck))

    if has_cache:
        pvt_ref[...] = pv_ref[0, 0, 0].T.astype(BF16)

    lam = (jnp.exp(jnp.sum(lq1_ref[0] * lk1_ref[0], axis=-1, keepdims=True))
           - jnp.exp(jnp.sum(lq2_ref[0] * lk2_ref[0], axis=-1, keepdims=True)) + lam_init)

    def rows_of(t):
        return pl.ds(pl.multiple_of(t * tq, tq), tq)

    def head(hh):
        cs = slice(hh * HEAD_W, (hh + 1) * HEAD_W)
        so = 2 * hh

        def keys(c):
            return pk_ref[0, 0, 0].astype(BF16) if c[0] else k_ref[0, c[1]:c[1] + c[3], cs]

        def vals_t(c):
            return pvt_ref[...] if c[0] else vt_ref[0, cs, c[1]:c[1] + c[3]]

        def scores_t(qq, c):
            return lax.dot_general(keys(c), qq, (((1,), (1,)), ((), ())), preferred_element_type=F32)

        def stacked_q(t):
            q = q_ref[0, rows_of(t), cs]
            lane = lax.broadcasted_iota(jnp.int32, q.shape, 1)
            zero = jnp.zeros_like(q)
            return jnp.concatenate([jnp.where(lane < HEAD_DIM, q, zero),
                                    jnp.where(lane >= HEAD_DIM, q, zero)], axis=0)

        def scores_phase(t, slot):
            qq = stacked_q(t)
            s0 = scores_t(qq, chunks[0])
            stab = jnp.max(s0, axis=0, keepdims=True)
            lv = jnp.zeros((1, 2 * tq), F32)
            for ci, c in enumerate(chunks):
                s = s0 if ci == 0 else scores_t(qq, c)
                p = jnp.exp2(s - stab)
                p_ref[so + slot, c[2]:c[2] + c[3], :] = p
                lv = lv + jnp.sum(p, axis=0, keepdims=True)
            lv_ref[so + slot] = jnp.broadcast_to(lv, (8, 2 * tq))

        def values_phase(slot):
            lsum = lv_ref[so + slot][0:1, :]
            l1, l2 = lsum[:, :tq], lsum[:, tq:]
            rho = lam * l1 / l2
            acc = jnp.zeros((HEAD_W, tq), F32)
            for c in chunks:
                a = (p_ref[so + slot, c[2]:c[2] + c[3], 0:tq]
                     - rho * p_ref[so + slot, c[2]:c[2] + c[3], tq:2 * tq]).astype(BF16)
                acc = acc + jnp.dot(vals_t(c), a, preferred_element_type=F32)
            return (acc / l1).T

        def finish(o, rows):
            o = (o * lax.rsqrt(jnp.mean(o * o, axis=-1, keepdims=True) + RMS_EPS)
                 * sw_ref[0] * (1.0 - lam_init))
            return o * _silu(g_ref[0, rows, cs])

        def emit(t, slot):
            rows = rows_of(t)
            out = finish(pend_ref[so + slot], rows)
            o_ref[0, rows, cs] = out.astype(BF16)
            bad_ref[hh * nq + t] = jnp.sum(jnp.where(jnp.isfinite(out), 0.0, 1.0))

        def keep(slot, value_fn):
            pend_ref[so + slot] = value_fn()

        def pair(i, carry):
            t = 2 * i + 2
            emit(t - 2, 0)
            keep(1, lambda: values_phase(1))
            scores_phase(t, 0)
            emit(t - 1, 1)
            keep(0, lambda: values_phase(0))
            scores_phase(t + 1, 1)
            return carry

        stages = [lambda: scores_phase(0, 0), lambda: keep(0, lambda: values_phase(0)),
                  lambda: scores_phase(1, 1)]
        if nq > 2:
            stages.append(lambda: lax.fori_loop(0, nq // 2 - 1, pair, 0))
        stages += [lambda: emit(nq - 2, 0), lambda: keep(1, lambda: values_phase(1)),
                   lambda: emit(nq - 1, 1)]

        def exact():
            def exact_tile(t, carry):
                @pl.when(bad_ref[hh * nq + t] > 0.0)
                def _():
                    qq = stacked_q(t)
                    m = None
                    for c in chunks:
                        s = scores_t(qq, c)
                        p_ref[so, c[2]:c[2] + c[3], :] = s
                        cm = jnp.max(s, axis=0, keepdims=True)
                        m = cm if m is None else jnp.maximum(m, cm)
                    lv = jnp.zeros((1, 2 * tq), F32)
                    for c in chunks:
                        p = jnp.exp2(p_ref[so, c[2]:c[2] + c[3], :] - m)
                        p_ref[so, c[2]:c[2] + c[3], :] = p
                        lv = lv + jnp.sum(p, axis=0, keepdims=True)
                    lv_ref[so] = jnp.broadcast_to(lv, (8, 2 * tq))
                    rows = rows_of(t)
                    o_ref[0, rows, cs] = finish(values_phase(0), rows).astype(BF16)
                return carry

            lax.fori_loop(0, nq, exact_tile, 0)

        return stages, exact

    heads = [head(hh) for hh in range(hp)]
    for stage in zip(*[stages for stages, _ in heads]):
        for run in stage:
            run()
    for _, exact in heads:
        exact()


def _attention(q, k, v, g_att, cache, layer, lam_params, subln_w, lam_init, tq, ck, hp):
    b, l, _ = q.shape
    tq = _row_tile(l, tq)
    ck = _row_tile(l, ck)
    nq = l // tq
    assert nq >= 2 and nq % 2 == 0 and N_ATT_HEADS % hp == 0
    has_cache = cache is not None
    assert not (has_cache and hp != 1)
    past = cache[0].shape[3] if has_cache else 0
    lk = past + l
    seq = pl.BlockSpec((1, l, hp * HEAD_W), lambda bi, h: (bi, 0, h))
    vec = lambda w: pl.BlockSpec((1, 1, w), lambda bi, h: (layer, 0, 0))
    vt_spec = pl.BlockSpec((1, hp * HEAD_W, l), lambda bi, h: (bi, h, 0))
    in_specs = [seq, seq, vt_spec]
    args = [q, k, v]
    if has_cache:
        cspec = pl.BlockSpec((1, 1, 1, past, HEAD_W), lambda bi, h: (bi, layer, h, 0, 0))
        in_specs += [cspec, cspec]
        args += list(cache)
    in_specs += [seq] + [vec(HEAD_DIM)] * 4 + [vec(HEAD_W)]
    args += [g_att] + list(lam_params) + [subln_w]
    return pl.pallas_call(
        functools.partial(_attn_kernel, lam_init=lam_init, has_cache=has_cache, ck=ck,
                          n_new=l // ck, past=past, tq=tq, nq=nq, hp=hp),
        grid=(b, N_ATT_HEADS // hp),
        in_specs=in_specs,
        out_specs=seq,
        out_shape=jax.ShapeDtypeStruct((b, l, ATT_WIDTH), BF16),
        scratch_shapes=[pltpu.VMEM((2 * hp, lk, 2 * tq), F32), pltpu.VMEM((2 * hp, 8, 2 * tq), F32),
                        pltpu.VMEM((2 * hp, tq, HEAD_W), F32), pltpu.SMEM((hp * nq,), F32)]
        + ([pltpu.VMEM((HEAD_W, past), BF16)] if has_cache else []),
        name=f"attn_{l}",
        compiler_params=_cparams("parallel", "parallel"),
    )(*args)


def _fourier_kernel(g_ref, u_ref, gate_ref, o_ref, *, nb, l):
    h = l // 2
    dot = functools.partial(jnp.dot, preferred_element_type=F32)
    for j in range(nb):
        even = dot(g_ref[:, 0:h], u_ref[j, 0, 0]) + dot(g_ref[:, l:l + h], u_ref[j, 1, 0])
        odd = dot(g_ref[:, h:l], u_ref[j, 0, 1]) + dot(g_ref[:, l + h:2 * l], u_ref[j, 1, 1])
        o_ref[j, 0] = ((even + odd) * _silu(gate_ref[j, 0])).astype(BF16)
        o_ref[j, 1] = ((even - odd) * _silu(gate_ref[j, 1])).astype(BF16)


def _fourier(gmat, uf, g_f, tr, nb):
    b, l, w = g_f.shape
    assert b % nb == 0
    h = l // 2
    tr = _row_tile(h, tr)
    halves = pl.BlockSpec((nb, 2, tr, w), lambda bi, i: (bi, 0, i, 0))
    out = pl.pallas_call(
        functools.partial(_fourier_kernel, nb=nb, l=l),
        grid=(b // nb, h // tr),
        in_specs=[pl.BlockSpec((tr, 2 * l), lambda bi, i: (i, 0)),
                  pl.BlockSpec((nb, 2, 2, h, w), lambda bi, i: (bi, 0, 0, 0, 0)),
                  halves],
        out_specs=halves,
        out_shape=jax.ShapeDtypeStruct((b, 2, h, w), BF16),
        name=f"fourier_{l}",
        compiler_params=_cparams("parallel", "arbitrary"),
    )(gmat, uf, g_f.reshape(b, 2, h, w))
    return out.reshape(b, l, w)


def _rope_tables(l):
    n_freq = HEAD_DIM // 4
    rows = l // GRID_W
    row = np.repeat(np.arange(rows, dtype=np.float64), GRID_W)
    col = np.tile(np.arange(GRID_W, dtype=np.float64), rows)
    inv = ROPE_THETA ** (-np.arange(n_freq, dtype=np.float64) / n_freq)
    ang_r, ang_c = row[:, None] * inv, col[:, None] * inv
    zeros = np.zeros_like(ang_r)

    def per_head(r_lo, r_hi, c_lo, c_hi):
        one = np.concatenate([r_lo, r_hi, c_lo, c_hi], axis=-1)
        return jnp.asarray(np.concatenate([one, one], axis=-1), F32)

    cr, sr, cc, sc = np.cos(ang_r), np.sin(ang_r), np.cos(ang_c), np.sin(ang_c)
    cos_t = per_head(cr, cr, cc, cc)
    sa_t = per_head(-sr, zeros, -sc, zeros)
    sb_t = per_head(zeros, sr, zeros, sc)
    return cos_t, sa_t, sb_t


def _run_group(x, mod3, per_batch_mod, p, gmat, rope_tabs, cache, emit_kv, alpha, tiles):
    depth = p["w_in"].shape[0]
    kv_buf = None
    if emit_kv:
        kv_shape = (x.shape[0], depth, N_ATT_HEADS, x.shape[1], HEAD_W)
        kv_buf = (jnp.zeros(kv_shape, F32), jnp.zeros(kv_shape, F32))
    outs = _boundary(x, None, mod3, per_batch_mod, p, None, 0, rope_tabs, kv_buf, alpha, tiles["tm"])
    for layer in range(depth):
        lam_init = 0.8 - 0.6 * math.exp(-0.3 * layer)
        q, k, vt, g_att, uf, g_f, gm = outs[:7]
        if emit_kv:
            kv_buf = tuple(outs[7:9])
        att = _attention(q, k, vt, g_att, cache, layer, p["lam"], p["subln_w"], lam_init,
                         tiles["tq"], tiles["ck"], tiles["hp"])
        fo = _fourier(gmat, uf, g_f, tiles["tr"], tiles["nb"])
        nxt = layer + 1 if layer + 1 < depth else None
        res = _boundary(x, (att, fo, gm), mod3, per_batch_mod, p, layer, nxt, rope_tabs, kv_buf,
                        alpha, tiles["tm"])
        x, outs = res[0], res[1:]
    return x, kv_buf


def kernel(x_prompt, x_sample, c, cache_k, cache_v, c_ctx, w_ada, b_ada, w_in, w_out,
           lam_q1, lam_k1, lam_q2, lam_k2, subln_w, fourier_w, gmlp_ws, gmlp_bs, ln_g, ln_b):
    depth, d, _ = w_ada.shape
    nb_lat = c.shape[0]
    l_ctx, l_lat = x_prompt.shape[1], x_sample.shape[1]
    assert 1 + nb_lat <= MOD_ROWS
    alpha = (2.0 * depth) ** 0.25

    cvec = jnp.concatenate([c_ctx[None, :], c, jnp.zeros((MOD_ROWS - 1 - nb_lat, d), F32)], axis=0)
    mod = _modulation(cvec, w_ada, b_ada)

    fa, fb = _fourier_channel_maps(fourier_w, 1.0 / math.sqrt(FOURIER_GROUP_DIM))
    params = {
        "w_in": w_in.astype(BF16), "w_out": w_out.astype(BF16), "fa": fa, "fb": fb,
        "lam": tuple(t[:, None, :] for t in (lam_q1, lam_k1, lam_q2, lam_k2)),
        "subln_w": subln_w[:, None, :], "ln_g": ln_g[:, None, :], "ln_b": ln_b[:, None, :],
        "gmlp_w": gmlp_ws.astype(BF16).transpose(0, 2, 1, 3).reshape(depth, CHUNK, N_GMLP_HEADS * CHUNK),
        "gmlp_bias": jnp.repeat(gmlp_bs.astype(F32).transpose(0, 2, 1), GMLP_HEAD_DIM, axis=2),
        "avg": jnp.asarray(np.kron(np.eye(N_GMLP_HEADS), np.full((GMLP_HEAD_DIM,) * 2, 1.0 / GMLP_HEAD_DIM)), BF16),
    }
    ctx_tiles = dict(tm=256, tq=128, ck=256, tr=256, nb=2, hp=N_ATT_HEADS)
    lat_tiles = dict(tm=512, tq=256, ck=512, tr=512, nb=2, hp=1)

    xp, (new_k, new_v) = _run_group(x_prompt, mod[:, 0:1].reshape(depth, 1, 3, d), False, params,
                                    _dft_matrix(l_ctx), None, None, True, alpha, ctx_tiles)

    xs, _ = _run_group(x_sample, mod[:, 1:1 + nb_lat].reshape(depth, nb_lat, 3, d), True, params,
                       _dft_matrix(l_lat), _rope_tables(l_lat), (cache_k, cache_v), False, alpha,
                       lat_tiles)
    return (xp, xs, new_k, new_v)
```

```python
import functools
import math

import numpy as np
import jax
import jax.numpy as jnp
from jax import lax
from jax.experimental import pallas as pl
from jax.experimental.pallas import tpu as pltpu

F32 = jnp.float32
BF16 = jnp.bfloat16

GRID_W = 64
HEAD_DIM = 64
N_ATT_HEADS = 4
HEAD_W = 2 * HEAD_DIM
ATT_WIDTH = N_ATT_HEADS * HEAD_W
N_FOURIER_GROUPS = 4
FOURIER_GROUP_DIM = 64
FOURIER_WIDTH = N_FOURIER_GROUPS * FOURIER_GROUP_DIM
N_GMLP_HEADS = 4
GMLP_HEAD_DIM = 64
GMLP_WIDTH = N_GMLP_HEADS * GMLP_HEAD_DIM
CHUNK = 128
ROPE_THETA = 10000.0
LN_EPS = 1e-5
RMS_EPS = 1e-5
QK_SCALE = HEAD_DIM ** -0.5
LOG2_E = math.log2(math.e)

_SEG_NAMES = ("q", "k", "v", "g_att", "f", "g_f", "u", "vm", "g_m")
_SEG_WIDTHS = (ATT_WIDTH,) * 4 + (FOURIER_WIDTH,) * 2 + (GMLP_WIDTH,) * 3
_SEG_START = dict(zip(_SEG_NAMES, np.concatenate([[0], np.cumsum(_SEG_WIDTHS)[:-1]]).tolist()))
_SEG_WIDTH = dict(zip(_SEG_NAMES, _SEG_WIDTHS))

VMEM_LIMIT_BYTES = 52 * 1024 * 1024
MOD_ROWS = 16


def _cparams(*sem):
    return pltpu.CompilerParams(dimension_semantics=sem, vmem_limit_bytes=VMEM_LIMIT_BYTES)


def _silu(x):
    return x * jax.nn.sigmoid(x)


def _row_tile(l, want):
    t = min(l, want)
    assert l % t == 0
    return t


def _mod_kernel(c_ref, w_ref, b_ref, o_ref):
    s = _silu(c_ref[...])
    o_ref[0] = jnp.dot(s, w_ref[0], precision=lax.Precision.HIGHEST,
                       preferred_element_type=F32) + b_ref[0]


def _modulation(cvec, w_ada, b_ada):
    depth, d, d3 = w_ada.shape
    tn = 1024
    return pl.pallas_call(
        _mod_kernel,
        grid=(depth, d3 // tn),
        in_specs=[pl.BlockSpec((MOD_ROWS, d), lambda l, j: (0, 0)),
                  pl.BlockSpec((1, d, tn), lambda l, j: (l, 0, j)),
                  pl.BlockSpec((1, 1, tn), lambda l, j: (l, 0, j))],
        out_specs=pl.BlockSpec((1, MOD_ROWS, tn), lambda l, j: (l, 0, j)),
        out_shape=jax.ShapeDtypeStruct((depth, MOD_ROWS, d3), F32),
        name="adaln_mod",
        compiler_params=_cparams("arbitrary", "arbitrary"),
    )(cvec, w_ada, b_ada.reshape(depth, 1, d3))


def _chanmap_kernel(cc_ref, ss_ref, w_ref, a_ref, b_ref, *, scale):
    w = w_ref[0]
    a_ref[0] = (jnp.dot(cc_ref[...], w, precision=lax.Precision.HIGHEST,
                        preferred_element_type=F32) * scale).astype(BF16)
    b_ref[0] = (jnp.dot(ss_ref[...], w, precision=lax.Precision.HIGHEST,
                        preferred_element_type=F32) * scale).astype(BF16)


def _fourier_channel_maps(fourier_w, scale):
    depth = fourier_w.shape[0]
    n = np.arange(FOURIER_GROUP_DIM)
    ang = 2.0 * np.pi * ((n[:, None] * n[None, :]) % FOURIER_GROUP_DIM) / FOURIER_GROUP_DIM
    eye = np.eye(N_FOURIER_GROUPS)
    cc = jnp.asarray(np.kron(eye, np.cos(ang)), F32)
    ss = jnp.asarray(np.kron(eye, np.sin(ang)), F32)
    wbd = jnp.einsum("gh,lgce->lgche", jnp.eye(N_FOURIER_GROUPS, dtype=F32),
                     fourier_w.astype(F32)).reshape(depth, FOURIER_WIDTH, FOURIER_WIDTH)
    w_spec = pl.BlockSpec((1, FOURIER_WIDTH, FOURIER_WIDTH), lambda l: (l, 0, 0))
    c_spec = pl.BlockSpec((FOURIER_WIDTH, FOURIER_WIDTH), lambda l: (0, 0))
    shp = jax.ShapeDtypeStruct((depth, FOURIER_WIDTH, FOURIER_WIDTH), BF16)
    return pl.pallas_call(
        functools.partial(_chanmap_kernel, scale=scale),
        grid=(depth,),
        in_specs=[c_spec, c_spec, w_spec],
        out_specs=[w_spec, w_spec],
        out_shape=[shp, shp],
        name="fourier_chanmap",
        compiler_params=_cparams("arbitrary"),
    )(cc, ss, wbd)


def _dft_kernel(ca_ref, sa_ref, cb_ref, sb_ref, g_ref, *, l):
    ca, sa = ca_ref[0], sa_ref[0]
    cb, sb = cb_ref[...], sb_ref[...]
    norm = 1.0 / math.sqrt(l)
    g_ref[:, 0:l] = ((ca * cb - sa * sb) * norm).astype(BF16)
    g_ref[:, l:2 * l] = ((sa * cb + ca * sb) * -norm).astype(BF16)


def _dft_matrix(l):
    assert l % (2 * GRID_W) == 0
    na = l // GRID_W
    k = np.concatenate([np.arange(0, l, 2), np.arange(1, l, 2)]).astype(np.int64)
    ang_a = ((np.arange(na // 2)[:, None] * k[None, :]) % na) * (2.0 * np.pi / na)
    ang_b = ((np.arange(GRID_W)[:, None] * k[None, :]) % l) * (2.0 * np.pi / l)
    ca = jnp.asarray(np.cos(ang_a).reshape(na // 2, 1, l), F32)
    sa = jnp.asarray(np.sin(ang_a).reshape(na // 2, 1, l), F32)
    cb, sb = jnp.asarray(np.cos(ang_b), F32), jnp.asarray(np.sin(ang_b), F32)
    a_spec = pl.BlockSpec((1, 1, l), lambda i: (i, 0, 0))
    b_spec = pl.BlockSpec((GRID_W, l), lambda i: (0, 0))
    return pl.pallas_call(
        functools.partial(_dft_kernel, l=l),
        grid=(na // 2,),
        in_specs=[a_spec, a_spec, b_spec, b_spec],
        out_specs=pl.BlockSpec((GRID_W, 2 * l), lambda i: (i, 0)),
        out_shape=jax.ShapeDtypeStruct((l // 2, 2 * l), BF16),
        name=f"dft_matrix_{l}",
        compiler_params=_cparams("arbitrary"),
    )(ca, sa, cb, sb)


def _layernorm(v, eps=LN_EPS):
    mu = jnp.mean(v, axis=-1, keepdims=True)
    vc = v - mu
    return vc * lax.rsqrt(jnp.mean(vc * vc, axis=-1, keepdims=True) + eps)


def _boundary_kernel(*refs, has_out, has_in, rope, emit_kv, alpha, n_chunks):
    refs = list(refs)
    take = lambda n: [refs.pop(0) for _ in range(n)]
    (x_ref,) = take(1)
    if has_out:
        att_ref, fo_ref, gmo_ref, modo_ref, wout_ref, lg_ref, lb_ref = take(7)
    if has_in:
        modi_ref, win_ref, fa_ref, fb_ref, gw_ref, gbias_ref, avg_ref, sel_ref = take(8)
        if rope:
            cos_ref, sa_ref, sb_ref = take(3)
        if emit_kv:
            take(2)
    if has_out:
        (xo_ref,) = take(1)
    if has_in:
        q_ref, k_ref, vt_ref, ga_ref, uf_ref, gf_ref, gm_ref = take(7)
        if emit_kv:
            kf_ref, vf_ref = take(2)

    x = x_ref[0]
    if has_out:
        a0, a1 = ATT_WIDTH, ATT_WIDTH + FOURIER_WIDTH
        y = (jnp.dot(att_ref[0], wout_ref[0, 0:a0, :], preferred_element_type=F32)
             + jnp.dot(fo_ref[0], wout_ref[0, a0:a1, :], preferred_element_type=F32)
             + jnp.dot(gmo_ref[0], wout_ref[0, a1:, :], preferred_element_type=F32))
        gate = modo_ref[0, 0, 2:3, :] * (1.0 / alpha)
        x = _layernorm(x + gate * y, LN_EPS / (alpha * alpha)) * lg_ref[0] + lb_ref[0]
        xo_ref[0] = x
    if not has_in:
        return

    shift, scale = modi_ref[0, 0, 0:1, :], modi_ref[0, 0, 1:2, :]
    h = (_layernorm(x) * (1.0 + scale) + shift).astype(BF16)

    def proj(name):
        c0 = _SEG_START[name]
        return jnp.dot(h, win_ref[0, :, c0:c0 + _SEG_WIDTH[name]], preferred_element_type=F32)

    def rotate(z):
        return (z * cos_ref[...] + pltpu.roll(z, HEAD_W - 16, 1) * sa_ref[...]
                + pltpu.roll(z, 16, 1) * sb_ref[...])

    zq, zk, zv = proj("q"), proj("k"), proj("v")
    for hh in range(N_ATT_HEADS):
        sl = slice(hh * HEAD_W, (hh + 1) * HEAD_W)
        qh, kh = zq[:, sl], zk[:, sl]
        if emit_kv:
            kf_ref[0, 0, hh] = kh
            vf_ref[0, 0, hh] = zv[:, sl]
        if rope:
            qh, kh = rotate(qh), rotate(kh)
        q_ref[0, :, sl] = (qh * (QK_SCALE * LOG2_E)).astype(BF16)
        k_ref[0, :, sl] = kh.astype(BF16)
    vt_ref[0] = zv.T.astype(BF16)
    ga_ref[0] = proj("g_att")
    f = jnp.dot(sel_ref[...], proj("f").astype(BF16), preferred_element_type=F32).astype(BF16)
    half = sel_ref.shape[0] // 2
    for j, fm_ref in enumerate((fa_ref, fb_ref)):
        u = jnp.dot(f, fm_ref[0], preferred_element_type=F32).astype(BF16)
        uf_ref[0, j, 0] = u[:half]
        uf_ref[0, j, 1] = u[half:]
    gf_ref[0] = proj("g_f")

    zu, zvm, zgm = proj("u"), proj("vm"), proj("g_m")
    avg = avg_ref[...]

    def group_mean(v):
        hi = v.astype(BF16)
        lo = (v - hi.astype(F32)).astype(BF16)
        return (jnp.dot(hi, avg, preferred_element_type=F32)
                + jnp.dot(lo, avg, preferred_element_type=F32))

    lane = lax.broadcasted_iota(jnp.int32, (CHUNK, GMLP_WIDTH), 1)
    head_mask = [(lane >= hh * GMLP_HEAD_DIM) & (lane < (hh + 1) * GMLP_HEAD_DIM)
                 for hh in range(N_GMLP_HEADS)]
    vc_all = zvm - group_mean(zvm)
    vn_all = (vc_all * lax.rsqrt(group_mean(vc_all * vc_all) + LN_EPS)).astype(BF16)
    for c in range(n_chunks):
        rows = slice(c * CHUNK, (c + 1) * CHUNK)
        vn = vn_all[rows]
        stacked = jnp.concatenate(
            [jnp.where(head_mask[hh], vn, jnp.zeros_like(vn)) for hh in range(N_GMLP_HEADS)], axis=0)
        s = jnp.dot(gw_ref[0], stacked, preferred_element_type=F32) + gbias_ref[0]
        gm_ref[0, rows, :] = (zu[rows] * s * _silu(zgm[rows])).astype(BF16)


def _boundary(x, branches, mod3, per_batch_mod, p, out_layer, in_layer, rope_tabs, kv_buf, alpha, tm):
    b, l, d = x.shape
    tm = _row_tile(l, tm)
    has_out, has_in = out_layer is not None, in_layer is not None
    emit_kv, aliases = kv_buf is not None and has_in, {}
    row = lambda w: pl.BlockSpec((1, tm, w), lambda bi, i: (bi, i, 0))
    const2 = lambda shp: pl.BlockSpec(shp, lambda bi, i: (0, 0))
    per_layer = lambda shp, layer: pl.BlockSpec((1,) + shp, lambda bi, i: (layer, 0, 0))
    mod_spec = lambda layer: pl.BlockSpec(
        (1, 1, 3, d), (lambda bi, i: (layer, bi, 0, 0)) if per_batch_mod else (lambda bi, i: (layer, 0, 0, 0)))
    sds = jax.ShapeDtypeStruct
    in_specs, args, out_specs, out_shape = [row(d)], [x], [], []
    if has_out:
        in_specs += [row(ATT_WIDTH), row(FOURIER_WIDTH), row(GMLP_WIDTH), mod_spec(out_layer),
                     per_layer((d, d), out_layer), per_layer((1, d), out_layer), per_layer((1, d), out_layer)]
        args += list(branches) + [mod3, p["w_out"], p["ln_g"], p["ln_b"]]
        out_specs += [row(d)]
        out_shape += [sds((b, l, d), F32)]
    if has_in:
        n_in = p["w_in"].shape[2]
        in_specs += [mod_spec(in_layer), per_layer((d, n_in), in_layer),
                     per_layer((FOURIER_WIDTH, FOURIER_WIDTH), in_layer),
                     per_layer((FOURIER_WIDTH, FOURIER_WIDTH), in_layer),
                     per_layer((CHUNK, N_GMLP_HEADS * CHUNK), in_layer),
                     per_layer((CHUNK, GMLP_WIDTH), in_layer), const2((GMLP_WIDTH, GMLP_WIDTH)),
                     const2((tm, tm))]
        sel = np.zeros((tm, tm), np.float32)
        sel[np.arange(tm), np.concatenate([np.arange(0, tm, 2), np.arange(1, tm, 2)])] = 1.0
        args += [mod3, p["w_in"], p["fa"], p["fb"], p["gmlp_w"], p["gmlp_bias"], p["avg"],
                 jnp.asarray(sel, BF16)]
        if rope_tabs is not None:
            in_specs += [pl.BlockSpec((tm, HEAD_W), lambda bi, i: (i, 0))] * 3
            args += list(rope_tabs)
        out_shape += [sds((b, l, ATT_WIDTH), BF16)] * 2 + [
            sds((b, ATT_WIDTH, l), BF16), sds((b, l, ATT_WIDTH), F32),
            sds((b, 2, 2, l // 2, FOURIER_WIDTH), BF16), sds((b, l, FOURIER_WIDTH), F32),
            sds((b, l, GMLP_WIDTH), BF16)]
        out_specs += [row(ATT_WIDTH)] * 2 + [
            pl.BlockSpec((1, ATT_WIDTH, tm), lambda bi, i: (bi, 0, i)), row(ATT_WIDTH),
            pl.BlockSpec((1, 2, 2, tm // 2, FOURIER_WIDTH), lambda bi, i: (bi, 0, 0, i, 0)),
            row(FOURIER_WIDTH), row(GMLP_WIDTH)]
        if emit_kv:
            aliases = {len(args): len(out_shape), len(args) + 1: len(out_shape) + 1}
            in_specs += [pl.BlockSpec(memory_space=pl.ANY)] * 2
            args += list(kv_buf)
            out_shape += [sds(kv_buf[0].shape, F32)] * 2
            out_specs += [pl.BlockSpec((1, 1, N_ATT_HEADS, tm, HEAD_W),
                                       lambda bi, i: (bi, in_layer, 0, i, 0))] * 2
    kern = functools.partial(_boundary_kernel, has_out=has_out, has_in=has_in,
                             rope=rope_tabs is not None, emit_kv=emit_kv, alpha=alpha,
                             n_chunks=tm // CHUNK)

    return pl.pallas_call(
        kern, grid=(b, l // tm), in_specs=in_specs, out_specs=out_specs, out_shape=out_shape,
        input_output_aliases=aliases,
        name=f"boundary_{l}_{int(has_out)}{int(has_in)}",
        compiler_params=_cparams("parallel", "parallel"),
    )(*args)


def _attn_kernel(*refs, lam_init, has_cache, ck, n_new, past, tq, nq, hp):
    if has_cache:
        (q_ref, k_ref, vt_ref, pk_ref, pv_ref, g_ref, lq1_ref, lk1_ref, lq2_ref, lk2_ref,
         sw_ref, o_ref, p_ref, lv_ref, pend_ref, bad_ref, pvt_ref) = refs
    else:
        (q_ref, k_ref, vt_ref, g_ref, lq1_ref, lk1_ref, lq2_ref, lk2_ref,
         sw_ref, o_ref, p_ref, lv_ref, pend_ref, bad_ref) = refs

    chunks = []
    if has_cache:
        chunks.append((True, 0, 0, past))
    for j in range(n_new):
        chunks.append((False, j * ck, past + j * ck, ck))

    if has_cache:
        pvt_ref[...] = pv_ref[0, 0, 0].T.astype(BF16)

    lam = (jnp.exp(jnp.sum(lq1_ref[0] * lk1_ref[0], axis=-1, keepdims=True))
           - jnp.exp(jnp.sum(lq2_ref[0] * lk2_ref[0], axis=-1, keepdims=True)) + lam_init)

    def rows_of(t):
        return pl.ds(pl.multiple_of(t * tq, tq), tq)

    def head(hh):
        cs = slice(hh * HEAD_W, (hh + 1) * HEAD_W)
        so = 2 * hh

        def keys(c):
            return pk_ref[0, 0, 0].astype(BF16) if c[0] else k_ref[0, c[1]:c[1] + c[3], cs]

        def vals_t(c):
            return pvt_ref[...] if c[0] else vt_ref[0, cs, c[1]:c[1] + c[3]]

        def scores_t(qq, c):
            return lax.dot_general(keys(c), qq, (((1,), (1,)), ((), ())), preferred_element_type=F32)

        def stacked_q(t):
            q = q_ref[0, rows_of(t), cs]
            lane = lax.broadcasted_iota(jnp.int32, q.shape, 1)
            zero = jnp.zeros_like(q)
            return jnp.concatenate([jnp.where(lane < HEAD_DIM, q, zero),
                                    jnp.where(lane >= HEAD_DIM, q, zero)], axis=0)

        def scores_phase(t, slot):
            qq = stacked_q(t)
            s0 = scores_t(qq, chunks[0])
            stab = jnp.max(s0, axis=0, keepdims=True)
            lv = jnp.zeros((1, 2 * tq), F32)
            for ci, c in enumerate(chunks):
                s = s0 if ci == 0 else scores_t(qq, c)
                p = jnp.exp2(s - stab)
                p_ref[so + slot, c[2]:c[2] + c[3], :] = p
                lv = lv + jnp.sum(p, axis=0, keepdims=True)
            lv_ref[so + slot] = jnp.broadcast_to(lv, (8, 2 * tq))

        def values_phase(slot):
            lsum = lv_ref[so + slot][0:1, :]
            l1, l2 = lsum[:, :tq], lsum[:, tq:]
            rho = lam * l1 / l2
            acc = jnp.zeros((HEAD_W, tq), F32)
            for c in chunks:
                a = (p_ref[so + slot, c[2]:c[2] + c[3], 0:tq]
                     - rho * p_ref[so + slot, c[2]:c[2] + c[3], tq:2 * tq]).astype(BF16)
                acc = acc + jnp.dot(vals_t(c), a, preferred_element_type=F32)
            return (acc / l1).T

        def finish(o, rows):
            o = (o * lax.rsqrt(jnp.mean(o * o, axis=-1, keepdims=True) + RMS_EPS)
                 * sw_ref[0] * (1.0 - lam_init))
            return o * _silu(g_ref[0, rows, cs])

        def emit(t, slot):
            rows = rows_of(t)
            out = finish(pend_ref[so + slot], rows)
            o_ref[0, rows, cs] = out.astype(BF16)
            bad_ref[hh * nq + t] = jnp.sum(jnp.where(jnp.isfinite(out), 0.0, 1.0))

        def keep(slot, value_fn):
            pend_ref[so + slot] = value_fn()

        def fused(t, slot_s, slot_v):
            qq = stacked_q(t)
            s0 = scores_t(qq, chunks[0])
            stab = jnp.max(s0, axis=0, keepdims=True)
            lv = jnp.zeros((1, 2 * tq), F32)
            lsum = lv_ref[so + slot_v][0:1, :]
            l1, l2 = lsum[:, :tq], lsum[:, tq:]
            rho = lam * l1 / l2
            acc = jnp.zeros((HEAD_W, tq), F32)
            for ci, c in enumerate(chunks):
                s = s0 if ci == 0 else scores_t(qq, c)
                p = jnp.exp2(s - stab)
                p_ref[so + slot_s, c[2]:c[2] + c[3], :] = p
                lv = lv + jnp.sum(p, axis=0, keepdims=True)
                a = (p_ref[so + slot_v, c[2]:c[2] + c[3], 0:tq]
                     - rho * p_ref[so + slot_v, c[2]:c[2] + c[3], tq:2 * tq]).astype(BF16)
                acc = acc + jnp.dot(vals_t(c), a, preferred_element_type=F32)
            lv_ref[so + slot_s] = jnp.broadcast_to(lv, (8, 2 * tq))
            pend_ref[so + slot_v] = (acc / l1).T

        def pair(i, carry):
            t = 2 * i + 2
            emit(t - 2, 0)
            fused(t, 0, 1)
            emit(t - 1, 1)
            fused(t + 1, 1, 0)
            return carry

        stages = [lambda: scores_phase(0, 0), lambda: fused(1, 1, 0)]
        if nq > 2:
            stages.append(lambda: lax.fori_loop(0, nq // 2 - 1, pair, 0))
        stages += [lambda: emit(nq - 2, 0), lambda: keep(1, lambda: values_phase(1)),
                   lambda: emit(nq - 1, 1)]

        def exact():
            def exact_tile(t, carry):
                @pl.when(bad_ref[hh * nq + t] > 0.0)
                def _():
                    qq = stacked_q(t)
                    m = None
                    for c in chunks:
                        s = scores_t(qq, c)
                        p_ref[so, c[2]:c[2] + c[3], :] = s
                        cm = jnp.max(s, axis=0, keepdims=True)
                        m = cm if m is None else jnp.maximum(m, cm)
                    lv = jnp.zeros((1, 2 * tq), F32)
                    for c in chunks:
                        p = jnp.exp2(p_ref[so, c[2]:c[2] + c[3], :] - m)
                        p_ref[so, c[2]:c[2] + c[3], :] = p
                        lv = lv + jnp.sum(p, axis=0, keepdims=True)
                    lv_ref[so] = jnp.broadcast_to(lv, (8, 2 * tq))
                    rows = rows_of(t)
                    o_ref[0, rows, cs] = finish(values_phase(0), rows).astype(BF16)
                return carry

            lax.fori_loop(0, nq, exact_tile, 0)

        return stages, exact

    heads = [head(hh) for hh in range(hp)]
    for stage in zip(*[stages for stages, _ in heads]):
        for run in stage:
            run()
    for _, exact in heads:
        exact()


def _attention(q, k, v, g_att, cache, layer, lam_params, subln_w, lam_init, tq, ck, hp):
    b, l, _ = q.shape
    tq = _row_tile(l, tq)
    ck = _row_tile(l, ck)
    nq = l // tq
    assert nq >= 2 and nq % 2 == 0 and N_ATT_HEADS % hp == 0
    has_cache = cache is not None
    assert not (has_cache and hp != 1)
    past = cache[0].shape[3] if has_cache else 0
    lk = past + l
    seq = pl.BlockSpec((1, l, hp * HEAD_W), lambda bi, h: (bi, 0, h))
    vec = lambda w: pl.BlockSpec((1, 1, w), lambda bi, h: (layer, 0, 0))
    vt_spec = pl.BlockSpec((1, hp * HEAD_W, l), lambda bi, h: (bi, h, 0))
    in_specs = [seq, seq, vt_spec]
    args = [q, k, v]
    if has_cache:
        cspec = pl.BlockSpec((1, 1, 1, past, HEAD_W), lambda bi, h: (bi, layer, h, 0, 0))
        in_specs += [cspec, cspec]
        args += list(cache)
    in_specs += [seq] + [vec(HEAD_DIM)] * 4 + [vec(HEAD_W)]
    args += [g_att] + list(lam_params) + [subln_w]
    return pl.pallas_call(
        functools.partial(_attn_kernel, lam_init=lam_init, has_cache=has_cache, ck=ck,
                          n_new=l // ck, past=past, tq=tq, nq=nq, hp=hp),
        grid=(b, N_ATT_HEADS // hp),
        in_specs=in_specs,
        out_specs=seq,
        out_shape=jax.ShapeDtypeStruct((b, l, ATT_WIDTH), BF16),
        scratch_shapes=[pltpu.VMEM((2 * hp, lk, 2 * tq), F32), pltpu.VMEM((2 * hp, 8, 2 * tq), F32),
                        pltpu.VMEM((2 * hp, tq, HEAD_W), F32), pltpu.SMEM((hp * nq,), F32)]
        + ([pltpu.VMEM((HEAD_W, past), BF16)] if has_cache else []),
        name=f"attn_{l}",
        compiler_params=_cparams("parallel", "parallel"),
    )(*args)


def _fourier_kernel(g_ref, u_ref, gate_ref, o_ref, *, nb, l):
    h = l // 2
    dot = functools.partial(jnp.dot, preferred_element_type=F32)
    for j in range(nb):
        even = dot(g_ref[:, 0:h], u_ref[j, 0, 0]) + dot(g_ref[:, l:l + h], u_ref[j, 1, 0])
        odd = dot(g_ref[:, h:l], u_ref[j, 0, 1]) + dot(g_ref[:, l + h:2 * l], u_ref[j, 1, 1])
        o_ref[j, 0] = ((even + odd) * _silu(gate_ref[j, 0])).astype(BF16)
        o_ref[j, 1] = ((even - odd) * _silu(gate_ref[j, 1])).astype(BF16)


def _fourier(gmat, uf, g_f, tr, nb):
    b, l, w = g_f.shape
    assert b % nb == 0
    h = l // 2
    tr = _row_tile(h, tr)
    halves = pl.BlockSpec((nb, 2, tr, w), lambda bi, i: (bi, 0, i, 0))
    out = pl.pallas_call(
        functools.partial(_fourier_kernel, nb=nb, l=l),
        grid=(b // nb, h // tr),
        in_specs=[pl.BlockSpec((tr, 2 * l), lambda bi, i: (i, 0)),
                  pl.BlockSpec((nb, 2, 2, h, w), lambda bi, i: (bi, 0, 0, 0, 0)),
                  halves],
        out_specs=halves,
        out_shape=jax.ShapeDtypeStruct((b, 2, h, w), BF16),
        name=f"fourier_{l}",
        compiler_params=_cparams("parallel", "arbitrary"),
    )(gmat, uf, g_f.reshape(b, 2, h, w))
    return out.reshape(b, l, w)


def _rope_tables(l):
    n_freq = HEAD_DIM // 4
    rows = l // GRID_W
    row = np.repeat(np.arange(rows, dtype=np.float64), GRID_W)
    col = np.tile(np.arange(GRID_W, dtype=np.float64), rows)
    inv = ROPE_THETA ** (-np.arange(n_freq, dtype=np.float64) / n_freq)
    ang_r, ang_c = row[:, None] * inv, col[:, None] * inv
    zeros = np.zeros_like(ang_r)

    def per_head(r_lo, r_hi, c_lo, c_hi):
        one = np.concatenate([r_lo, r_hi, c_lo, c_hi], axis=-1)
        return jnp.asarray(np.concatenate([one, one], axis=-1), F32)

    cr, sr, cc, sc = np.cos(ang_r), np.sin(ang_r), np.cos(ang_c), np.sin(ang_c)
    cos_t = per_head(cr, cr, cc, cc)
    sa_t = per_head(-sr, zeros, -sc, zeros)
    sb_t = per_head(zeros, sr, zeros, sc)
    return cos_t, sa_t, sb_t


def _run_group(x, mod3, per_batch_mod, p, gmat, rope_tabs, cache, emit_kv, alpha, tiles):
    depth = p["w_in"].shape[0]
    kv_buf = None
    if emit_kv:
        kv_shape = (x.shape[0], depth, N_ATT_HEADS, x.shape[1], HEAD_W)
        kv_buf = (jnp.zeros(kv_shape, F32), jnp.zeros(kv_shape, F32))
    outs = _boundary(x, None, mod3, per_batch_mod, p, None, 0, rope_tabs, kv_buf, alpha, tiles["tm"])
    for layer in range(depth):
        lam_init = 0.8 - 0.6 * math.exp(-0.3 * layer)
        q, k, vt, g_att, uf, g_f, gm = outs[:7]
        if emit_kv:
            kv_buf = tuple(outs[7:9])
        att = _attention(q, k, vt, g_att, cache, layer, p["lam"], p["subln_w"], lam_init,
                         tiles["tq"], tiles["ck"], tiles["hp"])
        fo = _fourier(gmat, uf, g_f, tiles["tr"], tiles["nb"])
        nxt = layer + 1 if layer + 1 < depth else None
        res = _boundary(x, (att, fo, gm), mod3, per_batch_mod, p, layer, nxt, rope_tabs, kv_buf,
                        alpha, tiles["tm"])
        x, outs = res[0], res[1:]
    return x, kv_buf


def kernel(x_prompt, x_sample, c, cache_k, cache_v, c_ctx, w_ada, b_ada, w_in, w_out,
           lam_q1, lam_k1, lam_q2, lam_k2, subln_w, fourier_w, gmlp_ws, gmlp_bs, ln_g, ln_b):
    depth, d, _ = w_ada.shape
    nb_lat = c.shape[0]
    l_ctx, l_lat = x_prompt.shape[1], x_sample.shape[1]
    assert 1 + nb_lat <= MOD_ROWS
    alpha = (2.0 * depth) ** 0.25

    cvec = jnp.concatenate([c_ctx[None, :], c, jnp.zeros((MOD_ROWS - 1 - nb_lat, d), F32)], axis=0)
    mod = _modulation(cvec, w_ada, b_ada)

    fa, fb = _fourier_channel_maps(fourier_w, 1.0 / math.sqrt(FOURIER_GROUP_DIM))
    params = {
        "w_in": w_in.astype(BF16), "w_out": w_out.astype(BF16), "fa": fa, "fb": fb,
        "lam": tuple(t[:, None, :] for t in (lam_q1, lam_k1, lam_q2, lam_k2)),
        "subln_w": subln_w[:, None, :], "ln_g": ln_g[:, None, :], "ln_b": ln_b[:, None, :],
        "gmlp_w": gmlp_ws.astype(BF16).transpose(0, 2, 1, 3).reshape(depth, CHUNK, N_GMLP_HEADS * CHUNK),
        "gmlp_bias": jnp.repeat(gmlp_bs.astype(F32).transpose(0, 2, 1), GMLP_HEAD_DIM, axis=2),
        "avg": jnp.asarray(np.kron(np.eye(N_GMLP_HEADS), np.full((GMLP_HEAD_DIM,) * 2, 1.0 / GMLP_HEAD_DIM)), BF16),
    }
    ctx_tiles = dict(tm=256, tq=128, ck=256, tr=256, nb=2, hp=N_ATT_HEADS)
    lat_tiles = dict(tm=512, tq=256, ck=512, tr=512, nb=2, hp=1)

    xp, (new_k, new_v) = _run_group(x_prompt, mod[:, 0:1].reshape(depth, 1, 3, d), False, params,
                                    _dft_matrix(l_ctx), None, None, True, alpha, ctx_tiles)

    xs, _ = _run_group(x_sample, mod[:, 1:1 + nb_lat].reshape(depth, nb_lat, 3, d), True, params,
                       _dft_matrix(l_lat), _rope_tables(l_lat), (cache_k, cache_v), False, alpha,
                       lat_tiles)
    return (xp, xs, new_k, new_v)
```

```python
import functools
import math

import numpy as np
import jax
import jax.numpy as jnp
from jax import lax
from jax.experimental import pallas as pl
from jax.experimental.pallas import tpu as pltpu

F32 = jnp.float32
BF16 = jnp.bfloat16

GRID_W = 64
HEAD_DIM = 64
N_ATT_HEADS = 4
HEAD_W = 2 * HEAD_DIM
ATT_WIDTH = N_ATT_HEADS * HEAD_W
N_FOURIER_GROUPS = 4
FOURIER_GROUP_DIM = 64
FOURIER_WIDTH = N_FOURIER_GROUPS * FOURIER_GROUP_DIM
N_GMLP_HEADS = 4
GMLP_HEAD_DIM = 64
GMLP_WIDTH = N_GMLP_HEADS * GMLP_HEAD_DIM
CHUNK = 128
ROPE_THETA = 10000.0
LN_EPS = 1e-5
RMS_EPS = 1e-5
QK_SCALE = HEAD_DIM ** -0.5
LOG2_E = math.log2(math.e)

_SEG_NAMES = ("q", "k", "v", "g_att", "f", "g_f", "u", "vm", "g_m")
_SEG_WIDTHS = (ATT_WIDTH,) * 4 + (FOURIER_WIDTH,) * 2 + (GMLP_WIDTH,) * 3
_SEG_START = dict(zip(_SEG_NAMES, np.concatenate([[0], np.cumsum(_SEG_WIDTHS)[:-1]]).tolist()))
_SEG_WIDTH = dict(zip(_SEG_NAMES, _SEG_WIDTHS))

VMEM_LIMIT_BYTES = 52 * 1024 * 1024
VMEM_LIMIT_BOUNDARY_BYTES = 58 * 1024 * 1024
MOD_ROWS = 16
SUB_TILE = 512


def _cparams(*sem, vmem=VMEM_LIMIT_BYTES):
    return pltpu.CompilerParams(dimension_semantics=sem, vmem_limit_bytes=vmem)


def _silu(x):
    return x * jax.nn.sigmoid(x)


def _row_tile(l, want):
    t = min(l, want)
    assert l % t == 0
    return t


def _mod_kernel(c_ref, w_ref, b_ref, o_ref):
    s = _silu(c_ref[...])
    o_ref[0] = jnp.dot(s, w_ref[0], precision=lax.Precision.HIGHEST,
                       preferred_element_type=F32) + b_ref[0]


def _modulation(cvec, w_ada, b_ada):
    depth, d, d3 = w_ada.shape
    tn = 1024
    return pl.pallas_call(
        _mod_kernel,
        grid=(depth, d3 // tn),
        in_specs=[pl.BlockSpec((MOD_ROWS, d), lambda l, j: (0, 0)),
                  pl.BlockSpec((1, d, tn), lambda l, j: (l, 0, j)),
                  pl.BlockSpec((1, 1, tn), lambda l, j: (l, 0, j))],
        out_specs=pl.BlockSpec((1, MOD_ROWS, tn), lambda l, j: (l, 0, j)),
        out_shape=jax.ShapeDtypeStruct((depth, MOD_ROWS, d3), F32),
        name="adaln_mod",
        compiler_params=_cparams("arbitrary", "arbitrary"),
    )(cvec, w_ada, b_ada.reshape(depth, 1, d3))


def _chanmap_kernel(cc_ref, ss_ref, w_ref, a_ref, b_ref, *, scale):
    w = w_ref[0]
    a_ref[0] = (jnp.dot(cc_ref[...], w, precision=lax.Precision.HIGHEST,
                        preferred_element_type=F32) * scale).astype(BF16)
    b_ref[0] = (jnp.dot(ss_ref[...], w, precision=lax.Precision.HIGHEST,
                        preferred_element_type=F32) * scale).astype(BF16)


def _fourier_channel_maps(fourier_w, scale):
    depth = fourier_w.shape[0]
    n = np.arange(FOURIER_GROUP_DIM)
    ang = 2.0 * np.pi * ((n[:, None] * n[None, :]) % FOURIER_GROUP_DIM) / FOURIER_GROUP_DIM
    eye = np.eye(N_FOURIER_GROUPS)
    cc = jnp.asarray(np.kron(eye, np.cos(ang)), F32)
    ss = jnp.asarray(np.kron(eye, np.sin(ang)), F32)
    wbd = jnp.einsum("gh,lgce->lgche", jnp.eye(N_FOURIER_GROUPS, dtype=F32),
                     fourier_w.astype(F32)).reshape(depth, FOURIER_WIDTH, FOURIER_WIDTH)
    w_spec = pl.BlockSpec((1, FOURIER_WIDTH, FOURIER_WIDTH), lambda l: (l, 0, 0))
    c_spec = pl.BlockSpec((FOURIER_WIDTH, FOURIER_WIDTH), lambda l: (0, 0))
    shp = jax.ShapeDtypeStruct((depth, FOURIER_WIDTH, FOURIER_WIDTH), BF16)
    return pl.pallas_call(
        functools.partial(_chanmap_kernel, scale=scale),
        grid=(depth,),
        in_specs=[c_spec, c_spec, w_spec],
        out_specs=[w_spec, w_spec],
        out_shape=[shp, shp],
        name="fourier_chanmap",
        compiler_params=_cparams("arbitrary"),
    )(cc, ss, wbd)


def _dft_kernel(ca_ref, sa_ref, cb_ref, sb_ref, g_ref, *, l):
    ca, sa = ca_ref[0], sa_ref[0]
    cb, sb = cb_ref[...], sb_ref[...]
    norm = 1.0 / math.sqrt(l)
    g_ref[:, 0:l] = ((ca * cb - sa * sb) * norm).astype(BF16)
    g_ref[:, l:2 * l] = ((sa * cb + ca * sb) * -norm).astype(BF16)


def _dft_matrix(l):
    assert l % (2 * GRID_W) == 0
    na = l // GRID_W
    k = np.concatenate([np.arange(0, l, 2), np.arange(1, l, 2)]).astype(np.int64)
    ang_a = ((np.arange(na // 2)[:, None] * k[None, :]) % na) * (2.0 * np.pi / na)
    ang_b = ((np.arange(GRID_W)[:, None] * k[None, :]) % l) * (2.0 * np.pi / l)
    ca = jnp.asarray(np.cos(ang_a).reshape(na // 2, 1, l), F32)
    sa = jnp.asarray(np.sin(ang_a).reshape(na // 2, 1, l), F32)
    cb, sb = jnp.asarray(np.cos(ang_b), F32), jnp.asarray(np.sin(ang_b), F32)
    a_spec = pl.BlockSpec((1, 1, l), lambda i: (i, 0, 0))
    b_spec = pl.BlockSpec((GRID_W, l), lambda i: (0, 0))
    return pl.pallas_call(
        functools.partial(_dft_kernel, l=l),
        grid=(na // 2,),
        in_specs=[a_spec, a_spec, b_spec, b_spec],
        out_specs=pl.BlockSpec((GRID_W, 2 * l), lambda i: (i, 0)),
        out_shape=jax.ShapeDtypeStruct((l // 2, 2 * l), BF16),
        name=f"dft_matrix_{l}",
        compiler_params=_cparams("arbitrary"),
    )(ca, sa, cb, sb)


def _layernorm(v, eps=LN_EPS):
    mu = jnp.mean(v, axis=-1, keepdims=True)
    vc = v - mu
    return vc * lax.rsqrt(jnp.mean(vc * vc, axis=-1, keepdims=True) + eps)


def _boundary_kernel(*refs, has_out, has_in, rope, emit_kv, alpha, n_sub):
    refs = list(refs)
    take = lambda n: [refs.pop(0) for _ in range(n)]
    (x_ref,) = take(1)
    if has_out:
        att_ref, fo_ref, gmo_ref, modo_ref, wout_ref, lg_ref, lb_ref = take(7)
    if has_in:
        modi_ref, win_ref, fa_ref, fb_ref, gw_ref, gbias_ref, avg_ref, sel_ref = take(8)
        if rope:
            cos_ref, sa_ref, sb_ref = take(3)
        if emit_kv:
            take(2)
    if has_out:
        (xo_ref,) = take(1)
    if has_in:
        q_ref, k_ref, vt_ref, ga_ref, uf_ref, gf_ref, gm_ref = take(7)
        if emit_kv:
            kf_ref, vf_ref = take(2)

    tm = x_ref.shape[1]

    def rotate(z, rs):
        return (z * cos_ref[rs, :] + pltpu.roll(z, HEAD_W - 16, 1) * sa_ref[rs, :]
                + pltpu.roll(z, 16, 1) * sb_ref[rs, :])

    def sub_tile(r0, ts):
        rs = slice(r0, r0 + ts)
        st = {}

        def s_finish():
            x = x_ref[0, rs, :]
            if has_out:
                a0, a1 = ATT_WIDTH, ATT_WIDTH + FOURIER_WIDTH
                y = (jnp.dot(att_ref[0, rs, :], wout_ref[0, 0:a0, :], preferred_element_type=F32)
                     + jnp.dot(fo_ref[0, rs, :], wout_ref[0, a0:a1, :], preferred_element_type=F32)
                     + jnp.dot(gmo_ref[0, rs, :], wout_ref[0, a1:, :], preferred_element_type=F32))
                gate = modo_ref[0, 0, 2:3, :] * (1.0 / alpha)
                x = _layernorm(x + gate * y, LN_EPS / (alpha * alpha)) * lg_ref[0] + lb_ref[0]
                xo_ref[0, rs, :] = x
            if has_in:
                shift, scale = modi_ref[0, 0, 0:1, :], modi_ref[0, 0, 1:2, :]
                st["h"] = (_layernorm(x) * (1.0 + scale) + shift).astype(BF16)

        def proj(name):
            c0 = _SEG_START[name]
            return jnp.dot(st["h"], win_ref[0, :, c0:c0 + _SEG_WIDTH[name]], preferred_element_type=F32)

        def s_qkv():
            zq, zk, zv = proj("q"), proj("k"), proj("v")
            for hh in range(N_ATT_HEADS):
                sl = slice(hh * HEAD_W, (hh + 1) * HEAD_W)
                qh, kh = zq[:, sl], zk[:, sl]
                if emit_kv:
                    kf_ref[0, 0, hh, rs, :] = kh
                    vf_ref[0, 0, hh, rs, :] = zv[:, sl]
                if rope:
                    qh, kh = rotate(qh, rs), rotate(kh, rs)
                q_ref[0, rs, sl] = (qh * (QK_SCALE * LOG2_E)).astype(BF16)
                k_ref[0, rs, sl] = kh.astype(BF16)
            vt_ref[0, :, rs] = zv.T.astype(BF16)

        def s_fourier():
            ga_ref[0, rs, :] = proj("g_att")
            f = jnp.dot(sel_ref[...], proj("f").astype(BF16), preferred_element_type=F32).astype(BF16)
            hs = slice(r0 // 2, (r0 + ts) // 2)
            for j, fm_ref in enumerate((fa_ref, fb_ref)):
                u = jnp.dot(f, fm_ref[0], preferred_element_type=F32).astype(BF16)
                uf_ref[0, j, 0, hs, :] = u[:ts // 2]
                uf_ref[0, j, 1, hs, :] = u[ts // 2:]
            gf_ref[0, rs, :] = proj("g_f")

        def s_gmlp():
            zu, zvm, zgm = proj("u"), proj("vm"), proj("g_m")
            avg = avg_ref[...]

            def group_mean(v):
                hi = v.astype(BF16)
                lo = (v - hi.astype(F32)).astype(BF16)
                return (jnp.dot(hi, avg, preferred_element_type=F32)
                        + jnp.dot(lo, avg, preferred_element_type=F32))

            lane = lax.broadcasted_iota(jnp.int32, (CHUNK, GMLP_WIDTH), 1)
            head_mask = [(lane >= hh * GMLP_HEAD_DIM) & (lane < (hh + 1) * GMLP_HEAD_DIM)
                         for hh in range(N_GMLP_HEADS)]
            vc_all = zvm - group_mean(zvm)
            vn_all = (vc_all * lax.rsqrt(group_mean(vc_all * vc_all) + LN_EPS)).astype(BF16)
            for c in range(ts // CHUNK):
                rows = slice(c * CHUNK, (c + 1) * CHUNK)
                vn = vn_all[rows]
                stacked = jnp.concatenate(
                    [jnp.where(head_mask[hh], vn, jnp.zeros_like(vn)) for hh in range(N_GMLP_HEADS)], axis=0)
                s = jnp.dot(gw_ref[0], stacked, preferred_element_type=F32) + gbias_ref[0]
                gm_ref[0, r0 + c * CHUNK:r0 + (c + 1) * CHUNK, :] = (
                    zu[rows] * s * _silu(zgm[rows])).astype(BF16)

        return [s_finish] + ([s_qkv, s_fourier, s_gmlp] if has_in else [])

    ts = tm // n_sub
    subs = [sub_tile(j * ts, ts) for j in range(n_sub)]
    n_stage = len(subs[0])
    for step in range(n_stage + n_sub - 1):
        for j in range(n_sub):
            if 0 <= step - j < n_stage:
                subs[j][step - j]()


def _boundary(x, branches, mod3, per_batch_mod, p, out_layer, in_layer, rope_tabs, kv_buf, alpha, tm):
    b, l, d = x.shape
    tm = _row_tile(l, tm)
    has_out, has_in = out_layer is not None, in_layer is not None
    emit_kv, aliases = kv_buf is not None and has_in, {}
    n_sub = tm // SUB_TILE if tm > SUB_TILE else 1
    tsel = tm // n_sub
    row = lambda w: pl.BlockSpec((1, tm, w), lambda bi, i: (bi, i, 0))
    const2 = lambda shp: pl.BlockSpec(shp, lambda bi, i: (0, 0))
    per_layer = lambda shp, layer: pl.BlockSpec((1,) + shp, lambda bi, i: (layer, 0, 0),
                                                pipeline_mode=pl.Buffered(1))
    mod_spec = lambda layer: pl.BlockSpec(
        (1, 1, 3, d), (lambda bi, i: (layer, bi, 0, 0)) if per_batch_mod else (lambda bi, i: (layer, 0, 0, 0)))
    sds = jax.ShapeDtypeStruct
    in_specs, args, out_specs, out_shape = [row(d)], [x], [], []
    if has_out:
        in_specs += [row(ATT_WIDTH), row(FOURIER_WIDTH), row(GMLP_WIDTH), mod_spec(out_layer),
                     per_layer((d, d), out_layer), per_layer((1, d), out_layer), per_layer((1, d), out_layer)]
        args += list(branches) + [mod3, p["w_out"], p["ln_g"], p["ln_b"]]
        out_specs += [row(d)]
        out_shape += [sds((b, l, d), F32)]
    if has_in:
        n_in = p["w_in"].shape[2]
        in_specs += [mod_spec(in_layer), per_layer((d, n_in), in_layer),
                     per_layer((FOURIER_WIDTH, FOURIER_WIDTH), in_layer),
                     per_layer((FOURIER_WIDTH, FOURIER_WIDTH), in_layer),
                     per_layer((CHUNK, N_GMLP_HEADS * CHUNK), in_layer),
                     per_layer((CHUNK, GMLP_WIDTH), in_layer), const2((GMLP_WIDTH, GMLP_WIDTH)),
                     const2((tsel, tsel))]
        sel = np.zeros((tsel, tsel), np.float32)
        sel[np.arange(tsel), np.concatenate([np.arange(0, tsel, 2), np.arange(1, tsel, 2)])] = 1.0
        args += [mod3, p["w_in"], p["fa"], p["fb"], p["gmlp_w"], p["gmlp_bias"], p["avg"],
                 jnp.asarray(sel, BF16)]
        if rope_tabs is not None:
            in_specs += [pl.BlockSpec((tm, HEAD_W), lambda bi, i: (i, 0))] * 3
            args += list(rope_tabs)
        out_shape += [sds((b, l, ATT_WIDTH), BF16)] * 2 + [
            sds((b, ATT_WIDTH, l), BF16), sds((b, l, ATT_WIDTH), F32),
            sds((b, 2, 2, l // 2, FOURIER_WIDTH), BF16), sds((b, l, FOURIER_WIDTH), F32),
            sds((b, l, GMLP_WIDTH), BF16)]
        out_specs += [row(ATT_WIDTH)] * 2 + [
            pl.BlockSpec((1, ATT_WIDTH, tm), lambda bi, i: (bi, 0, i)), row(ATT_WIDTH),
            pl.BlockSpec((1, 2, 2, tm // 2, FOURIER_WIDTH), lambda bi, i: (bi, 0, 0, i, 0)),
            row(FOURIER_WIDTH), row(GMLP_WIDTH)]
        if emit_kv:
            aliases = {len(args): len(out_shape), len(args) + 1: len(out_shape) + 1}
            in_specs += [pl.BlockSpec(memory_space=pl.ANY)] * 2
            args += list(kv_buf)
            out_shape += [sds(kv_buf[0].shape, F32)] * 2
            out_specs += [pl.BlockSpec((1, 1, N_ATT_HEADS, tm, HEAD_W),
                                       lambda bi, i: (bi, in_layer, 0, i, 0))] * 2
    kern = functools.partial(_boundary_kernel, has_out=has_out, has_in=has_in,
                             rope=rope_tabs is not None, emit_kv=emit_kv, alpha=alpha, n_sub=n_sub)

    return pl.pallas_call(
        kern, grid=(b, l // tm), in_specs=in_specs, out_specs=out_specs, out_shape=out_shape,
        input_output_aliases=aliases,
        name=f"boundary_{l}_{int(has_out)}{int(has_in)}",
        compiler_params=_cparams("parallel", "parallel", vmem=VMEM_LIMIT_BOUNDARY_BYTES),
    )(*args)


def _attn_kernel(*refs, lam_init, has_cache, ck, n_new, past, tq, nq, hp):
    if has_cache:
        (q_ref, k_ref, vt_ref, pk_ref, pv_ref, g_ref, lq1_ref, lk1_ref, lq2_ref, lk2_ref,
         sw_ref, o_ref, p_ref, lv_ref, pend_ref, bad_ref, pvt_ref) = refs
    else:
        (q_ref, k_ref, vt_ref, g_ref, lq1_ref, lk1_ref, lq2_ref, lk2_ref,
         sw_ref, o_ref, p_ref, lv_ref, pend_ref, bad_ref) = refs

    chunks = []
    if has_cache:
        chunks.append((True, 0, 0, past))
    for j in range(n_new):
        chunks.append((False, j * ck, past + j * ck, ck))

    if has_cache:
        pvt_ref[...] = pv_ref[0, 0, 0].T.astype(BF16)

    lam = (jnp.exp(jnp.sum(lq1_ref[0] * lk1_ref[0], axis=-1, keepdims=True))
           - jnp.exp(jnp.sum(lq2_ref[0] * lk2_ref[0], axis=-1, keepdims=True)) + lam_init)

    def rows_of(t):
        return pl.ds(pl.multiple_of(t * tq, tq), tq)

    def head(hh):
        cs = slice(hh * HEAD_W, (hh + 1) * HEAD_W)
        so = 2 * hh

        def keys(c):
            return pk_ref[0, 0, 0].astype(BF16) if c[0] else k_ref[0, c[1]:c[1] + c[3], cs]

        def vals_t(c):
            return pvt_ref[...] if c[0] else vt_ref[0, cs, c[1]:c[1] + c[3]]

        def scores_t(qq, c):
            return lax.dot_general(keys(c), qq, (((1,), (1,)), ((), ())), preferred_element_type=F32)

        def stacked_q(t):
            q = q_ref[0, rows_of(t), cs]
            lane = lax.broadcasted_iota(jnp.int32, q.shape, 1)
            zero = jnp.zeros_like(q)
            return jnp.concatenate([jnp.where(lane < HEAD_DIM, q, zero),
                                    jnp.where(lane >= HEAD_DIM, q, zero)], axis=0)

        def scores_phase(t, slot):
            qq = stacked_q(t)
            s0 = scores_t(qq, chunks[0])
            stab = jnp.max(s0, axis=0, keepdims=True)
            lv = jnp.zeros((1, 2 * tq), F32)
            for ci, c in enumerate(chunks):
                s = s0 if ci == 0 else scores_t(qq, c)
                p = jnp.exp2(s - stab)
                p_ref[so + slot, c[2]:c[2] + c[3], :] = p
                lv = lv + jnp.sum(p, axis=0, keepdims=True)
            lv_ref[so + slot] = jnp.broadcast_to(lv, (8, 2 * tq))

        def values_phase(slot):
            lsum = lv_ref[so + slot][0:1, :]
            l1, l2 = lsum[:, :tq], lsum[:, tq:]
            rho = lam * l1 / l2
            acc = jnp.zeros((HEAD_W, tq), F32)
            for c in chunks:
                a = (p_ref[so + slot, c[2]:c[2] + c[3], 0:tq]
                     - rho * p_ref[so + slot, c[2]:c[2] + c[3], tq:2 * tq]).astype(BF16)
                acc = acc + jnp.dot(vals_t(c), a, preferred_element_type=F32)
            return (acc / l1).T

        def finish(o, rows):
            o = (o * lax.rsqrt(jnp.mean(o * o, axis=-1, keepdims=True) + RMS_EPS)
                 * sw_ref[0] * (1.0 - lam_init))
            return o * _silu(g_ref[0, rows, cs])

        def emit(t, slot):
            rows = rows_of(t)
            out = finish(pend_ref[so + slot], rows)
            o_ref[0, rows, cs] = out.astype(BF16)
            bad_ref[hh * nq + t] = jnp.sum(jnp.where(jnp.isfinite(out), 0.0, 1.0))

        def keep(slot, value_fn):
            pend_ref[so + slot] = value_fn()

        def fused(t, slot_s, slot_v):
            qq = stacked_q(t)
            s0 = scores_t(qq, chunks[0])
            stab = jnp.max(s0, axis=0, keepdims=True)
            lv = jnp.zeros((1, 2 * tq), F32)
            lsum = lv_ref[so + slot_v][0:1, :]
            l1, l2 = lsum[:, :tq], lsum[:, tq:]
            rho = lam * l1 / l2
            acc = jnp.zeros((HEAD_W, tq), F32)
            for ci, c in enumerate(chunks):
                s = s0 if ci == 0 else scores_t(qq, c)
                p = jnp.exp2(s - stab)
                p_ref[so + slot_s, c[2]:c[2] + c[3], :] = p
                lv = lv + jnp.sum(p, axis=0, keepdims=True)
                a = (p_ref[so + slot_v, c[2]:c[2] + c[3], 0:tq]
                     - rho * p_ref[so + slot_v, c[2]:c[2] + c[3], tq:2 * tq]).astype(BF16)
                acc = acc + jnp.dot(vals_t(c), a, preferred_element_type=F32)
            lv_ref[so + slot_s] = jnp.broadcast_to(lv, (8, 2 * tq))
            pend_ref[so + slot_v] = (acc / l1).T

        def pair(i, carry):
            t = 2 * i + 2
            emit(t - 2, 0)
            fused(t, 0, 1)
            emit(t - 1, 1)
            fused(t + 1, 1, 0)
            return carry

        stages = [lambda: scores_phase(0, 0), lambda: fused(1, 1, 0)]
        if nq > 2:
            stages.append(lambda: lax.fori_loop(0, nq // 2 - 1, pair, 0))
        stages += [lambda: emit(nq - 2, 0), lambda: keep(1, lambda: values_phase(1)),
                   lambda: emit(nq - 1, 1)]

        def exact():
            def exact_tile(t, carry):
                @pl.when(bad_ref[hh * nq + t] > 0.0)
                def _():
                    qq = stacked_q(t)
                    m = None
                    for c in chunks:
                        s = scores_t(qq, c)
                        p_ref[so, c[2]:c[2] + c[3], :] = s
                        cm = jnp.max(s, axis=0, keepdims=True)
                        m = cm if m is None else jnp.maximum(m, cm)
                    lv = jnp.zeros((1, 2 * tq), F32)
                    for c in chunks:
                        p = jnp.exp2(p_ref[so, c[2]:c[2] + c[3], :] - m)
                        p_ref[so, c[2]:c[2] + c[3], :] = p
                        lv = lv + jnp.sum(p, axis=0, keepdims=True)
                    lv_ref[so] = jnp.broadcast_to(lv, (8, 2 * tq))
                    rows = rows_of(t)
                    o_ref[0, rows, cs] = finish(values_phase(0), rows).astype(BF16)
                return carry

            lax.fori_loop(0, nq, exact_tile, 0)

        return stages, exact

    heads = [head(hh) for hh in range(hp)]
    for stage in zip(*[stages for stages, _ in heads]):
        for run in stage:
            run()
    for _, exact in heads:
        exact()


def _attention(q, k, v, g_att, cache, layer, lam_params, subln_w, lam_init, tq, ck, hp):
    b, l, _ = q.shape
    tq = _row_tile(l, tq)
    ck = _row_tile(l, ck)
    nq = l // tq
    assert nq >= 2 and nq % 2 == 0 and N_ATT_HEADS % hp == 0
    has_cache = cache is not None
    assert not (has_cache and hp != 1)
    past = cache[0].shape[3] if has_cache else 0
    lk = past + l
    seq = pl.BlockSpec((1, l, hp * HEAD_W), lambda bi, h: (bi, 0, h))
    vec = lambda w: pl.BlockSpec((1, 1, w), lambda bi, h: (layer, 0, 0))
    vt_spec = pl.BlockSpec((1, hp * HEAD_W, l), lambda bi, h: (bi, h, 0))
    in_specs = [seq, seq, vt_spec]
    args = [q, k, v]
    if has_cache:
        cspec = pl.BlockSpec((1, 1, 1, past, HEAD_W), lambda bi, h: (bi, layer, h, 0, 0))
        in_specs += [cspec, cspec]
        args += list(cache)
    in_specs += [seq] + [vec(HEAD_DIM)] * 4 + [vec(HEAD_W)]
    args += [g_att] + list(lam_params) + [subln_w]
    return pl.pallas_call(
        functools.partial(_attn_kernel, lam_init=lam_init, has_cache=has_cache, ck=ck,
                          n_new=l // ck, past=past, tq=tq, nq=nq, hp=hp),
        grid=(b, N_ATT_HEADS // hp),
        in_specs=in_specs,
        out_specs=seq,
        out_shape=jax.ShapeDtypeStruct((b, l, ATT_WIDTH), BF16),
        scratch_shapes=[pltpu.VMEM((2 * hp, lk, 2 * tq), F32), pltpu.VMEM((2 * hp, 8, 2 * tq), F32),
                        pltpu.VMEM((2 * hp, tq, HEAD_W), F32), pltpu.SMEM((hp * nq,), F32)]
        + ([pltpu.VMEM((HEAD_W, past), BF16)] if has_cache else []),
        name=f"attn_{l}",
        compiler_params=_cparams("parallel", "parallel"),
    )(*args)


def _fourier_kernel(g_ref, u_ref, gate_ref, o_ref, *, nb, l):
    h = l // 2
    dot = functools.partial(jnp.dot, preferred_element_type=F32)
    for j in range(nb):
        even = dot(g_ref[:, 0:h], u_ref[j, 0, 0]) + dot(g_ref[:, l:l + h], u_ref[j, 1, 0])
        odd = dot(g_ref[:, h:l], u_ref[j, 0, 1]) + dot(g_ref[:, l + h:2 * l], u_ref[j, 1, 1])
        o_ref[j, 0] = ((even + odd) * _silu(gate_ref[j, 0])).astype(BF16)
        o_ref[j, 1] = ((even - odd) * _silu(gate_ref[j, 1])).astype(BF16)


def _fourier(gmat, uf, g_f, tr, nb):
    b, l, w = g_f.shape
    assert b % nb == 0
    h = l // 2
    tr = _row_tile(h, tr)
    halves = pl.BlockSpec((nb, 2, tr, w), lambda bi, i: (bi, 0, i, 0))
    out = pl.pallas_call(
        functools.partial(_fourier_kernel, nb=nb, l=l),
        grid=(b // nb, h // tr),
        in_specs=[pl.BlockSpec((tr, 2 * l), lambda bi, i: (i, 0)),
                  pl.BlockSpec((nb, 2, 2, h, w), lambda bi, i: (bi, 0, 0, 0, 0)),
                  halves],
        out_specs=halves,
        out_shape=jax.ShapeDtypeStruct((b, 2, h, w), BF16),
        name=f"fourier_{l}",
        compiler_params=_cparams("parallel", "arbitrary"),
    )(gmat, uf, g_f.reshape(b, 2, h, w))
    return out.reshape(b, l, w)


def _rope_tables(l):
    n_freq = HEAD_DIM // 4
    rows = l // GRID_W
    row = np.repeat(np.arange(rows, dtype=np.float64), GRID_W)
    col = np.tile(np.arange(GRID_W, dtype=np.float64), rows)
    inv = ROPE_THETA ** (-np.arange(n_freq, dtype=np.float64) / n_freq)
    ang_r, ang_c = row[:, None] * inv, col[:, None] * inv
    zeros = np.zeros_like(ang_r)

    def per_head(r_lo, r_hi, c_lo, c_hi):
        one = np.concatenate([r_lo, r_hi, c_lo, c_hi], axis=-1)
        return jnp.asarray(np.concatenate([one, one], axis=-1), F32)

    cr, sr, cc, sc = np.cos(ang_r), np.sin(ang_r), np.cos(ang_c), np.sin(ang_c)
    cos_t = per_head(cr, cr, cc, cc)
    sa_t = per_head(-sr, zeros, -sc, zeros)
    sb_t = per_head(zeros, sr, zeros, sc)
    return cos_t, sa_t, sb_t


def _run_group(x, mod3, per_batch_mod, p, gmat, rope_tabs, cache, emit_kv, alpha, tiles):
    depth = p["w_in"].shape[0]
    kv_buf = None
    if emit_kv:
        kv_shape = (x.shape[0], depth, N_ATT_HEADS, x.shape[1], HEAD_W)
        kv_buf = (jnp.zeros(kv_shape, F32), jnp.zeros(kv_shape, F32))
    outs = _boundary(x, None, mod3, per_batch_mod, p, None, 0, rope_tabs, kv_buf, alpha, tiles["tm"])
    for layer in range(depth):
        lam_init = 0.8 - 0.6 * math.exp(-0.3 * layer)
        q, k, vt, g_att, uf, g_f, gm = outs[:7]
        if emit_kv:
            kv_buf = tuple(outs[7:9])
        att = _attention(q, k, vt, g_att, cache, layer, p["lam"], p["subln_w"], lam_init,
                         tiles["tq"], tiles["ck"], tiles["hp"])
        fo = _fourier(gmat, uf, g_f, tiles["tr"], tiles["nb"])
        nxt = layer + 1 if layer + 1 < depth else None
        res = _boundary(x, (att, fo, gm), mod3, per_batch_mod, p, layer, nxt, rope_tabs, kv_buf,
                        alpha, tiles["tm"])
        x, outs = res[0], res[1:]
    return x, kv_buf


def kernel(x_prompt, x_sample, c, cache_k, cache_v, c_ctx, w_ada, b_ada, w_in, w_out,
           lam_q1, lam_k1, lam_q2, lam_k2, subln_w, fourier_w, gmlp_ws, gmlp_bs, ln_g, ln_b):
    depth, d, _ = w_ada.shape
    nb_lat = c.shape[0]
    l_ctx, l_lat = x_prompt.shape[1], x_sample.shape[1]
    assert 1 + nb_lat <= MOD_ROWS
    alpha = (2.0 * depth) ** 0.25

    cvec = jnp.concatenate([c_ctx[None, :], c, jnp.zeros((MOD_ROWS - 1 - nb_lat, d), F32)], axis=0)
    mod = _modulation(cvec, w_ada, b_ada)

    fa, fb = _fourier_channel_maps(fourier_w, 1.0 / math.sqrt(FOURIER_GROUP_DIM))
    params = {
        "w_in": w_in.astype(BF16), "w_out": w_out.astype(BF16), "fa": fa, "fb": fb,
        "lam": tuple(t[:, None, :] for t in (lam_q1, lam_k1, lam_q2, lam_k2)),
        "subln_w": subln_w[:, None, :], "ln_g": ln_g[:, None, :], "ln_b": ln_b[:, None, :],
        "gmlp_w": gmlp_ws.astype(BF16).transpose(0, 2, 1, 3).reshape(depth, CHUNK, N_GMLP_HEADS * CHUNK),
        "gmlp_bias": jnp.repeat(gmlp_bs.astype(F32).transpose(0, 2, 1), GMLP_HEAD_DIM, axis=2),
        "avg": jnp.asarray(np.kron(np.eye(N_GMLP_HEADS), np.full((GMLP_HEAD_DIM,) * 2, 1.0 / GMLP_HEAD_DIM)), BF16),
    }
    ctx_tiles = dict(tm=256, tq=128, ck=256, tr=256, nb=2, hp=N_ATT_HEADS)
    lat_tiles = dict(tm=1024, tq=256, ck=512, tr=512, nb=2, hp=1)

    xp, (new_k, new_v) = _run_group(x_prompt, mod[:, 0:1].reshape(depth, 1, 3, d), False, params,
                                    _dft_matrix(l_ctx), None, None, True, alpha, ctx_tiles)

    xs, _ = _run_group(x_sample, mod[:, 1:1 + nb_lat].reshape(depth, nb_lat, 3, d), True, params,
                       _dft_matrix(l_lat), _rope_tables(l_lat), (cache_k, cache_v), False, alpha,
                       lat_tiles)
    return (xp, xs, new_k, new_v)
```

```python
import functools
import math

import numpy as np
import jax
import jax.numpy as jnp
from jax import lax
from jax.experimental import pallas as pl
from jax.experimental.pallas import tpu as pltpu

F32 = jnp.float32
BF16 = jnp.bfloat16

GRID_W = 64
HEAD_DIM = 64
N_ATT_HEADS = 4
HEAD_W = 2 * HEAD_DIM
ATT_WIDTH = N_ATT_HEADS * HEAD_W
N_FOURIER_GROUPS = 4
FOURIER_GROUP_DIM = 64
FOURIER_WIDTH = N_FOURIER_GROUPS * FOURIER_GROUP_DIM
N_GMLP_HEADS = 4
GMLP_HEAD_DIM = 64
GMLP_WIDTH = N_GMLP_HEADS * GMLP_HEAD_DIM
CHUNK = 128
ROPE_THETA = 10000.0
LN_EPS = 1e-5
RMS_EPS = 1e-5
QK_SCALE = HEAD_DIM ** -0.5
LOG2_E = math.log2(math.e)

_SEG_NAMES = ("q", "k", "v", "g_att", "f", "g_f", "u", "vm", "g_m")
_SEG_WIDTHS = (ATT_WIDTH,) * 4 + (FOURIER_WIDTH,) * 2 + (GMLP_WIDTH,) * 3
_SEG_START = dict(zip(_SEG_NAMES, np.concatenate([[0], np.cumsum(_SEG_WIDTHS)[:-1]]).tolist()))
_SEG_WIDTH = dict(zip(_SEG_NAMES, _SEG_WIDTHS))

VMEM_LIMIT_BYTES = 52 * 1024 * 1024
VMEM_LIMIT_BOUNDARY_BYTES = 58 * 1024 * 1024
MOD_ROWS = 16
SUB_TILE = 512


def _cparams(*sem, vmem=VMEM_LIMIT_BYTES):
    return pltpu.CompilerParams(dimension_semantics=sem, vmem_limit_bytes=vmem)


def _silu(x):
    return x * jax.nn.sigmoid(x)


def _row_tile(l, want):
    t = min(l, want)
    assert l % t == 0
    return t


def _mod_kernel(c_ref, w_ref, b_ref, o_ref):
    s = _silu(c_ref[...])
    o_ref[0] = jnp.dot(s, w_ref[0], precision=lax.Precision.HIGHEST,
                       preferred_element_type=F32) + b_ref[0]


def _modulation(cvec, w_ada, b_ada):
    depth, d, d3 = w_ada.shape
    tn = 1024
    return pl.pallas_call(
        _mod_kernel,
        grid=(depth, d3 // tn),
        in_specs=[pl.BlockSpec((MOD_ROWS, d), lambda l, j: (0, 0)),
                  pl.BlockSpec((1, d, tn), lambda l, j: (l, 0, j)),
                  pl.BlockSpec((1, 1, tn), lambda l, j: (l, 0, j))],
        out_specs=pl.BlockSpec((1, MOD_ROWS, tn), lambda l, j: (l, 0, j)),
        out_shape=jax.ShapeDtypeStruct((depth, MOD_ROWS, d3), F32),
        name="adaln_mod",
        compiler_params=_cparams("arbitrary", "arbitrary"),
    )(cvec, w_ada, b_ada.reshape(depth, 1, d3))


def _chanmap_kernel(cc_ref, ss_ref, w_ref, a_ref, b_ref, *, scale):
    w = w_ref[0]
    a_ref[0] = (jnp.dot(cc_ref[...], w, precision=lax.Precision.HIGHEST,
                        preferred_element_type=F32) * scale).astype(BF16)
    b_ref[0] = (jnp.dot(ss_ref[...], w, precision=lax.Precision.HIGHEST,
                        preferred_element_type=F32) * scale).astype(BF16)


def _fourier_channel_maps(fourier_w, scale):
    depth = fourier_w.shape[0]
    n = np.arange(FOURIER_GROUP_DIM)
    ang = 2.0 * np.pi * ((n[:, None] * n[None, :]) % FOURIER_GROUP_DIM) / FOURIER_GROUP_DIM
    eye = np.eye(N_FOURIER_GROUPS)
    cc = jnp.asarray(np.kron(eye, np.cos(ang)), F32)
    ss = jnp.asarray(np.kron(eye, np.sin(ang)), F32)
    wbd = jnp.einsum("gh,lgce->lgche", jnp.eye(N_FOURIER_GROUPS, dtype=F32),
                     fourier_w.astype(F32)).reshape(depth, FOURIER_WIDTH, FOURIER_WIDTH)
    w_spec = pl.BlockSpec((1, FOURIER_WIDTH, FOURIER_WIDTH), lambda l: (l, 0, 0))
    c_spec = pl.BlockSpec((FOURIER_WIDTH, FOURIER_WIDTH), lambda l: (0, 0))
    shp = jax.ShapeDtypeStruct((depth, FOURIER_WIDTH, FOURIER_WIDTH), BF16)
    return pl.pallas_call(
        functools.partial(_chanmap_kernel, scale=scale),
        grid=(depth,),
        in_specs=[c_spec, c_spec, w_spec],
        out_specs=[w_spec, w_spec],
        out_shape=[shp, shp],
        name="fourier_chanmap",
        compiler_params=_cparams("arbitrary"),
    )(cc, ss, wbd)


def _dft_kernel(ca_ref, sa_ref, cb_ref, sb_ref, g_ref, *, l):
    ca, sa = ca_ref[0], sa_ref[0]
    cb, sb = cb_ref[...], sb_ref[...]
    norm = 1.0 / math.sqrt(l)
    g_ref[:, 0:l] = ((ca * cb - sa * sb) * norm).astype(BF16)
    g_ref[:, l:2 * l] = ((sa * cb + ca * sb) * -norm).astype(BF16)


def _dft_matrix(l):
    assert l % (2 * GRID_W) == 0
    na = l // GRID_W
    k = np.concatenate([np.arange(0, l, 2), np.arange(1, l, 2)]).astype(np.int64)
    ang_a = ((np.arange(na // 2)[:, None] * k[None, :]) % na) * (2.0 * np.pi / na)
    ang_b = ((np.arange(GRID_W)[:, None] * k[None, :]) % l) * (2.0 * np.pi / l)
    ca = jnp.asarray(np.cos(ang_a).reshape(na // 2, 1, l), F32)
    sa = jnp.asarray(np.sin(ang_a).reshape(na // 2, 1, l), F32)
    cb, sb = jnp.asarray(np.cos(ang_b), F32), jnp.asarray(np.sin(ang_b), F32)
    a_spec = pl.BlockSpec((1, 1, l), lambda i: (i, 0, 0))
    b_spec = pl.BlockSpec((GRID_W, l), lambda i: (0, 0))
    return pl.pallas_call(
        functools.partial(_dft_kernel, l=l),
        grid=(na // 2,),
        in_specs=[a_spec, a_spec, b_spec, b_spec],
        out_specs=pl.BlockSpec((GRID_W, 2 * l), lambda i: (i, 0)),
        out_shape=jax.ShapeDtypeStruct((l // 2, 2 * l), BF16),
        name=f"dft_matrix_{l}",
        compiler_params=_cparams("arbitrary"),
    )(ca, sa, cb, sb)


def _layernorm(v, eps=LN_EPS):
    mu = jnp.mean(v, axis=-1, keepdims=True)
    vc = v - mu
    return vc * lax.rsqrt(jnp.mean(vc * vc, axis=-1, keepdims=True) + eps)


def _boundary_kernel(*refs, has_out, has_in, rope, emit_kv, alpha, n_sub):
    refs = list(refs)
    take = lambda n: [refs.pop(0) for _ in range(n)]
    (x_ref,) = take(1)
    if has_out:
        att_ref, fo_ref, gmo_ref, modo_ref, wout_ref, lg_ref, lb_ref = take(7)
    if has_in:
        modi_ref, win_ref, fa_ref, fb_ref, gw_ref, gbias_ref, avg_ref, sel_ref = take(8)
        if rope:
            cos_ref, sa_ref, sb_ref = take(3)
        if emit_kv:
            take(2)
    if has_out:
        (xo_ref,) = take(1)
    if has_in:
        q_ref, k_ref, vt_ref, ga_ref, uf_ref, gf_ref, gm_ref = take(7)
        if emit_kv:
            kf_ref, vf_ref = take(2)

    tm = x_ref.shape[1]

    def rotate(z, rs):
        return (z * cos_ref[rs, :] + pltpu.roll(z, HEAD_W - 16, 1) * sa_ref[rs, :]
                + pltpu.roll(z, 16, 1) * sb_ref[rs, :])

    def sub_tile(r0, ts):
        rs = slice(r0, r0 + ts)
        st = {}

        def s_finish():
            x = x_ref[0, rs, :]
            if has_out:
                a0, a1 = ATT_WIDTH, ATT_WIDTH + FOURIER_WIDTH
                y = (jnp.dot(att_ref[0, rs, :], wout_ref[0, 0:a0, :], preferred_element_type=F32)
                     + jnp.dot(fo_ref[0, rs, :], wout_ref[0, a0:a1, :], preferred_element_type=F32)
                     + jnp.dot(gmo_ref[0, rs, :], wout_ref[0, a1:, :], preferred_element_type=F32))
                gate = modo_ref[0, 0, 2:3, :] * (1.0 / alpha)
                x = _layernorm(x + gate * y, LN_EPS / (alpha * alpha)) * lg_ref[0] + lb_ref[0]
                xo_ref[0, rs, :] = x
            if has_in:
                shift, scale = modi_ref[0, 0, 0:1, :], modi_ref[0, 0, 1:2, :]
                st["h"] = (_layernorm(x) * (1.0 + scale) + shift).astype(BF16)

        def proj(name):
            c0 = _SEG_START[name]
            return jnp.dot(st["h"], win_ref[0, :, c0:c0 + _SEG_WIDTH[name]], preferred_element_type=F32)

        def s_qkv():
            zq, zk, zv = proj("q"), proj("k"), proj("v")
            for hh in range(N_ATT_HEADS):
                sl = slice(hh * HEAD_W, (hh + 1) * HEAD_W)
                qh, kh = zq[:, sl], zk[:, sl]
                if emit_kv:
                    kf_ref[0, 0, hh, rs, :] = kh
                    vf_ref[0, 0, hh, rs, :] = zv[:, sl]
                if rope:
                    qh, kh = rotate(qh, rs), rotate(kh, rs)
                q_ref[0, rs, sl] = (qh * (QK_SCALE * LOG2_E)).astype(BF16)
                k_ref[0, rs, sl] = kh.astype(BF16)
            vt_ref[0, :, rs] = zv.T.astype(BF16)

        def s_fourier():
            ga_ref[0, rs, :] = proj("g_att")
            f = jnp.dot(sel_ref[...], proj("f").astype(BF16), preferred_element_type=F32).astype(BF16)
            hs = slice(r0 // 2, (r0 + ts) // 2)
            for j, fm_ref in enumerate((fa_ref, fb_ref)):
                u = jnp.dot(f, fm_ref[0], preferred_element_type=F32).astype(BF16)
                uf_ref[0, j, 0, hs, :] = u[:ts // 2]
                uf_ref[0, j, 1, hs, :] = u[ts // 2:]
            gf_ref[0, rs, :] = proj("g_f")

        def s_gmlp():
            zu, zvm, zgm = proj("u"), proj("vm"), proj("g_m")
            avg = avg_ref[...]

            def group_mean(v):
                hi = v.astype(BF16)
                lo = (v - hi.astype(F32)).astype(BF16)
                return (jnp.dot(hi, avg, preferred_element_type=F32)
                        + jnp.dot(lo, avg, preferred_element_type=F32))

            lane = lax.broadcasted_iota(jnp.int32, (CHUNK, GMLP_WIDTH), 1)
            head_mask = [(lane >= hh * GMLP_HEAD_DIM) & (lane < (hh + 1) * GMLP_HEAD_DIM)
                         for hh in range(N_GMLP_HEADS)]
            vc_all = zvm - group_mean(zvm)
            vn_all = (vc_all * lax.rsqrt(group_mean(vc_all * vc_all) + LN_EPS)).astype(BF16)
            for c in range(ts // CHUNK):
                rows = slice(c * CHUNK, (c + 1) * CHUNK)
                vn = vn_all[rows]
                stacked = jnp.concatenate(
                    [jnp.where(head_mask[hh], vn, jnp.zeros_like(vn)) for hh in range(N_GMLP_HEADS)], axis=0)
                s = jnp.dot(gw_ref[0], stacked, preferred_element_type=F32) + gbias_ref[0]
                gm_ref[0, r0 + c * CHUNK:r0 + (c + 1) * CHUNK, :] = (
                    zu[rows] * s * _silu(zgm[rows])).astype(BF16)

        return [s_finish] + ([s_qkv, s_fourier, s_gmlp] if has_in else [])

    ts = tm // n_sub
    subs = [sub_tile(j * ts, ts) for j in range(n_sub)]
    n_stage = len(subs[0])
    for step in range(n_stage + n_sub - 1):
        for j in reversed(range(n_sub)):
            if 0 <= step - j < n_stage:
                subs[j][step - j]()


def _boundary(x, branches, mod3, per_batch_mod, p, out_layer, in_layer, rope_tabs, kv_buf, alpha, tm):
    b, l, d = x.shape
    tm = _row_tile(l, tm)
    has_out, has_in = out_layer is not None, in_layer is not None
    emit_kv, aliases = kv_buf is not None and has_in, {}
    n_sub = tm // SUB_TILE if tm > SUB_TILE else 1
    tsel = tm // n_sub
    row = lambda w: pl.BlockSpec((1, tm, w), lambda bi, i: (bi, i, 0))
    const2 = lambda shp: pl.BlockSpec(shp, lambda bi, i: (0, 0))
    per_layer = lambda shp, layer: pl.BlockSpec((1,) + shp, lambda bi, i: (layer, 0, 0),
                                                pipeline_mode=pl.Buffered(1))
    mod_spec = lambda layer: pl.BlockSpec(
        (1, 1, 3, d), (lambda bi, i: (layer, bi, 0, 0)) if per_batch_mod else (lambda bi, i: (layer, 0, 0, 0)))
    sds = jax.ShapeDtypeStruct
    in_specs, args, out_specs, out_shape = [row(d)], [x], [], []
    if has_out:
        in_specs += [row(ATT_WIDTH), row(FOURIER_WIDTH), row(GMLP_WIDTH), mod_spec(out_layer),
                     per_layer((d, d), out_layer), per_layer((1, d), out_layer), per_layer((1, d), out_layer)]
        args += list(branches) + [mod3, p["w_out"], p["ln_g"], p["ln_b"]]
        out_specs += [row(d)]
        out_shape += [sds((b, l, d), F32)]
    if has_in:
        n_in = p["w_in"].shape[2]
        in_specs += [mod_spec(in_layer), per_layer((d, n_in), in_layer),
                     per_layer((FOURIER_WIDTH, FOURIER_WIDTH), in_layer),
                     per_layer((FOURIER_WIDTH, FOURIER_WIDTH), in_layer),
                     per_layer((CHUNK, N_GMLP_HEADS * CHUNK), in_layer),
                     per_layer((CHUNK, GMLP_WIDTH), in_layer), const2((GMLP_WIDTH, GMLP_WIDTH)),
                     const2((tsel, tsel))]
        sel = np.zeros((tsel, tsel), np.float32)
        sel[np.arange(tsel), np.concatenate([np.arange(0, tsel, 2), np.arange(1, tsel, 2)])] = 1.0
        args += [mod3, p["w_in"], p["fa"], p["fb"], p["gmlp_w"], p["gmlp_bias"], p["avg"],
                 jnp.asarray(sel, BF16)]
        if rope_tabs is not None:
            in_specs += [pl.BlockSpec((tm, HEAD_W), lambda bi, i: (i, 0))] * 3
            args += list(rope_tabs)
        out_shape += [sds((b, l, ATT_WIDTH), BF16)] * 2 + [
            sds((b, ATT_WIDTH, l), BF16), sds((b, l, ATT_WIDTH), F32),
            sds((b, 2, 2, l // 2, FOURIER_WIDTH), BF16), sds((b, l, FOURIER_WIDTH), F32),
            sds((b, l, GMLP_WIDTH), BF16)]
        out_specs += [row(ATT_WIDTH)] * 2 + [
            pl.BlockSpec((1, ATT_WIDTH, tm), lambda bi, i: (bi, 0, i)), row(ATT_WIDTH),
            pl.BlockSpec((1, 2, 2, tm // 2, FOURIER_WIDTH), lambda bi, i: (bi, 0, 0, i, 0)),
            row(FOURIER_WIDTH), row(GMLP_WIDTH)]
        if emit_kv:
            aliases = {len(args): len(out_shape), len(args) + 1: len(out_shape) + 1}
            in_specs += [pl.BlockSpec(memory_space=pl.ANY)] * 2
            args += list(kv_buf)
            out_shape += [sds(kv_buf[0].shape, F32)] * 2
            out_specs += [pl.BlockSpec((1, 1, N_ATT_HEADS, tm, HEAD_W),
                                       lambda bi, i: (bi, in_layer, 0, i, 0))] * 2
    kern = functools.partial(_boundary_kernel, has_out=has_out, has_in=has_in,
                             rope=rope_tabs is not None, emit_kv=emit_kv, alpha=alpha, n_sub=n_sub)

    return pl.pallas_call(
        kern, grid=(b, l // tm), in_specs=in_specs, out_specs=out_specs, out_shape=out_shape,
        input_output_aliases=aliases,
        name=f"boundary_{l}_{int(has_out)}{int(has_in)}",
        compiler_params=_cparams("parallel", "parallel", vmem=VMEM_LIMIT_BOUNDARY_BYTES),
    )(*args)


def _attn_kernel(*refs, lam_init, has_cache, ck, n_new, past, tq, nq, hp):
    if has_cache:
        (q_ref, k_ref, vt_ref, pk_ref, pv_ref, g_ref, lq1_ref, lk1_ref, lq2_ref, lk2_ref,
         sw_ref, o_ref, p_ref, lv_ref, pend_ref, bad_ref, pvt_ref) = refs
    else:
        (q_ref, k_ref, vt_ref, g_ref, lq1_ref, lk1_ref, lq2_ref, lk2_ref,
         sw_ref, o_ref, p_ref, lv_ref, pend_ref, bad_ref) = refs

    chunks = []
    if has_cache:
        chunks.append((True, 0, 0, past))
    for j in range(n_new):
        chunks.append((False, j * ck, past + j * ck, ck))

    if has_cache:
        pvt_ref[...] = pv_ref[0, 0, 0].T.astype(BF16)

    lam = (jnp.exp(jnp.sum(lq1_ref[0] * lk1_ref[0], axis=-1, keepdims=True))
           - jnp.exp(jnp.sum(lq2_ref[0] * lk2_ref[0], axis=-1, keepdims=True)) + lam_init)

    def rows_of(t):
        return pl.ds(pl.multiple_of(t * tq, tq), tq)

    def head(hh):
        cs = slice(hh * HEAD_W, (hh + 1) * HEAD_W)
        so = 2 * hh

        def keys(c):
            return pk_ref[0, 0, 0].astype(BF16) if c[0] else k_ref[0, c[1]:c[1] + c[3], cs]

        def vals_t(c):
            return pvt_ref[...] if c[0] else vt_ref[0, cs, c[1]:c[1] + c[3]]

        def scores_t(qq, c):
            return lax.dot_general(keys(c), qq, (((1,), (1,)), ((), ())), preferred_element_type=F32)

        def stacked_q(t):
            q = q_ref[0, rows_of(t), cs]
            lane = lax.broadcasted_iota(jnp.int32, q.shape, 1)
            zero = jnp.zeros_like(q)
            return jnp.concatenate([jnp.where(lane < HEAD_DIM, q, zero),
                                    jnp.where(lane >= HEAD_DIM, q, zero)], axis=0)

        def scores_phase(t, slot):
            qq = stacked_q(t)
            s0 = scores_t(qq, chunks[0])
            stab = jnp.max(s0, axis=0, keepdims=True)
            lv = jnp.zeros((1, 2 * tq), F32)
            for ci, c in enumerate(chunks):
                s = s0 if ci == 0 else scores_t(qq, c)
                p = jnp.exp2(s - stab)
                p_ref[so + slot, c[2]:c[2] + c[3], :] = p
                lv = lv + jnp.sum(p, axis=0, keepdims=True)
            lv_ref[so + slot] = jnp.broadcast_to(lv, (8, 2 * tq))

        def values_phase(slot):
            lsum = lv_ref[so + slot][0:1, :]
            l1, l2 = lsum[:, :tq], lsum[:, tq:]
            rho = lam * l1 / l2
            acc = jnp.zeros((HEAD_W, tq), F32)
            for c in chunks:
                a = (p_ref[so + slot, c[2]:c[2] + c[3], 0:tq]
                     - rho * p_ref[so + slot, c[2]:c[2] + c[3], tq:2 * tq]).astype(BF16)
                acc = acc + jnp.dot(vals_t(c), a, preferred_element_type=F32)
            return (acc / l1).T

        def finish(o, rows):
            o = (o * lax.rsqrt(jnp.mean(o * o, axis=-1, keepdims=True) + RMS_EPS)
                 * sw_ref[0] * (1.0 - lam_init))
            return o * _silu(g_ref[0, rows, cs])

        def emit(t, slot):
            rows = rows_of(t)
            out = finish(pend_ref[so + slot], rows)
            o_ref[0, rows, cs] = out.astype(BF16)
            bad_ref[hh * nq + t] = jnp.sum(jnp.where(jnp.isfinite(out), 0.0, 1.0))

        def keep(slot, value_fn):
            pend_ref[so + slot] = value_fn()

        def fused(t, slot_s, slot_v):
            qq = stacked_q(t)
            s0 = scores_t(qq, chunks[0])
            stab = jnp.max(s0, axis=0, keepdims=True)
            lv = jnp.zeros((1, 2 * tq), F32)
            lsum = lv_ref[so + slot_v][0:1, :]
            l1, l2 = lsum[:, :tq], lsum[:, tq:]
            rho = lam * l1 / l2
            acc = jnp.zeros((HEAD_W, tq), F32)
            for ci, c in enumerate(chunks):
                s = s0 if ci == 0 else scores_t(qq, c)
                p = jnp.exp2(s - stab)
                p_ref[so + slot_s, c[2]:c[2] + c[3], :] = p
                lv = lv + jnp.sum(p, axis=0, keepdims=True)
                a = (p_ref[so + slot_v, c[2]:c[2] + c[3], 0:tq]
                     - rho * p_ref[so + slot_v, c[2]:c[2] + c[3], tq:2 * tq]).astype(BF16)
                acc = acc + jnp.dot(vals_t(c), a, preferred_element_type=F32)
            lv_ref[so + slot_s] = jnp.broadcast_to(lv, (8, 2 * tq))
            pend_ref[so + slot_v] = (acc / l1).T

        def pair(i, carry):
            t = 2 * i + 2
            emit(t - 2, 0)
            fused(t, 0, 1)
            emit(t - 1, 1)
            fused(t + 1, 1, 0)
            return carry

        stages = [lambda: scores_phase(0, 0), lambda: fused(1, 1, 0)]
        if nq > 2:
            stages.append(lambda: lax.fori_loop(0, nq // 2 - 1, pair, 0))
        stages += [lambda: emit(nq - 2, 0), lambda: keep(1, lambda: values_phase(1)),
                   lambda: emit(nq - 1, 1)]

        def exact():
            def exact_tile(t, carry):
                @pl.when(bad_ref[hh * nq + t] > 0.0)
                def _():
                    qq = stacked_q(t)
                    m = None
                    for c in chunks:
                        s = scores_t(qq, c)
                        p_ref[so, c[2]:c[2] + c[3], :] = s
                        cm = jnp.max(s, axis=0, keepdims=True)
                        m = cm if m is None else jnp.maximum(m, cm)
                    lv = jnp.zeros((1, 2 * tq), F32)
                    for c in chunks:
                        p = jnp.exp2(p_ref[so, c[2]:c[2] + c[3], :] - m)
                        p_ref[so, c[2]:c[2] + c[3], :] = p
                        lv = lv + jnp.sum(p, axis=0, keepdims=True)
                    lv_ref[so] = jnp.broadcast_to(lv, (8, 2 * tq))
                    rows = rows_of(t)
                    o_ref[0, rows, cs] = finish(values_phase(0), rows).astype(BF16)
                return carry

            lax.fori_loop(0, nq, exact_tile, 0)

        return stages, exact

    heads = [head(hh) for hh in range(hp)]
    for stage in zip(*[stages for stages, _ in heads]):
        for run in stage:
            run()
    for _, exact in heads:
        exact()


def _attention(q, k, v, g_att, cache, layer, lam_params, subln_w, lam_init, tq, ck, hp):
    b, l, _ = q.shape
    tq = _row_tile(l, tq)
    ck = _row_tile(l, ck)
    nq = l // tq
    assert nq >= 2 and nq % 2 == 0 and N_ATT_HEADS % hp == 0
    has_cache = cache is not None
    assert not (has_cache and hp != 1)
    past = cache[0].shape[3] if has_cache else 0
    lk = past + l
    seq = pl.BlockSpec((1, l, hp * HEAD_W), lambda bi, h: (bi, 0, h))
    vec = lambda w: pl.BlockSpec((1, 1, w), lambda bi, h: (layer, 0, 0))
    vt_spec = pl.BlockSpec((1, hp * HEAD_W, l), lambda bi, h: (bi, h, 0))
    in_specs = [seq, seq, vt_spec]
    args = [q, k, v]
    if has_cache:
        cspec = pl.BlockSpec((1, 1, 1, past, HEAD_W), lambda bi, h: (bi, layer, h, 0, 0))
        in_specs += [cspec, cspec]
        args += list(cache)
    in_specs += [seq] + [vec(HEAD_DIM)] * 4 + [vec(HEAD_W)]
    args += [g_att] + list(lam_params) + [subln_w]
    return pl.pallas_call(
        functools.partial(_attn_kernel, lam_init=lam_init, has_cache=has_cache, ck=ck,
                          n_new=l // ck, past=past, tq=tq, nq=nq, hp=hp),
        grid=(b, N_ATT_HEADS // hp),
        in_specs=in_specs,
        out_specs=seq,
        out_shape=jax.ShapeDtypeStruct((b, l, ATT_WIDTH), BF16),
        scratch_shapes=[pltpu.VMEM((2 * hp, lk, 2 * tq), F32), pltpu.VMEM((2 * hp, 8, 2 * tq), F32),
                        pltpu.VMEM((2 * hp, tq, HEAD_W), F32), pltpu.SMEM((hp * nq,), F32)]
        + ([pltpu.VMEM((HEAD_W, past), BF16)] if has_cache else []),
        name=f"attn_{l}",
        compiler_params=_cparams("parallel", "parallel"),
    )(*args)


def _fourier_kernel(g_ref, u_ref, gate_ref, o_ref, *, nb, l):
    h = l // 2
    dot = functools.partial(jnp.dot, preferred_element_type=F32)
    for j in range(nb):
        even = dot(g_ref[:, 0:h], u_ref[j, 0, 0]) + dot(g_ref[:, l:l + h], u_ref[j, 1, 0])
        odd = dot(g_ref[:, h:l], u_ref[j, 0, 1]) + dot(g_ref[:, l + h:2 * l], u_ref[j, 1, 1])
        o_ref[j, 0] = ((even + odd) * _silu(gate_ref[j, 0])).astype(BF16)
        o_ref[j, 1] = ((even - odd) * _silu(gate_ref[j, 1])).astype(BF16)


def _fourier(gmat, uf, g_f, tr, nb):
    b, l, w = g_f.shape
    assert b % nb == 0
    h = l // 2
    tr = _row_tile(h, tr)
    halves = pl.BlockSpec((nb, 2, tr, w), lambda bi, i: (bi, 0, i, 0))
    out = pl.pallas_call(
        functools.partial(_fourier_kernel, nb=nb, l=l),
        grid=(b // nb, h // tr),
        in_specs=[pl.BlockSpec((tr, 2 * l), lambda bi, i: (i, 0)),
                  pl.BlockSpec((nb, 2, 2, h, w), lambda bi, i: (bi, 0, 0, 0, 0)),
                  halves],
        out_specs=halves,
        out_shape=jax.ShapeDtypeStruct((b, 2, h, w), BF16),
        name=f"fourier_{l}",
        compiler_params=_cparams("parallel", "arbitrary"),
    )(gmat, uf, g_f.reshape(b, 2, h, w))
    return out.reshape(b, l, w)


def _rope_tables(l):
    n_freq = HEAD_DIM // 4
    rows = l // GRID_W
    row = np.repeat(np.arange(rows, dtype=np.float64), GRID_W)
    col = np.tile(np.arange(GRID_W, dtype=np.float64), rows)
    inv = ROPE_THETA ** (-np.arange(n_freq, dtype=np.float64) / n_freq)
    ang_r, ang_c = row[:, None] * inv, col[:, None] * inv
    zeros = np.zeros_like(ang_r)

    def per_head(r_lo, r_hi, c_lo, c_hi):
        one = np.concatenate([r_lo, r_hi, c_lo, c_hi], axis=-1)
        return jnp.asarray(np.concatenate([one, one], axis=-1), F32)

    cr, sr, cc, sc = np.cos(ang_r), np.sin(ang_r), np.cos(ang_c), np.sin(ang_c)
    cos_t = per_head(cr, cr, cc, cc)
    sa_t = per_head(-sr, zeros, -sc, zeros)
    sb_t = per_head(zeros, sr, zeros, sc)
    return cos_t, sa_t, sb_t


def _run_group(x, mod3, per_batch_mod, p, gmat, rope_tabs, cache, emit_kv, alpha, tiles):
    depth = p["w_in"].shape[0]
    kv_buf = None
    if emit_kv:
        kv_shape = (x.shape[0], depth, N_ATT_HEADS, x.shape[1], HEAD_W)
        kv_buf = (jnp.zeros(kv_shape, F32), jnp.zeros(kv_shape, F32))
    outs = _boundary(x, None, mod3, per_batch_mod, p, None, 0, rope_tabs, kv_buf, alpha, tiles["tm"])
    for layer in range(depth):
        lam_init = 0.8 - 0.6 * math.exp(-0.3 * layer)
        q, k, vt, g_att, uf, g_f, gm = outs[:7]
        if emit_kv:
            kv_buf = tuple(outs[7:9])
        att = _attention(q, k, vt, g_att, cache, layer, p["lam"], p["subln_w"], lam_init,
                         tiles["tq"], tiles["ck"], tiles["hp"])
        fo = _fourier(gmat, uf, g_f, tiles["tr"], tiles["nb"])
        nxt = layer + 1 if layer + 1 < depth else None
        res = _boundary(x, (att, fo, gm), mod3, per_batch_mod, p, layer, nxt, rope_tabs, kv_buf,
                        alpha, tiles["tm"])
        x, outs = res[0], res[1:]
    return x, kv_buf


def kernel(x_prompt, x_sample, c, cache_k, cache_v, c_ctx, w_ada, b_ada, w_in, w_out,
           lam_q1, lam_k1, lam_q2, lam_k2, subln_w, fourier_w, gmlp_ws, gmlp_bs, ln_g, ln_b):
    depth, d, _ = w_ada.shape
    nb_lat = c.shape[0]
    l_ctx, l_lat = x_prompt.shape[1], x_sample.shape[1]
    assert 1 + nb_lat <= MOD_ROWS
    alpha = (2.0 * depth) ** 0.25

    cvec = jnp.concatenate([c_ctx[None, :], c, jnp.zeros((MOD_ROWS - 1 - nb_lat, d), F32)], axis=0)
    mod = _modulation(cvec, w_ada, b_ada)

    fa, fb = _fourier_channel_maps(fourier_w, 1.0 / math.sqrt(FOURIER_GROUP_DIM))
    params = {
        "w_in": w_in.astype(BF16), "w_out": w_out.astype(BF16), "fa": fa, "fb": fb,
        "lam": tuple(t[:, None, :] for t in (lam_q1, lam_k1, lam_q2, lam_k2)),
        "subln_w": subln_w[:, None, :], "ln_g": ln_g[:, None, :], "ln_b": ln_b[:, None, :],
        "gmlp_w": gmlp_ws.astype(BF16).transpose(0, 2, 1, 3).reshape(depth, CHUNK, N_GMLP_HEADS * CHUNK),
        "gmlp_bias": jnp.repeat(gmlp_bs.astype(F32).transpose(0, 2, 1), GMLP_HEAD_DIM, axis=2),
        "avg": jnp.asarray(np.kron(np.eye(N_GMLP_HEADS), np.full((GMLP_HEAD_DIM,) * 2, 1.0 / GMLP_HEAD_DIM)), BF16),
    }
    ctx_tiles = dict(tm=256, tq=128, ck=256, tr=256, nb=2, hp=N_ATT_HEADS)
    lat_tiles = dict(tm=1024, tq=256, ck=512, tr=512, nb=2, hp=1)

    xp, (new_k, new_v) = _run_group(x_prompt, mod[:, 0:1].reshape(depth, 1, 3, d), False, params,
                                    _dft_matrix(l_ctx), None, None, True, alpha, ctx_tiles)

    xs, _ = _run_group(x_sample, mod[:, 1:1 + nb_lat].reshape(depth, nb_lat, 3, d), True, params,
                       _dft_matrix(l_lat), _rope_tables(l_lat), (cache_k, cache_v), False, alpha,
                       lat_tiles)
    return (xp, xs, new_k, new_v)
```
